```python
import math
import jax
import jax.numpy as jnp
from jax import lax
import numpy as np

D_MODEL = 2048
BATCH = 4
SEQ = 4096
DEPTH = 1

SSD_WIDTH = D_MODEL
SSD_HEADDIM = 64
SSD_HEADS = SSD_WIDTH // SSD_HEADDIM
SSD_GROUPS = 4
SSD_STATE = 128
SSD_CONV_K = 3
SSD_CHUNK = 128
SC_WIDTH = D_MODEL
SC_GROUPS = 32
SC_CONV_K = 3
MIX_WIDTH = SSD_WIDTH + SC_WIDTH
XBC_WIDTH = SSD_WIDTH + 2 * SSD_GROUPS * SSD_STATE
IN_COLS = SSD_WIDTH + XBC_WIDTH + 2 * SSD_HEADS + 3 * SC_WIDTH

PEER_HEADS = 8
N_KEYS = 128
N_EXPERTS = N_KEYS * N_KEYS
PEER_TOPK = 16
D_QUERY = 512
PEER_TOKEN_BLOCK = 128

DEEPNORM_ALPHA = (2.0 * DEPTH) ** 0.25
DEEPNORM_BETA = (8.0 * DEPTH) ** -0.25
NORM_EPS = 1e-5
DT_MIN = 0.001
DT_MAX = 0.1
A_MIN = 1.0
A_MAX = 16.0

kernel_name = 'hymba_bissd_shortconv_peer_deepnorm_adaln'


def layer_norm(x, g, b):
    xf = x.astype(jnp.float32)
    mu = jnp.mean(xf, axis=-1, keepdims=True)
    xc = xf - mu
    var = jnp.mean(xc * xc, axis=-1, keepdims=True)
    return (xc * lax.rsqrt(var + NORM_EPS) * g + b).astype(x.dtype)


def group_rms_norm(y, g, n_groups):
    yf = y.astype(jnp.float32).reshape(y.shape[:-1] + (n_groups, y.shape[-1] // n_groups))
    yf = yf * lax.rsqrt(jnp.mean(yf * yf, axis=-1, keepdims=True) + NORM_EPS)
    return (yf.reshape(y.shape) * g).astype(y.dtype)


def centred_depthwise_conv(u, w):
    k = w.shape[0]
    pad = k // 2
    s = u.shape[1]
    up = jnp.pad(u, ((0, 0), (pad, pad), (0, 0)))
    out = up[:, 0:s] * w[0]
    for j in range(1, k):
        out = out + up[:, j:j + s] * w[j]
    return out


def ssd_chunked(x, dt, a, bm, cm):
    b, l, h, p = x.shape
    g, n = bm.shape[-2:]
    r = h // g
    cs = SSD_CHUNK
    nc = l // cs
    xd = (x * dt[..., None]).reshape(b, nc, cs, g, r, p)
    adt = (dt * a).reshape(b, nc, cs, g, r)
    acs = jnp.moveaxis(jnp.cumsum(adt, axis=2), 2, -1)
    bc = bm.reshape(b, nc, cs, g, n)
    cc = cm.reshape(b, nc, cs, g, n)
    mask = jnp.tril(jnp.ones((cs, cs), dtype=bool))
    seg = acs[..., :, None] - acs[..., None, :]
    lmat = jnp.exp(jnp.where(mask, seg, -jnp.inf))
    cb = jnp.einsum('bclgn,bcsgn->bcgls', cc, bc)
    y_diag = jnp.einsum('bcgls,bcgrls,bcsgrp->bclgrp', cb, lmat, xd)
    decay_states = jnp.exp(acs[..., -1:] - acs)
    states = jnp.einsum('bclgn,bcgrl,bclgrp->bcgrpn', bc, decay_states, xd)
    chunk_decay = jnp.exp(acs[..., -1])

    def step(h_prev, inp):
        dec, st = inp
        return dec[..., None, None] * h_prev + st, h_prev

    h0 = jnp.zeros((b, g, r, p, n), dtype=states.dtype)
    _, h_in = lax.scan(step, h0, (jnp.moveaxis(chunk_decay, 1, 0), jnp.moveaxis(states, 1, 0)))
    h_in = jnp.moveaxis(h_in, 0, 1)
    y_off = jnp.einsum('bclgn,bcgrpn,bcgrl->bclgrp', cc, h_in, jnp.exp(acs))
    return (y_diag + y_off).reshape(b, l, h, p)


def ssd_bidirectional(xs, bm, cm, dt_f, dt_b, a_f, a_b, d_skip):
    y_f = ssd_chunked(xs, dt_f, a_f, bm, cm)
    y_b = jnp.flip(ssd_chunked(jnp.flip(xs, 1), jnp.flip(dt_b, 1), a_b, jnp.flip(bm, 1), jnp.flip(cm, 1)), 1)
    return y_f + y_b + xs * d_skip[:, None]


def hybrid_mixer(h, w_in, conv_ssd_w, conv_ssd_b, dt_bias_f, dt_bias_b, a_log_f, a_log_b,
                 d_skip, ssd_norm_g, short_conv_w, sc_norm_g, w_out):
    b, s, _ = h.shape
    proj = h @ w_in
    o1 = SSD_WIDTH
    o2 = o1 + XBC_WIDTH
    o3 = o2 + 2 * SSD_HEADS
    o4 = o3 + SC_WIDTH
    o5 = o4 + SC_WIDTH
    z, xbc, dt_raw, g_b, g_c, v = jnp.split(proj, [o1, o2, o3, o4, o5], axis=-1)
    xbc = jax.nn.silu(centred_depthwise_conv(xbc, conv_ssd_w) + conv_ssd_b)
    gn = SSD_GROUPS * SSD_STATE
    xs, bm, cm = jnp.split(xbc, [SSD_WIDTH, SSD_WIDTH + gn], axis=-1)
    xs = xs.reshape(b, s, SSD_HEADS, SSD_HEADDIM)
    bm = bm.reshape(b, s, SSD_GROUPS, SSD_STATE)
    cm = cm.reshape(b, s, SSD_GROUPS, SSD_STATE)
    dt_raw = dt_raw.astype(jnp.float32)
    dt_f = jax.nn.softplus(dt_raw[..., :SSD_HEADS] + dt_bias_f)
    dt_b = jax.nn.softplus(dt_raw[..., SSD_HEADS:] + dt_bias_b)
    a_f = -jnp.exp(a_log_f.astype(jnp.float32))
    a_b = -jnp.exp(a_log_b.astype(jnp.float32))
    y_ssd = ssd_bidirectional(xs, bm, cm, dt_f, dt_b, a_f, a_b, d_skip).reshape(b, s, SSD_WIDTH)
    y_ssd = group_rms_norm(y_ssd * jax.nn.silu(z), ssd_norm_g, SSD_GROUPS)
    y_sc = g_b * centred_depthwise_conv(g_c * v, short_conv_w)
    y_sc = group_rms_norm(y_sc, sc_norm_g, SC_GROUPS)
    return jnp.concatenate([y_ssd, y_sc], axis=-1) @ w_out


def peer(h, w_query, sub_keys, expert_u, expert_v):
    b, s, d = h.shape
    t = b * s
    half = D_QUERY // 2
    q = (h @ w_query).reshape(b, s, PEER_HEADS, 2, half)
    s1 = jnp.einsum('bshd,hkd->bshk', q[..., 0, :], sub_keys[:, 0])
    s2 = jnp.einsum('bshd,hkd->bshk', q[..., 1, :], sub_keys[:, 1])
    v1, i1 = lax.top_k(s1, PEER_TOPK)
    v2, i2 = lax.top_k(s2, PEER_TOPK)
    cand = (v1[..., :, None] + v2[..., None, :]).reshape(b, s, PEER_HEADS, PEER_TOPK * PEER_TOPK)
    score, flat = lax.top_k(cand, PEER_TOPK)
    e1 = jnp.take_along_axis(i1, flat // PEER_TOPK, axis=-1)
    e2 = jnp.take_along_axis(i2, flat % PEER_TOPK, axis=-1)
    expert_idx = e1 * N_KEYS + e2
    gate = jax.nn.softmax(score.astype(jnp.float32), axis=-1)
    nb = t // PEER_TOKEN_BLOCK
    hk = PEER_HEADS * PEER_TOPK
    idx_blocks = expert_idx.reshape(nb, PEER_TOKEN_BLOCK, hk)
    gate_blocks = gate.reshape(nb, PEER_TOKEN_BLOCK, hk)
    x_blocks = h.reshape(nb, PEER_TOKEN_BLOCK, d)

    def block(args):
        xb, ib, gb = args
        u_sel = jnp.take(expert_u, ib, axis=0)
        act = jax.nn.gelu(jnp.einsum('tkd,td->tk', u_sel, xb), approximate=False)
        v_sel = jnp.take(expert_v, ib, axis=0)
        return jnp.einsum('tk,tkd->td', act * gb, v_sel)

    out = lax.map(block, (x_blocks, idx_blocks, gate_blocks))
    return out.reshape(b, s, d)


def setup_inputs(seed: int = 0) -> dict:
    key = jax.random.key(seed)
    ks = jax.random.split(key, 24)
    f32 = jnp.float32
    L = DEPTH

    def nrm(k, shape, std):
        return jax.random.normal(k, shape, f32) * std

    def dt_bias(k):
        dt = jnp.exp(jax.random.uniform(k, (L, SSD_HEADS), f32, math.log(DT_MIN), math.log(DT_MAX)))
        return dt + jnp.log(-jnp.expm1(-dt))

    return {
        'x': nrm(ks[0], (BATCH, SEQ, D_MODEL), 1.0),
        'c': nrm(ks[1], (BATCH, D_MODEL), 1.0),
        'w_ada': nrm(ks[2], (L, D_MODEL, 6 * D_MODEL), D_MODEL ** -0.5),
        'b_ada': nrm(ks[3], (L, 6 * D_MODEL), 0.01),
        'w_in': nrm(ks[4], (L, D_MODEL, IN_COLS), D_MODEL ** -0.5),
        'conv_ssd_w': nrm(ks[5], (L, SSD_CONV_K, XBC_WIDTH), SSD_CONV_K ** -0.5),
        'conv_ssd_b': nrm(ks[6], (L, XBC_WIDTH), 0.01),
        'dt_bias_f': dt_bias(ks[7]),
        'dt_bias_b': dt_bias(ks[8]),
        'a_log_f': jnp.log(jax.random.uniform(ks[9], (L, SSD_HEADS), f32, A_MIN, A_MAX)),
        'a_log_b': jnp.log(jax.random.uniform(ks[10], (L, SSD_HEADS), f32, A_MIN, A_MAX)),
        'd_skip': 1.0 + nrm(ks[11], (L, SSD_HEADS), 0.02),
        'ssd_norm_g': 1.0 + nrm(ks[12], (L, SSD_WIDTH), 0.02),
        'short_conv_w': nrm(ks[13], (L, SC_CONV_K, SC_WIDTH), SC_CONV_K ** -0.5),
        'sc_norm_g': 1.0 + nrm(ks[14], (L, SC_WIDTH), 0.02),
        'w_out': nrm(ks[15], (L, MIX_WIDTH, D_MODEL), DEEPNORM_BETA * MIX_WIDTH ** -0.5),
        'ln1_g': 1.0 + nrm(ks[16], (L, D_MODEL), 0.02),
        'ln1_b': nrm(ks[17], (L, D_MODEL), 0.01),
        'w_query': nrm(ks[18], (L, D_MODEL, PEER_HEADS * D_QUERY), D_MODEL ** -0.5),
        'sub_keys': nrm(ks[19], (L, PEER_HEADS, 2, N_KEYS, D_QUERY // 2), (D_QUERY // 2) ** -0.5),
        'expert_u': nrm(ks[20], (L, N_EXPERTS, D_MODEL), D_MODEL ** -0.5),
        'expert_v': nrm(ks[21], (L, N_EXPERTS, D_MODEL), DEEPNORM_BETA),
        'ln2_g': 1.0 + nrm(ks[22], (L, D_MODEL), 0.02),
        'ln2_b': nrm(ks[23], (L, D_MODEL), 0.01),
    }


def reference(x, c, w_ada, b_ada, w_in, conv_ssd_w, conv_ssd_b, dt_bias_f, dt_bias_b,
              a_log_f, a_log_b, d_skip, ssd_norm_g, short_conv_w, sc_norm_g, w_out,
              ln1_g, ln1_b, w_query, sub_keys, expert_u, expert_v, ln2_g, ln2_b):
    for i in range(DEPTH):
        mod = jax.nn.silu(c) @ w_ada[i] + b_ada[i]
        sh1, sc1, g1, sh2, sc2, g2 = jnp.split(mod[:, None, :], 6, axis=-1)
        h = x * (1.0 + sc1) + sh1
        mix = hybrid_mixer(h, w_in[i], conv_ssd_w[i], conv_ssd_b[i], dt_bias_f[i], dt_bias_b[i],
                           a_log_f[i], a_log_b[i], d_skip[i], ssd_norm_g[i], short_conv_w[i],
                           sc_norm_g[i], w_out[i])
        x = layer_norm(DEEPNORM_ALPHA * x + g1 * mix, ln1_g[i], ln1_b[i])
        h = x * (1.0 + sc2) + sh2
        ffn = peer(h, w_query[i], sub_keys[i], expert_u[i], expert_v[i])
        x = layer_norm(DEEPNORM_ALPHA * x + g2 * ffn, ln2_g[i], ln2_b[i])
    return x
```

```python
import functools
import math

import jax
import jax.numpy as jnp
import numpy as np
from jax import lax
from jax.experimental import pallas as pl
from jax.experimental.pallas import tpu as pltpu

F32 = jnp.float32
BF16 = jnp.bfloat16

D_MODEL = 2048
SSD_WIDTH = D_MODEL
SSD_HEADDIM = 64
SSD_HEADS = SSD_WIDTH // SSD_HEADDIM
SSD_GROUPS = 4
SSD_STATE = 128
SSD_CHUNK = 128
HEADS_PER_GROUP = SSD_HEADS // SSD_GROUPS
GROUP_WIDTH = HEADS_PER_GROUP * SSD_HEADDIM
SC_WIDTH = D_MODEL
SC_GROUPS = 32
XBC_WIDTH = SSD_WIDTH + 2 * SSD_GROUPS * SSD_STATE
MAIN_COLS = 4 * D_MODEL + XBC_WIDTH
XBC_COL0 = 4 * D_MODEL
PEER_HEADS = 8
N_KEYS = 128
PEER_TOPK = 16
D_QUERY = 512
HALF_QUERY = D_QUERY // 2
NORM_EPS = 1e-5
LANES = 128
SUBLANES = 8
VMEM_LIMIT = 56 * 1024 * 1024


def _dot(a, b):
    return jnp.dot(a, b, preferred_element_type=F32)


def _dot_nt(a, b):
    return lax.dot_general(a, b, (((1,), (1,)), ((), ())), preferred_element_type=F32)


def _split2(x):
    hi = x.astype(BF16)
    lo = (x - hi.astype(F32)).astype(BF16)
    return hi, lo


def _split3(x):
    hi = x.astype(BF16)
    r = x - hi.astype(F32)
    mid = r.astype(BF16)
    lo = (r - mid.astype(F32)).astype(BF16)
    return hi, mid, lo


def _dot_lhs_split2(x, m_bf16):
    hi, lo = _split2(x)
    return _dot(hi, m_bf16) + _dot(lo, m_bf16)


def _softplus(x):
    return jnp.maximum(x, 0.0) + jnp.log1p(jnp.exp(-jnp.abs(x)))


def _silu(x):
    return x * (1.0 / (1.0 + jnp.exp(-x)))


def _params(sem):
    return pltpu.CompilerParams(dimension_semantics=sem, vmem_limit_bytes=VMEM_LIMIT)


def _ada_kernel(c_ref, w_ref, b_ref, o_ref):
    sc = _silu(c_ref[...])
    o_ref[...] = _dot(sc.astype(BF16), w_ref[...].astype(BF16)) + b_ref[...]


def _ada(c_pad, w, b):
    rows, d = c_pad.shape
    n = w.shape[1]
    bn = 1024
    return pl.pallas_call(
        _ada_kernel,
        grid=(n // bn,),
        in_specs=[pl.BlockSpec((rows, d), lambda j: (0, 0)),
                  pl.BlockSpec((d, bn), lambda j: (0, j)),
                  pl.BlockSpec((1, bn), lambda j: (0, j))],
        out_specs=pl.BlockSpec((rows, bn), lambda j: (0, j)),
        out_shape=jax.ShapeDtypeStruct((rows, n), F32),
        compiler_params=_params(("arbitrary",)),
        name="ada",
    )(c_pad, w, b)


def _inproj_kernel(x_ref, mod_ref, w_ref, wdh_ref, wdl_ref, o_ref, dt_ref, h_scr):
    @pl.when(pl.program_id(1) == 0)
    def _():
        m = mod_ref[0]
        h = x_ref[...] * (1.0 + m[1:2]) + m[0:1]
        hi, lo = _split2(h)
        h_scr[...] = hi
        dt_ref[...] = _dot(hi, wdh_ref[...]) + _dot(lo, wdh_ref[...]) + _dot(hi, wdl_ref[...])

    o_ref[...] = _dot(h_scr[...], w_ref[...])


def _inproj(x2, mod3, w_main, wd_hi, wd_lo, seq):
    t, d = x2.shape
    n = w_main.shape[1]
    bm = min(1024, seq)
    bn = 1024
    per_batch = seq // bm
    return pl.pallas_call(
        _inproj_kernel,
        grid=(t // bm, n // bn),
        in_specs=[pl.BlockSpec((bm, d), lambda i, j: (i, 0)),
                  pl.BlockSpec((1, 6, d), lambda i, j: (i // per_batch, 0, 0)),
                  pl.BlockSpec((d, bn), lambda i, j: (0, j)),
                  pl.BlockSpec((d, LANES), lambda i, j: (0, 0)),
                  pl.BlockSpec((d, LANES), lambda i, j: (0, 0))],
        out_specs=[pl.BlockSpec((bm, bn), lambda i, j: (i, j)),
                   pl.BlockSpec((bm, LANES), lambda i, j: (i, 0))],
        out_shape=[jax.ShapeDtypeStruct((t, n), F32),
                   jax.ShapeDtypeStruct((t, LANES), F32)],
        scratch_shapes=[pltpu.VMEM((bm, d), BF16)],
        compiler_params=_params(("parallel", "arbitrary")),
        name="inproj",
    )(x2, mod3, w_main, wd_hi, wd_lo)


def _shifted_rows(u, prev_row, next_row):
    rows = u.shape[0]
    ridx = lax.broadcasted_iota(jnp.int32, u.shape, 0)
    up = jnp.where(ridx == 0, prev_row, pltpu.roll(u, 1, axis=0))
    un = jnp.where(ridx == rows - 1, next_row, pltpu.roll(u, rows - 1, axis=0))
    return up, un


def _halo_specs(bm, bw, col_of, t, ):
    per = bm // SUBLANES
    last = t // SUBLANES - 1
    prev = pl.BlockSpec((SUBLANES, bw), lambda i, j: (jnp.maximum(i * per - 1, 0), col_of(j)))
    nxt = pl.BlockSpec((SUBLANES, bw), lambda i, j: (jnp.minimum((i + 1) * per, last), col_of(j)))
    return prev, nxt


def _conv_kernel(u_ref, p_ref, n_ref, w_ref, b_ref, o_ref, *, bm, seq):
    i = pl.program_id(0)
    u = u_ref[...]
    has_prev = ((i * bm) % seq != 0).astype(F32)
    has_next = (((i + 1) * bm) % seq != 0).astype(F32)
    up, un = _shifted_rows(u, p_ref[SUBLANES - 1:SUBLANES, :] * has_prev, n_ref[0:1, :] * has_next)
    w = w_ref[...]
    o_ref[...] = _silu(up * w[0:1] + u * w[1:2] + un * w[2:3] + b_ref[...])


def _conv(proj, conv_w, conv_b, seq):
    t = proj.shape[0]
    bm = min(512, seq)
    bw = 1024
    col0 = XBC_COL0 // bw
    col_of = lambda j: col0 + j
    prev, nxt = _halo_specs(bm, bw, col_of, t)
    return pl.pallas_call(
        functools.partial(_conv_kernel, bm=bm, seq=seq),
        grid=(t // bm, XBC_WIDTH // bw),
        in_specs=[pl.BlockSpec((bm, bw), lambda i, j: (i, col_of(j))), prev, nxt,
                  pl.BlockSpec((3, bw), lambda i, j: (0, j)),
                  pl.BlockSpec((1, bw), lambda i, j: (0, j))],
        out_specs=pl.BlockSpec((bm, bw), lambda i, j: (i, j)),
        out_shape=jax.ShapeDtypeStruct((t, XBC_WIDTH), F32),
        compiler_params=_params(("parallel", "parallel")),
        name="ssdconv",
    )(proj, proj, proj, conv_w, conv_b)


def _ssd_direction(x_ref, b_ref, c_ref, dt_ref, e_ref, h_scr, y_ref, bias, a, dskip, backward):
    cs_len = SSD_CHUNK
    row = lax.broadcasted_iota(jnp.int32, (cs_len, cs_len), 0)
    col = lax.broadcasted_iota(jnp.int32, (cs_len, cs_len), 1)
    tri = (col <= row).astype(BF16)
    dt = _softplus(dt_ref[...] + bias)
    adt = dt * a
    h3 = _split3(adt)
    cs = _dot(tri, h3[0]) + _dot(tri, h3[1]) + _dot(tri, h3[2])
    tot = cs[cs_len - 1:cs_len, :]
    if backward:
        ecs = cs - adt
        p = -ecs
        wst = dt * jnp.exp(ecs)
        indec = jnp.exp(tot - ecs)
        mask = col >= row
    else:
        p = cs
        wst = dt * jnp.exp(tot - cs)
        indec = jnp.exp(cs)
        mask = col <= row
    pt = p.T
    e = e_ref[...]
    dtx = _dot_lhs_split2(dt, e)
    wstx = _dot_lhs_split2(wst, e)
    indx = _dot_lhs_split2(indec, e)
    xs = x_ref[...]
    xd = (xs * dtx).astype(BF16)
    xw = (xs * wstx).astype(BF16)
    col_off = SSD_HEADS if backward else 0
    for g in range(SSD_GROUPS):
        gs = slice(g * GROUP_WIDTH, (g + 1) * GROUP_WIDTH)
        bg = b_ref[:, g * SSD_STATE:(g + 1) * SSD_STATE]
        cg = c_ref[:, g * SSD_STATE:(g + 1) * SSD_STATE].astype(BF16)
        cb = _dot_nt(cg, bg.astype(BF16))
        h_in = h_scr[g]
        y_off = _dot(cg, h_in.astype(BF16)) * indx[:, gs]
        st = _dot(bg.T.astype(BF16), xw[:, gs])
        for r in range(HEADS_PER_GROUP):
            hd = g * HEADS_PER_GROUP + r
            ci = hd + col_off
            seg = p[:, ci:ci + 1] - pt[ci:ci + 1, :]
            lmat = jnp.exp(jnp.where(mask, seg, -jnp.inf))
            m = (cb * lmat).astype(BF16)
            hs = slice(hd * SSD_HEADDIM, (hd + 1) * SSD_HEADDIM)
            y = _dot(m, xd[:, hs]) + y_off[:, r * SSD_HEADDIM:(r + 1) * SSD_HEADDIM]
            if dskip is not None:
                y = y + xs[:, hs] * dskip[:, hs]
            y_ref[:, hs] = y
        edge = 0 if backward else cs_len - 1
        h_scr[g] = indx[edge:edge + 1, gs] * h_in + st


def _ssd_kernel(xf_ref, bf_ref, cf_ref, dtf_ref, xb_ref, bb_ref, cb_ref, dtb_ref,
                bias_ref, alog_ref, dskip_ref, ef_ref, eb_ref, yf_ref, yb_ref, hf_scr, hb_scr):
    @pl.when(pl.program_id(1) == 0)
    def _():
        hf_scr[...] = jnp.zeros_like(hf_scr)
        hb_scr[...] = jnp.zeros_like(hb_scr)

    bias = bias_ref[...]
    a = -jnp.exp(alog_ref[...])
    _ssd_direction(xf_ref, bf_ref, cf_ref, dtf_ref, ef_ref, hf_scr, yf_ref, bias, a, dskip_ref[...], False)
    _ssd_direction(xb_ref, bb_ref, cb_ref, dtb_ref, eb_ref, hb_scr, yb_ref, bias, a, None, True)


def _ssd(xbc, dt_raw, bias, alog, dskip_x, e_f, e_b, batch, seq):
    t = xbc.shape[0]
    nc = seq // SSD_CHUNK
    cl = SSD_CHUNK
    gn = SSD_GROUPS * SSD_STATE
    bcol = SSD_WIDTH // gn
    fwd = lambda b, k: b * nc + k
    bwd = lambda b, k: b * nc + (nc - 1 - k)

    def specs(ch):
        return [pl.BlockSpec((cl, SSD_WIDTH), lambda b, k: (ch(b, k), 0)),
                pl.BlockSpec((cl, gn), lambda b, k: (ch(b, k), bcol)),
                pl.BlockSpec((cl, gn), lambda b, k: (ch(b, k), bcol + 1)),
                pl.BlockSpec((cl, LANES), lambda b, k: (ch(b, k), 0))]

    const = lambda shape: pl.BlockSpec(shape, lambda b, k: (0, 0))
    state = pltpu.VMEM((SSD_GROUPS, SSD_STATE, GROUP_WIDTH), F32)
    return pl.pallas_call(
        _ssd_kernel,
        grid=(batch, nc),
        in_specs=specs(fwd) + specs(bwd) + [const((1, LANES)), const((1, LANES)), const((1, SSD_WIDTH)),
                                            const((LANES, SSD_WIDTH)), const((LANES, SSD_WIDTH))],
        out_specs=[pl.BlockSpec((cl, SSD_WIDTH), lambda b, k: (fwd(b, k), 0)),
                   pl.BlockSpec((cl, SSD_WIDTH), lambda b, k: (bwd(b, k), 0))],
        out_shape=[jax.ShapeDtypeStruct((t, SSD_WIDTH), F32)] * 2,
        scratch_shapes=[state, state],
        compiler_params=_params(("parallel", "arbitrary")),
        name="ssd",
    )(xbc, xbc, xbc, dt_raw, xbc, xbc, xbc, dt_raw, bias, alog, dskip_x, e_f, e_b)


def _group_rms(y, r_ref, e_ref, group_size):
    gsum = _dot_lhs_split2(y * y, r_ref[...])
    inv = lax.rsqrt(gsum * (1.0 / group_size) + NORM_EPS)
    return y * _dot_lhs_split2(inv, e_ref[...])


def _layer_norm(v, g, b):
    mu = jnp.mean(v, axis=-1, keepdims=True)
    vc = v - mu
    var = jnp.mean(vc * vc, axis=-1, keepdims=True)
    return vc * lax.rsqrt(var + NORM_EPS) * g + b


def _post_kernel(yf_ref, yb_ref, z_ref, gb_ref, gc_ref, gcp_ref, gcn_ref, v_ref, vp_ref, vn_ref,
                 x_ref, mod_ref, scw_ref, ssdg_ref, scg_ref, r4_ref, e4_ref, r32_ref, e32_ref,
                 wout_ref, lng_ref, lnb_ref, x1_ref, h2_ref, *, bm, seq, alpha):
    i = pl.program_id(0)
    m = mod_ref[0]
    y = (yf_ref[...] + yb_ref[...]) * _silu(z_ref[...])
    y_ssd = _group_rms(y, r4_ref, e4_ref, SSD_WIDTH // SSD_GROUPS) * ssdg_ref[...]

    u = gc_ref[...] * v_ref[...]
    has_prev = ((i * bm) % seq != 0).astype(F32)
    has_next = (((i + 1) * bm) % seq != 0).astype(F32)
    last = SUBLANES - 1
    up, un = _shifted_rows(u, gcp_ref[last:last + 1, :] * vp_ref[last:last + 1, :] * has_prev,
                           gcn_ref[0:1, :] * vn_ref[0:1, :] * has_next)
    w = scw_ref[...]
    y_sc = gb_ref[...] * (up * w[0:1] + u * w[1:2] + un * w[2:3])
    y_sc = _group_rms(y_sc, r32_ref, e32_ref, SC_WIDTH // SC_GROUPS) * scg_ref[...]

    mix = (_dot(y_ssd.astype(BF16), wout_ref[0:SSD_WIDTH, :])
           + _dot(y_sc.astype(BF16), wout_ref[SSD_WIDTH:SSD_WIDTH + SC_WIDTH, :]))
    x1 = _layer_norm(alpha * x_ref[...] + m[2:3] * mix, lng_ref[...], lnb_ref[...])
    x1_ref[...] = x1
    h2_ref[...] = (x1 * (1.0 + m[4:5]) + m[3:4]).astype(BF16)


def _single(shape, index_map):
    return pl.BlockSpec(shape, index_map, pipeline_mode=pl.Buffered(1))


def _post(yf, yb, proj, x2, mod3, scw, ssdg, scg, r4, e4, r32, e32, wout, lng, lnb, seq, alpha):
    t, d = x2.shape
    bm = min(128, seq)
    per_batch = seq // bm
    per = bm // SUBLANES
    last = t // SUBLANES - 1
    main = lambda c: pl.BlockSpec((bm, d), lambda i: (i, c))
    prev = lambda c: pl.BlockSpec((SUBLANES, d), lambda i: (jnp.maximum(i * per - 1, 0), c))
    nxt = lambda c: pl.BlockSpec((SUBLANES, d), lambda i: (jnp.minimum((i + 1) * per, last), c))
    const = lambda shape: _single(shape, lambda i: (0, 0))
    return pl.pallas_call(
        functools.partial(_post_kernel, bm=bm, seq=seq, alpha=alpha),
        grid=(t // bm,),
        in_specs=[main(0), main(0), main(0), main(1), main(2), prev(2), nxt(2), main(3), prev(3), nxt(3),
                  main(0), pl.BlockSpec((1, 6, d), lambda i: (i // per_batch, 0, 0)),
                  const((3, d)), const((1, d)), const((1, d)),
                  const((d, LANES)), const((LANES, d)), const((d, LANES)), const((LANES, d)),
                  const((2 * d, d)), const((1, d)), const((1, d))],
        out_specs=[pl.BlockSpec((bm, d), lambda i: (i, 0)), pl.BlockSpec((bm, d), lambda i: (i, 0))],
        out_shape=[jax.ShapeDtypeStruct((t, d), F32), jax.ShapeDtypeStruct((t, d), BF16)],
        compiler_params=_params(("parallel",)),
        name="post",
    )(yf, yb, proj, proj, proj, proj, proj, proj, proj, proj, x2, mod3,
      scw, ssdg, scg, r4, e4, r32, e32, wout, lng, lnb)


N_EXTRACT = PEER_TOPK + 1


def _top_values(s, n):
    vals = []
    for _ in range(n):
        m = jnp.max(s, axis=0, keepdims=True)
        vals.append(m)
        s = jnp.where(s == m, -jnp.inf, s)
    return vals


def _query_kernel(h2_ref, wq_ref, k_ref, s2_ref, e2_ref, thr_ref, c1_ref):
    q = _dot(h2_ref[...], wq_ref[...]).astype(BF16)
    tb = q.shape[0]
    for h in range(PEER_HEADS):
        q1 = q[:, h * D_QUERY:h * D_QUERY + HALF_QUERY]
        q2 = q[:, h * D_QUERY + HALF_QUERY:(h + 1) * D_QUERY]
        s1 = _dot_nt(k_ref[h, 0], q1)
        s2 = _dot_nt(k_ref[h, 1], q2)
        v1 = _top_values(s1, N_EXTRACT)
        v2 = _top_values(s2, N_EXTRACT)
        rows = [v1[a] + v2[b] for a in range(N_EXTRACT) for b in range(N_EXTRACT)
                if (a + 1) * (b + 1) <= N_EXTRACT]
        pad = (-len(rows)) % SUBLANES
        rows += [jnp.full((1, tb), -jnp.inf, F32)] * pad
        top = _top_values(jnp.concatenate(rows, axis=0), N_EXTRACT)
        z = jnp.ones((1, tb), F32)
        for kk in range(1, PEER_TOPK):
            z = z + jnp.exp(top[kk] - top[0])
        tau = 0.5 * (top[PEER_TOPK - 1] + top[PEER_TOPK])
        s2_ref[h] = s2
        e2_ref[h] = jnp.exp(s2 - v2[0])
        thr_ref[h] = tau - s1
        c1_ref[h] = jnp.exp(s1 - v1[0]) / z


def _query(h2, wq, keys):
    t, d = h2.shape
    tb = min(256, t)
    out = jax.ShapeDtypeStruct((PEER_HEADS, N_KEYS, t), F32)
    ospec = pl.BlockSpec((PEER_HEADS, N_KEYS, tb), lambda i: (0, 0, i))
    return pl.pallas_call(
        _query_kernel,
        grid=(t // tb,),
        in_specs=[pl.BlockSpec((tb, d), lambda i: (i, 0)),
                  _single((d, PEER_HEADS * D_QUERY), lambda i: (0, 0)),
                  _single((PEER_HEADS, 2, N_KEYS, HALF_QUERY), lambda i: (0, 0, 0, 0))],
        out_specs=[ospec] * 4,
        out_shape=[out] * 4,
        compiler_params=_params(("parallel",)),
        name="peerquery",
    )(h2, wq, keys)


def _gelu(a):
    return 0.5 * a * (1.0 + lax.erf(a * np.float32(math.sqrt(0.5))))


def _peer_kernel(h2_ref, u_ref, vt_ref, s2_ref, e2_ref, thr_ref, c1_ref, x1_ref, mod_ref,
                 lng_ref, lnb_ref, o_ref, acc_ref, *, alpha):
    e = pl.program_id(1)

    @pl.when(e == 0)
    def _():
        acc_ref[...] = jnp.zeros_like(acc_ref)

    at = _dot_nt(u_ref[...], h2_ref[...])
    parts = []
    for ii in range(u_ref.shape[0] // N_KEYS):
        g = None
        for h in range(PEER_HEADS):
            w = jnp.where(s2_ref[h] >= thr_ref[h, ii:ii + 1, :], e2_ref[h], 0.0) * c1_ref[h, ii:ii + 1, :]
            g = w if g is None else g + w
        a = at[ii * N_KEYS:(ii + 1) * N_KEYS, :]
        parts.append((_gelu(a) * g).astype(BF16))
    pt = jnp.concatenate(parts, axis=0)
    acc_ref[...] += _dot(vt_ref[...], pt)

    @pl.when(e == pl.num_programs(1) - 1)
    def _():
        m = mod_ref[0]
        ffn = acc_ref[...].T
        o_ref[...] = _layer_norm(alpha * x1_ref[...] + m[5:6] * ffn, lng_ref[...], lnb_ref[...])


def _peer(h2, u, vt, s2, e2, thr, c1, x1, mod3, lng, lnb, seq, alpha):
    t, d = h2.shape
    n_exp = u.shape[0]
    tb = min(512, seq)
    eb = 1024
    per_batch = seq // tb
    kb = eb // N_KEYS
    full = _single((PEER_HEADS, N_KEYS, tb), lambda i, j: (0, 0, i))
    part = pl.BlockSpec((PEER_HEADS, kb, tb), lambda i, j: (0, j, i))
    return pl.pallas_call(
        functools.partial(_peer_kernel, alpha=alpha),
        grid=(t // tb, n_exp // eb),
        in_specs=[_single((tb, d), lambda i, j: (i, 0)),
                  pl.BlockSpec((eb, d), lambda i, j: (j, 0)),
                  pl.BlockSpec((d, eb), lambda i, j: (0, j)),
                  full, full, part, part,
                  _single((tb, d), lambda i, j: (i, 0)),
                  pl.BlockSpec((1, 6, d), lambda i, j: (i // per_batch, 0, 0)),
                  pl.BlockSpec((1, d), lambda i, j: (0, 0)),
                  pl.BlockSpec((1, d), lambda i, j: (0, 0))],
        out_specs=pl.BlockSpec((tb, d), lambda i, j: (i, 0)),
        out_shape=jax.ShapeDtypeStruct((t, d), F32),
        scratch_shapes=[pltpu.VMEM((d, tb), F32)],
        compiler_params=_params(("parallel", "arbitrary")),
        name="peer",
    )(h2, u, vt, s2, e2, thr, c1, x1, mod3, lng, lnb)


def _one_hot_cols(n_rows, n_cols, group_size, row_offset=0):
    r = np.arange(n_rows)[:, None]
    c = np.arange(n_cols)[None, :]
    return jnp.asarray(r == row_offset + c // group_size, dtype=BF16)


def _row(v):
    return v.reshape(1, -1).astype(F32)


def _pad_lanes(v):
    return jnp.pad(v, (0, LANES - v.shape[0])).reshape(1, LANES).astype(F32)


def kernel(x, c, w_ada, b_ada, w_in, conv_ssd_w, conv_ssd_b, dt_bias_f, dt_bias_b, a_log_f, a_log_b, d_skip, ssd_norm_g, short_conv_w, sc_norm_g, w_out, ln1_g, ln1_b, w_query, sub_keys, expert_u, expert_v, ln2_g, ln2_b):
    batch, seq, d = x.shape
    depth = w_ada.shape[0]
    alpha = (2.0 * depth) ** 0.25
    t = batch * seq
    x2 = x.reshape(t, d)
    c_pad = jnp.pad(c, ((0, SUBLANES - batch % SUBLANES), (0, 0))) if batch % SUBLANES else c

    e_f = _one_hot_cols(LANES, SSD_WIDTH, SSD_HEADDIM, 0)
    e_b = _one_hot_cols(LANES, SSD_WIDTH, SSD_HEADDIM, SSD_HEADS)
    e4 = _one_hot_cols(LANES, SSD_WIDTH, SSD_WIDTH // SSD_GROUPS)
    e32 = _one_hot_cols(LANES, SC_WIDTH, SC_WIDTH // SC_GROUPS)
    r4 = e4.T
    r32 = e32.T

    o1 = SSD_WIDTH
    o2 = o1 + XBC_WIDTH
    o3 = o2 + 2 * SSD_HEADS
    o4 = o3 + SC_WIDTH
    o5 = o4 + SC_WIDTH
    for i in range(depth):
        mod = _ada(c_pad, w_ada[i], b_ada[i].reshape(1, -1))
        mod3 = mod[:batch].reshape(batch, 6, d)

        w = w_in[i]
        w_main = jnp.concatenate([w[:, :o1], w[:, o3:o4], w[:, o4:o5], w[:, o5:], w[:, o1:o2]],
                                 axis=1).astype(BF16)
        w_dt = jnp.pad(w[:, o2:o3], ((0, 0), (0, LANES - 2 * SSD_HEADS)))
        wd_hi, wd_lo = _split2(w_dt)
        proj, dt_raw = _inproj(x2, mod3, w_main, wd_hi, wd_lo, seq)

        xbc = _conv(proj, conv_ssd_w[i], conv_ssd_b[i].reshape(1, -1), seq)
        bias = _pad_lanes(jnp.concatenate([dt_bias_f[i], dt_bias_b[i]]))
        alog = _pad_lanes(jnp.concatenate([a_log_f[i], a_log_b[i]]))
        dskip_x = _row(jnp.repeat(d_skip[i], SSD_HEADDIM))
        y_f, y_b = _ssd(xbc, dt_raw, bias, alog, dskip_x, e_f, e_b, batch, seq)

        x1, h2 = _post(y_f, y_b, proj, x2, mod3, short_conv_w[i], _row(ssd_norm_g[i]), _row(sc_norm_g[i]),
                       r4, e4, r32, e32, w_out[i].astype(BF16), _row(ln1_g[i]), _row(ln1_b[i]), seq, alpha)

        s2, e2, thr, c1 = _query(h2, w_query[i].astype(BF16), sub_keys[i].astype(BF16))
        x2 = _peer(h2, expert_u[i].astype(BF16), expert_v[i].T.astype(BF16), s2, e2, thr, c1,
                   x1, mod3, _row(ln2_g[i]), _row(ln2_b[i]), seq, alpha)
    return x2.reshape(batch, seq, d)
```

```python
import functools
import math

import jax
import jax.numpy as jnp
import numpy as np
from jax import lax
from jax.experimental import pallas as pl
from jax.experimental.pallas import tpu as pltpu

F32 = jnp.float32
BF16 = jnp.bfloat16

D_MODEL = 2048
SSD_WIDTH = D_MODEL
SSD_HEADDIM = 64
SSD_HEADS = SSD_WIDTH // SSD_HEADDIM
SSD_GROUPS = 4
SSD_STATE = 128
SSD_CHUNK = 128
HEADS_PER_GROUP = SSD_HEADS // SSD_GROUPS
GROUP_WIDTH = HEADS_PER_GROUP * SSD_HEADDIM
SC_WIDTH = D_MODEL
SC_GROUPS = 32
XBC_WIDTH = SSD_WIDTH + 2 * SSD_GROUPS * SSD_STATE
MAIN_COLS = 4 * D_MODEL + XBC_WIDTH
XBC_COL0 = 4 * D_MODEL
PEER_HEADS = 8
N_KEYS = 128
PEER_TOPK = 16
D_QUERY = 512
HALF_QUERY = D_QUERY // 2
PEER_TOKEN_CHUNK = 256
NORM_EPS = 1e-5
LANES = 128
SUBLANES = 8
VMEM_LIMIT = 56 * 1024 * 1024


def _dot(a, b):
    return jnp.dot(a, b, preferred_element_type=F32)


def _dot_nt(a, b):
    return lax.dot_general(a, b, (((1,), (1,)), ((), ())), preferred_element_type=F32)


def _split2(x):
    hi = x.astype(BF16)
    lo = (x - hi.astype(F32)).astype(BF16)
    return hi, lo


def _split3(x):
    hi = x.astype(BF16)
    r = x - hi.astype(F32)
    mid = r.astype(BF16)
    lo = (r - mid.astype(F32)).astype(BF16)
    return hi, mid, lo


def _dot_lhs_split2(x, m_bf16):
    hi, lo = _split2(x)
    return _dot(hi, m_bf16) + _dot(lo, m_bf16)


def _softplus(x):
    return jnp.maximum(x, 0.0) + jnp.log1p(jnp.exp(-jnp.abs(x)))


def _silu(x):
    return x * (1.0 / (1.0 + jnp.exp(-x)))


def _params(sem):
    return pltpu.CompilerParams(dimension_semantics=sem, vmem_limit_bytes=VMEM_LIMIT)


def _ada_kernel(c_ref, w_ref, b_ref, o_ref):
    sc = _silu(c_ref[...])
    o_ref[...] = _dot(sc.astype(BF16), w_ref[...].astype(BF16)) + b_ref[...]


def _ada(c_pad, w, b):
    rows, d = c_pad.shape
    n = w.shape[1]
    bn = 1024
    return pl.pallas_call(
        _ada_kernel,
        grid=(n // bn,),
        in_specs=[pl.BlockSpec((rows, d), lambda j: (0, 0)),
                  pl.BlockSpec((d, bn), lambda j: (0, j)),
                  pl.BlockSpec((1, bn), lambda j: (0, j))],
        out_specs=pl.BlockSpec((rows, bn), lambda j: (0, j)),
        out_shape=jax.ShapeDtypeStruct((rows, n), F32),
        compiler_params=_params(("arbitrary",)),
        name="ada",
    )(c_pad, w, b)


def _inproj_kernel(x_ref, mod_ref, w_ref, wdh_ref, wdl_ref, o_ref, dt_ref, h_scr):
    @pl.when(pl.program_id(1) == 0)
    def _():
        m = mod_ref[0]
        h = x_ref[...] * (1.0 + m[1:2]) + m[0:1]
        hi, lo = _split2(h)
        h_scr[...] = hi
        dt_ref[...] = _dot(hi, wdh_ref[...]) + _dot(lo, wdh_ref[...]) + _dot(hi, wdl_ref[...])

    o_ref[...] = _dot(h_scr[...], w_ref[...])


def _inproj(x2, mod3, w_main, wd_hi, wd_lo, seq):
    t, d = x2.shape
    n = w_main.shape[1]
    bm = min(1024, seq)
    bn = 1024
    per_batch = seq // bm
    return pl.pallas_call(
        _inproj_kernel,
        grid=(t // bm, n // bn),
        in_specs=[pl.BlockSpec((bm, d), lambda i, j: (i, 0)),
                  pl.BlockSpec((1, 6, d), lambda i, j: (i // per_batch, 0, 0)),
                  pl.BlockSpec((d, bn), lambda i, j: (0, j)),
                  pl.BlockSpec((d, LANES), lambda i, j: (0, 0)),
                  pl.BlockSpec((d, LANES), lambda i, j: (0, 0))],
        out_specs=[pl.BlockSpec((bm, bn), lambda i, j: (i, j)),
                   pl.BlockSpec((bm, LANES), lambda i, j: (i, 0))],
        out_shape=[jax.ShapeDtypeStruct((t, n), F32),
                   jax.ShapeDtypeStruct((t, LANES), F32)],
        scratch_shapes=[pltpu.VMEM((bm, d), BF16)],
        compiler_params=_params(("parallel", "arbitrary")),
        name="inproj",
    )(x2, mod3, w_main, wd_hi, wd_lo)


def _shifted_rows(u, prev_row, next_row):
    rows = u.shape[0]
    ridx = lax.broadcasted_iota(jnp.int32, u.shape, 0)
    up = jnp.where(ridx == 0, prev_row, pltpu.roll(u, 1, axis=0))
    un = jnp.where(ridx == rows - 1, next_row, pltpu.roll(u, rows - 1, axis=0))
    return up, un


def _halo_specs(bm, bw, col_of, t, ):
    per = bm // SUBLANES
    last = t // SUBLANES - 1
    prev = pl.BlockSpec((SUBLANES, bw), lambda i, j: (jnp.maximum(i * per - 1, 0), col_of(j)))
    nxt = pl.BlockSpec((SUBLANES, bw), lambda i, j: (jnp.minimum((i + 1) * per, last), col_of(j)))
    return prev, nxt


def _conv_kernel(u_ref, p_ref, n_ref, w_ref, b_ref, o_ref, *, bm, seq):
    i = pl.program_id(0)
    u = u_ref[...]
    has_prev = ((i * bm) % seq != 0).astype(F32)
    has_next = (((i + 1) * bm) % seq != 0).astype(F32)
    up, un = _shifted_rows(u, p_ref[SUBLANES - 1:SUBLANES, :] * has_prev, n_ref[0:1, :] * has_next)
    w = w_ref[...]
    o_ref[...] = _silu(up * w[0:1] + u * w[1:2] + un * w[2:3] + b_ref[...])


def _conv(proj, conv_w, conv_b, seq):
    t = proj.shape[0]
    bm = min(512, seq)
    bw = 1024
    col0 = XBC_COL0 // bw
    col_of = lambda j: col0 + j
    prev, nxt = _halo_specs(bm, bw, col_of, t)
    return pl.pallas_call(
        functools.partial(_conv_kernel, bm=bm, seq=seq),
        grid=(t // bm, XBC_WIDTH // bw),
        in_specs=[pl.BlockSpec((bm, bw), lambda i, j: (i, col_of(j))), prev, nxt,
                  pl.BlockSpec((3, bw), lambda i, j: (0, j)),
                  pl.BlockSpec((1, bw), lambda i, j: (0, j))],
        out_specs=pl.BlockSpec((bm, bw), lambda i, j: (i, j)),
        out_shape=jax.ShapeDtypeStruct((t, XBC_WIDTH), F32),
        compiler_params=_params(("parallel", "parallel")),
        name="ssdconv",
    )(proj, proj, proj, conv_w, conv_b)


def _ssd_direction(x_ref, b_ref, c_ref, dt_ref, e_ref, h_scr, y_ref, bias, a, dskip, backward):
    cs_len = SSD_CHUNK
    row = lax.broadcasted_iota(jnp.int32, (cs_len, cs_len), 0)
    col = lax.broadcasted_iota(jnp.int32, (cs_len, cs_len), 1)
    tri = (col <= row).astype(BF16)
    dt = _softplus(dt_ref[...] + bias)
    adt = dt * a
    h3 = _split3(adt)
    cs = _dot(tri, h3[0]) + _dot(tri, h3[1]) + _dot(tri, h3[2])
    tot = cs[cs_len - 1:cs_len, :]
    if backward:
        ecs = cs - adt
        p = -ecs
        wst = dt * jnp.exp(ecs)
        indec = jnp.exp(tot - ecs)
        mask = col >= row
    else:
        p = cs
        wst = dt * jnp.exp(tot - cs)
        indec = jnp.exp(cs)
        mask = col <= row
    pt = p.T
    e = e_ref[...]
    dtx = _dot_lhs_split2(dt, e)
    wstx = _dot_lhs_split2(wst, e)
    indx = _dot_lhs_split2(indec, e)
    xs = x_ref[...]
    xd = (xs * dtx).astype(BF16)
    xw = (xs * wstx).astype(BF16)
    col_off = SSD_HEADS if backward else 0
    for g in range(SSD_GROUPS):
        gs = slice(g * GROUP_WIDTH, (g + 1) * GROUP_WIDTH)
        bg = b_ref[:, g * SSD_STATE:(g + 1) * SSD_STATE]
        cg = c_ref[:, g * SSD_STATE:(g + 1) * SSD_STATE].astype(BF16)
        cb = _dot_nt(cg, bg.astype(BF16))
        h_in = h_scr[g]
        y_off = _dot(cg, h_in.astype(BF16)) * indx[:, gs]
        st = _dot(bg.T.astype(BF16), xw[:, gs])
        for r in range(HEADS_PER_GROUP):
            hd = g * HEADS_PER_GROUP + r
            ci = hd + col_off
            seg = p[:, ci:ci + 1] - pt[ci:ci + 1, :]
            lmat = jnp.exp(jnp.where(mask, seg, -jnp.inf))
            m = (cb * lmat).astype(BF16)
            hs = slice(hd * SSD_HEADDIM, (hd + 1) * SSD_HEADDIM)
            y = _dot(m, xd[:, hs]) + y_off[:, r * SSD_HEADDIM:(r + 1) * SSD_HEADDIM]
            if dskip is not None:
                y = y + xs[:, hs] * dskip[:, hs]
            y_ref[:, hs] = y
        edge = 0 if backward else cs_len - 1
        h_scr[g] = indx[edge:edge + 1, gs] * h_in + st


def _ssd_kernel(xf_ref, bf_ref, cf_ref, dtf_ref, xb_ref, bb_ref, cb_ref, dtb_ref,
                bias_ref, alog_ref, dskip_ref, ef_ref, eb_ref, yf_ref, yb_ref, hf_scr, hb_scr):
    @pl.when(pl.program_id(1) == 0)
    def _():
        hf_scr[...] = jnp.zeros_like(hf_scr)
        hb_scr[...] = jnp.zeros_like(hb_scr)

    bias = bias_ref[...]
    a = -jnp.exp(alog_ref[...])
    _ssd_direction(xf_ref, bf_ref, cf_ref, dtf_ref, ef_ref, hf_scr, yf_ref, bias, a, dskip_ref[...], False)
    _ssd_direction(xb_ref, bb_ref, cb_ref, dtb_ref, eb_ref, hb_scr, yb_ref, bias, a, None, True)


def _ssd(xbc, dt_raw, bias, alog, dskip_x, e_f, e_b, batch, seq):
    t = xbc.shape[0]
    nc = seq // SSD_CHUNK
    cl = SSD_CHUNK
    gn = SSD_GROUPS * SSD_STATE
    bcol = SSD_WIDTH // gn
    fwd = lambda b, k: b * nc + k
    bwd = lambda b, k: b * nc + (nc - 1 - k)

    def specs(ch):
        return [pl.BlockSpec((cl, SSD_WIDTH), lambda b, k: (ch(b, k), 0)),
                pl.BlockSpec((cl, gn), lambda b, k: (ch(b, k), bcol)),
                pl.BlockSpec((cl, gn), lambda b, k: (ch(b, k), bcol + 1)),
                pl.BlockSpec((cl, LANES), lambda b, k: (ch(b, k), 0))]

    const = lambda shape: pl.BlockSpec(shape, lambda b, k: (0, 0))
    state = pltpu.VMEM((SSD_GROUPS, SSD_STATE, GROUP_WIDTH), F32)
    return pl.pallas_call(
        _ssd_kernel,
        grid=(batch, nc),
        in_specs=specs(fwd) + specs(bwd) + [const((1, LANES)), const((1, LANES)), const((1, SSD_WIDTH)),
                                            const((LANES, SSD_WIDTH)), const((LANES, SSD_WIDTH))],
        out_specs=[pl.BlockSpec((cl, SSD_WIDTH), lambda b, k: (fwd(b, k), 0)),
                   pl.BlockSpec((cl, SSD_WIDTH), lambda b, k: (bwd(b, k), 0))],
        out_shape=[jax.ShapeDtypeStruct((t, SSD_WIDTH), F32)] * 2,
        scratch_shapes=[state, state],
        compiler_params=_params(("parallel", "arbitrary")),
        name="ssd",
    )(xbc, xbc, xbc, dt_raw, xbc, xbc, xbc, dt_raw, bias, alog, dskip_x, e_f, e_b)


def _group_rms(y, r_ref, e_ref, group_size):
    gsum = _dot_lhs_split2(y * y, r_ref[...])
    inv = lax.rsqrt(gsum * (1.0 / group_size) + NORM_EPS)
    return y * _dot_lhs_split2(inv, e_ref[...])


def _layer_norm(v, g, b):
    mu = jnp.mean(v, axis=-1, keepdims=True)
    vc = v - mu
    var = jnp.mean(vc * vc, axis=-1, keepdims=True)
    return vc * lax.rsqrt(var + NORM_EPS) * g + b


def _post_kernel(yf_ref, yb_ref, z_ref, gb_ref, gc_ref, gcp_ref, gcn_ref, v_ref, vp_ref, vn_ref,
                 x_ref, mod_ref, scw_ref, ssdg_ref, scg_ref, r4_ref, e4_ref, r32_ref, e32_ref,
                 wout_ref, lng_ref, lnb_ref, x1_ref, h2_ref, *, bm, seq, alpha):
    i = pl.program_id(0)
    m = mod_ref[0]
    y = (yf_ref[...] + yb_ref[...]) * _silu(z_ref[...])
    y_ssd = _group_rms(y, r4_ref, e4_ref, SSD_WIDTH // SSD_GROUPS) * ssdg_ref[...]

    u = gc_ref[...] * v_ref[...]
    has_prev = ((i * bm) % seq != 0).astype(F32)
    has_next = (((i + 1) * bm) % seq != 0).astype(F32)
    last = SUBLANES - 1
    up, un = _shifted_rows(u, gcp_ref[last:last + 1, :] * vp_ref[last:last + 1, :] * has_prev,
                           gcn_ref[0:1, :] * vn_ref[0:1, :] * has_next)
    w = scw_ref[...]
    y_sc = gb_ref[...] * (up * w[0:1] + u * w[1:2] + un * w[2:3])
    y_sc = _group_rms(y_sc, r32_ref, e32_ref, SC_WIDTH // SC_GROUPS) * scg_ref[...]

    mix = (_dot(y_ssd.astype(BF16), wout_ref[0:SSD_WIDTH, :])
           + _dot(y_sc.astype(BF16), wout_ref[SSD_WIDTH:SSD_WIDTH + SC_WIDTH, :]))
    x1 = _layer_norm(alpha * x_ref[...] + m[2:3] * mix, lng_ref[...], lnb_ref[...])
    x1_ref[...] = x1
    h2_ref[...] = (x1 * (1.0 + m[4:5]) + m[3:4]).astype(BF16)


def _single(shape, index_map):
    return pl.BlockSpec(shape, index_map, pipeline_mode=pl.Buffered(1))


def _post(yf, yb, proj, x2, mod3, scw, ssdg, scg, r4, e4, r32, e32, wout, lng, lnb, seq, alpha):
    t, d = x2.shape
    bm = min(128, seq)
    per_batch = seq // bm
    per = bm // SUBLANES
    last = t // SUBLANES - 1
    main = lambda c: pl.BlockSpec((bm, d), lambda i: (i, c))
    prev = lambda c: pl.BlockSpec((SUBLANES, d), lambda i: (jnp.maximum(i * per - 1, 0), c))
    nxt = lambda c: pl.BlockSpec((SUBLANES, d), lambda i: (jnp.minimum((i + 1) * per, last), c))
    const = lambda shape: _single(shape, lambda i: (0, 0))
    return pl.pallas_call(
        functools.partial(_post_kernel, bm=bm, seq=seq, alpha=alpha),
        grid=(t // bm,),
        in_specs=[main(0), main(0), main(0), main(1), main(2), prev(2), nxt(2), main(3), prev(3), nxt(3),
                  main(0), pl.BlockSpec((1, 6, d), lambda i: (i // per_batch, 0, 0)),
                  const((3, d)), const((1, d)), const((1, d)),
                  const((d, LANES)), const((LANES, d)), const((d, LANES)), const((LANES, d)),
                  const((2 * d, d)), const((1, d)), const((1, d))],
        out_specs=[pl.BlockSpec((bm, d), lambda i: (i, 0)), pl.BlockSpec((bm, d), lambda i: (i, 0))],
        out_shape=[jax.ShapeDtypeStruct((t, d), F32), jax.ShapeDtypeStruct((t, d), BF16)],
        compiler_params=_params(("parallel",)),
        name="post",
    )(yf, yb, proj, proj, proj, proj, proj, proj, proj, proj, x2, mod3,
      scw, ssdg, scg, r4, e4, r32, e32, wout, lng, lnb)


N_EXTRACT = PEER_TOPK + 1


def _top_values(s, n, want_rank=False):
    vals = []
    rank = jnp.full(s.shape, float(n), F32) if want_rank else None
    for r in range(n):
        m = jnp.max(s, axis=0, keepdims=True)
        vals.append(m)
        hit = s == m
        if want_rank:
            rank = jnp.where(hit, float(r), rank)
        s = jnp.where(hit, -jnp.inf, s)
    return (vals, rank) if want_rank else vals


def _query_kernel(h2_ref, wq_ref, k_ref, r2_ref, e2_ref, n1_ref, c1_ref):
    q = _dot(h2_ref[...], wq_ref[...]).astype(BF16)
    tb = q.shape[0]
    for h in range(PEER_HEADS):
        q1 = q[:, h * D_QUERY:h * D_QUERY + HALF_QUERY]
        q2 = q[:, h * D_QUERY + HALF_QUERY:(h + 1) * D_QUERY]
        s1 = _dot_nt(k_ref[h, 0], q1)
        s2 = _dot_nt(k_ref[h, 1], q2)
        v1 = _top_values(s1, N_EXTRACT)
        v2, r2 = _top_values(s2, N_EXTRACT, want_rank=True)
        rows = [v1[a] + v2[b] for a in range(N_EXTRACT) for b in range(N_EXTRACT)
                if (a + 1) * (b + 1) <= N_EXTRACT]
        pad = (-len(rows)) % SUBLANES
        rows += [jnp.full((1, tb), -jnp.inf, F32)] * pad
        top = _top_values(jnp.concatenate(rows, axis=0), N_EXTRACT)
        z = jnp.ones((1, tb), F32)
        for kk in range(1, PEER_TOPK):
            z = z + jnp.exp(top[kk] - top[0])
        tau = 0.5 * (top[PEER_TOPK - 1] + top[PEER_TOPK])
        n1 = jnp.zeros(s1.shape, F32)
        for b in range(PEER_TOPK):
            n1 = n1 + jnp.where(s1 + v2[b] >= tau, 1.0, 0.0)
        r2_ref[h] = r2.astype(BF16)
        e2_ref[h] = jnp.exp(s2 - v2[0]).astype(BF16)
        n1_ref[h] = n1
        c1_ref[h] = jnp.exp(s1 - v1[0]) / z


def _query(h2, wq, keys):
    t, d = h2.shape
    tb = min(256, t)
    ospec = pl.BlockSpec((PEER_HEADS, N_KEYS, tb), lambda i: (0, 0, i))
    return pl.pallas_call(
        _query_kernel,
        grid=(t // tb,),
        in_specs=[pl.BlockSpec((tb, d), lambda i: (i, 0)),
                  _single((d, PEER_HEADS * D_QUERY), lambda i: (0, 0)),
                  _single((PEER_HEADS, 2, N_KEYS, HALF_QUERY), lambda i: (0, 0, 0, 0))],
        out_specs=[ospec] * 4,
        out_shape=[jax.ShapeDtypeStruct((PEER_HEADS, N_KEYS, t), dt) for dt in (BF16, BF16, F32, F32)],
        compiler_params=_params(("parallel",)),
        name="peerquery",
    )(h2, wq, keys)


def _gelu(a):
    return 0.5 * a * (1.0 + lax.erf(a * np.float32(math.sqrt(0.5))))


def _peer_kernel(h2_ref, u_ref, vt_ref, r2_ref, e2_ref, n1_ref, c1_ref, x1_ref, mod_ref,
                 lng_ref, lnb_ref, o_ref, acc_ref, *, alpha):
    e = pl.program_id(1)

    @pl.when(e == 0)
    def _():
        acc_ref[...] = jnp.zeros_like(acc_ref)

    chunks = [slice(s, s + PEER_TOKEN_CHUNK) for s in range(0, h2_ref.shape[0], PEER_TOKEN_CHUNK)]
    ats = [_dot_nt(u_ref[...], h2_ref[ls, :]) for ls in chunks]
    for ls, at in zip(chunks, ats):
        parts = []
        for ii in range(u_ref.shape[0] // N_KEYS):
            g = None
            for h in range(PEER_HEADS):
                n_row = n1_ref[h, ii:ii + 1, ls].astype(BF16)
                c_row = c1_ref[h, ii:ii + 1, ls].astype(BF16)
                w = jnp.where(r2_ref[h, :, ls] < n_row, e2_ref[h, :, ls], jnp.zeros((), BF16)) * c_row
                g = w if g is None else g + w
            a = at[ii * N_KEYS:(ii + 1) * N_KEYS, :]
            parts.append(_gelu(a).astype(BF16) * g)
        pt = jnp.concatenate(parts, axis=0)
        acc_ref[:, ls] += _dot(vt_ref[...], pt)

    @pl.when(e == pl.num_programs(1) - 1)
    def _():
        m = mod_ref[0]
        ffn = acc_ref[...].T
        o_ref[...] = _layer_norm(alpha * x1_ref[...] + m[5:6] * ffn, lng_ref[...], lnb_ref[...])


def _peer(h2, u, vt, r2, e2, n1, c1, x1, mod3, lng, lnb, seq, alpha):
    t, d = h2.shape
    n_exp = u.shape[0]
    tb = min(512, seq)
    eb = 1024
    per_batch = seq // tb
    kb = eb // N_KEYS
    full = _single((PEER_HEADS, N_KEYS, tb), lambda i, j: (0, 0, i))
    part = pl.BlockSpec((PEER_HEADS, kb, tb), lambda i, j: (0, j, i))
    return pl.pallas_call(
        functools.partial(_peer_kernel, alpha=alpha),
        grid=(t // tb, n_exp // eb),
        in_specs=[_single((tb, d), lambda i, j: (i, 0)),
                  pl.BlockSpec((eb, d), lambda i, j: (j, 0)),
                  pl.BlockSpec((d, eb), lambda i, j: (0, j)),
                  full, full, part, part,
                  _single((tb, d), lambda i, j: (i, 0)),
                  pl.BlockSpec((1, 6, d), lambda i, j: (i // per_batch, 0, 0)),
                  pl.BlockSpec((1, d), lambda i, j: (0, 0)),
                  pl.BlockSpec((1, d), lambda i, j: (0, 0))],
        out_specs=pl.BlockSpec((tb, d), lambda i, j: (i, 0)),
        out_shape=jax.ShapeDtypeStruct((t, d), F32),
        scratch_shapes=[pltpu.VMEM((d, tb), F32)],
        compiler_params=_params(("parallel", "arbitrary")),
        name="peer",
    )(h2, u, vt, r2, e2, n1, c1, x1, mod3, lng, lnb)


def _one_hot_cols(n_rows, n_cols, group_size, row_offset=0):
    r = np.arange(n_rows)[:, None]
    c = np.arange(n_cols)[None, :]
    return jnp.asarray(r == row_offset + c // group_size, dtype=BF16)


def _row(v):
    return v.reshape(1, -1).astype(F32)


def _pad_lanes(v):
    return jnp.pad(v, (0, LANES - v.shape[0])).reshape(1, LANES).astype(F32)


def kernel(x, c, w_ada, b_ada, w_in, conv_ssd_w, conv_ssd_b, dt_bias_f, dt_bias_b, a_log_f, a_log_b, d_skip, ssd_norm_g, short_conv_w, sc_norm_g, w_out, ln1_g, ln1_b, w_query, sub_keys, expert_u, expert_v, ln2_g, ln2_b):
    batch, seq, d = x.shape
    depth = w_ada.shape[0]
    alpha = (2.0 * depth) ** 0.25
    t = batch * seq
    x2 = x.reshape(t, d)
    c_pad = jnp.pad(c, ((0, SUBLANES - batch % SUBLANES), (0, 0))) if batch % SUBLANES else c

    e_f = _one_hot_cols(LANES, SSD_WIDTH, SSD_HEADDIM, 0)
    e_b = _one_hot_cols(LANES, SSD_WIDTH, SSD_HEADDIM, SSD_HEADS)
    e4 = _one_hot_cols(LANES, SSD_WIDTH, SSD_WIDTH // SSD_GROUPS)
    e32 = _one_hot_cols(LANES, SC_WIDTH, SC_WIDTH // SC_GROUPS)
    r4 = e4.T
    r32 = e32.T

    o1 = SSD_WIDTH
    o2 = o1 + XBC_WIDTH
    o3 = o2 + 2 * SSD_HEADS
    o4 = o3 + SC_WIDTH
    o5 = o4 + SC_WIDTH
    for i in range(depth):
        mod = _ada(c_pad, w_ada[i], b_ada[i].reshape(1, -1))
        mod3 = mod[:batch].reshape(batch, 6, d)

        w = w_in[i]
        w_main = jnp.concatenate([w[:, :o1], w[:, o3:o4], w[:, o4:o5], w[:, o5:], w[:, o1:o2]],
                                 axis=1).astype(BF16)
        w_dt = jnp.pad(w[:, o2:o3], ((0, 0), (0, LANES - 2 * SSD_HEADS)))
        wd_hi, wd_lo = _split2(w_dt)
        proj, dt_raw = _inproj(x2, mod3, w_main, wd_hi, wd_lo, seq)

        xbc = _conv(proj, conv_ssd_w[i], conv_ssd_b[i].reshape(1, -1), seq)
        bias = _pad_lanes(jnp.concatenate([dt_bias_f[i], dt_bias_b[i]]))
        alog = _pad_lanes(jnp.concatenate([a_log_f[i], a_log_b[i]]))
        dskip_x = _row(jnp.repeat(d_skip[i], SSD_HEADDIM))
        y_f, y_b = _ssd(xbc, dt_raw, bias, alog, dskip_x, e_f, e_b, batch, seq)

        x1, h2 = _post(y_f, y_b, proj, x2, mod3, short_conv_w[i], _row(ssd_norm_g[i]), _row(sc_norm_g[i]),
                       r4, e4, r32, e32, w_out[i].astype(BF16), _row(ln1_g[i]), _row(ln1_b[i]), seq, alpha)

        r2, e2, n1, c1 = _query(h2, w_query[i].astype(BF16), sub_keys[i].astype(BF16))
        x2 = _peer(h2, expert_u[i].astype(BF16), expert_v[i].T.astype(BF16), r2, e2, n1, c1,
                   x1, mod3, _row(ln2_g[i]), _row(ln2_b[i]), seq, alpha)
    return x2.reshape(batch, seq, d)
```

```python
import functools
import math

import jax
import jax.numpy as jnp
import numpy as np
from jax import lax
from jax.experimental import pallas as pl
from jax.experimental.pallas import tpu as pltpu

F32 = jnp.float32
BF16 = jnp.bfloat16

D_MODEL = 2048
SSD_WIDTH = D_MODEL
SSD_HEADDIM = 64
SSD_HEADS = SSD_WIDTH // SSD_HEADDIM
SSD_GROUPS = 4
SSD_STATE = 128
SSD_CHUNK = 128
HEADS_PER_GROUP = SSD_HEADS // SSD_GROUPS
GROUP_WIDTH = HEADS_PER_GROUP * SSD_HEADDIM
SC_WIDTH = D_MODEL
SC_GROUPS = 32
XBC_WIDTH = SSD_WIDTH + 2 * SSD_GROUPS * SSD_STATE
MAIN_COLS = 4 * D_MODEL + XBC_WIDTH
XBC_COL0 = 4 * D_MODEL
PEER_HEADS = 8
N_KEYS = 128
PEER_TOPK = 16
D_QUERY = 512
HALF_QUERY = D_QUERY // 2
PEER_TOKEN_CHUNK = 256
NORM_EPS = 1e-5
LANES = 128
SUBLANES = 8
HALO_ROWS = 16
VMEM_LIMIT = 56 * 1024 * 1024


def _dot(a, b):
    return jnp.dot(a, b, preferred_element_type=F32)


def _dot_nt(a, b):
    return lax.dot_general(a, b, (((1,), (1,)), ((), ())), preferred_element_type=F32)


def _split2(x):
    hi = x.astype(BF16)
    lo = (x - hi.astype(F32)).astype(BF16)
    return hi, lo


def _split3(x):
    hi = x.astype(BF16)
    r = x - hi.astype(F32)
    mid = r.astype(BF16)
    lo = (r - mid.astype(F32)).astype(BF16)
    return hi, mid, lo


def _dot_lhs_split2(x, m_bf16):
    hi, lo = _split2(x)
    return _dot(hi, m_bf16) + _dot(lo, m_bf16)


def _softplus(x):
    return jnp.maximum(x, 0.0) + jnp.log1p(jnp.exp(-jnp.abs(x)))


def _silu(x):
    return x * (1.0 / (1.0 + jnp.exp(-x)))


def _params(sem):
    return pltpu.CompilerParams(dimension_semantics=sem, vmem_limit_bytes=VMEM_LIMIT)


def _ada_kernel(c_ref, w_ref, b_ref, o_ref):
    sc = _silu(c_ref[...])
    o_ref[...] = _dot(sc.astype(BF16), w_ref[...].astype(BF16)) + b_ref[...]


def _ada(c_pad, w, b):
    rows, d = c_pad.shape
    n = w.shape[1]
    bn = 1024
    return pl.pallas_call(
        _ada_kernel,
        grid=(n // bn,),
        in_specs=[pl.BlockSpec((rows, d), lambda j: (0, 0)),
                  pl.BlockSpec((d, bn), lambda j: (0, j)),
                  pl.BlockSpec((1, bn), lambda j: (0, j))],
        out_specs=pl.BlockSpec((rows, bn), lambda j: (0, j)),
        out_shape=jax.ShapeDtypeStruct((rows, n), F32),
        compiler_params=_params(("arbitrary",)),
        name="ada",
    )(c_pad, w, b)


def _inproj_kernel(x_ref, mod_ref, w_ref, wdh_ref, wdl_ref, o_ref, dt_ref, h_scr):
    @pl.when(pl.program_id(1) == 0)
    def _():
        m = mod_ref[0]
        h = x_ref[...] * (1.0 + m[1:2]) + m[0:1]
        hi, lo = _split2(h)
        h_scr[...] = hi
        dt_ref[...] = _dot(hi, wdh_ref[...]) + _dot(lo, wdh_ref[...]) + _dot(hi, wdl_ref[...])

    o_ref[...] = _dot(h_scr[...], w_ref[...]).astype(o_ref.dtype)


def _inproj(x2, mod3, w_main, wd_hi, wd_lo, seq):
    t, d = x2.shape
    n = w_main.shape[1]
    bm = min(1024, seq)
    bn = 1024
    per_batch = seq // bm
    return pl.pallas_call(
        _inproj_kernel,
        grid=(t // bm, n // bn),
        in_specs=[pl.BlockSpec((bm, d), lambda i, j: (i, 0)),
                  pl.BlockSpec((1, 6, d), lambda i, j: (i // per_batch, 0, 0)),
                  pl.BlockSpec((d, bn), lambda i, j: (0, j)),
                  pl.BlockSpec((d, LANES), lambda i, j: (0, 0)),
                  pl.BlockSpec((d, LANES), lambda i, j: (0, 0))],
        out_specs=[pl.BlockSpec((bm, bn), lambda i, j: (i, j)),
                   pl.BlockSpec((bm, LANES), lambda i, j: (i, 0))],
        out_shape=[jax.ShapeDtypeStruct((t, n), BF16),
                   jax.ShapeDtypeStruct((t, LANES), F32)],
        scratch_shapes=[pltpu.VMEM((bm, d), BF16)],
        compiler_params=_params(("parallel", "arbitrary")),
        name="inproj",
    )(x2, mod3, w_main, wd_hi, wd_lo)


def _shifted_rows(u, prev_row, next_row):
    rows = u.shape[0]
    ridx = lax.broadcasted_iota(jnp.int32, u.shape, 0)
    up = jnp.where(ridx == 0, prev_row, pltpu.roll(u, 1, axis=0))
    un = jnp.where(ridx == rows - 1, next_row, pltpu.roll(u, rows - 1, axis=0))
    return up, un


def _halo_specs(bm, bw, col_of, t):
    per = bm // HALO_ROWS
    last = t // HALO_ROWS - 1
    prev = pl.BlockSpec((HALO_ROWS, bw), lambda i, j: (jnp.maximum(i * per - 1, 0), col_of(j)))
    nxt = pl.BlockSpec((HALO_ROWS, bw), lambda i, j: (jnp.minimum((i + 1) * per, last), col_of(j)))
    return prev, nxt


def _conv_kernel(u_ref, p_ref, n_ref, w_ref, b_ref, o_ref, *, bm, seq):
    i = pl.program_id(0)
    u = u_ref[...].astype(F32)
    has_prev = ((i * bm) % seq != 0).astype(F32)
    has_next = (((i + 1) * bm) % seq != 0).astype(F32)
    up, un = _shifted_rows(u, p_ref[HALO_ROWS - 1:HALO_ROWS, :].astype(F32) * has_prev,
                           n_ref[0:1, :].astype(F32) * has_next)
    w = w_ref[...]
    o_ref[...] = _silu(up * w[0:1] + u * w[1:2] + un * w[2:3] + b_ref[...])


def _conv(proj, conv_w, conv_b, seq):
    t = proj.shape[0]
    bm = min(512, seq)
    bw = 1024
    col0 = XBC_COL0 // bw
    col_of = lambda j: col0 + j
    prev, nxt = _halo_specs(bm, bw, col_of, t)
    return pl.pallas_call(
        functools.partial(_conv_kernel, bm=bm, seq=seq),
        grid=(t // bm, XBC_WIDTH // bw),
        in_specs=[pl.BlockSpec((bm, bw), lambda i, j: (i, col_of(j))), prev, nxt,
                  pl.BlockSpec((3, bw), lambda i, j: (0, j)),
                  pl.BlockSpec((1, bw), lambda i, j: (0, j))],
        out_specs=pl.BlockSpec((bm, bw), lambda i, j: (i, j)),
        out_shape=jax.ShapeDtypeStruct((t, XBC_WIDTH), F32),
        compiler_params=_params(("parallel", "parallel")),
        name="ssdconv",
    )(proj, proj, proj, conv_w, conv_b)


def _ssd_direction(x_ref, b_ref, c_ref, dt_ref, e_ref, h_scr, y_ref, bias, a, dskip, backward):
    cs_len = SSD_CHUNK
    row = lax.broadcasted_iota(jnp.int32, (cs_len, cs_len), 0)
    col = lax.broadcasted_iota(jnp.int32, (cs_len, cs_len), 1)
    tri = (col <= row).astype(BF16)
    dt = _softplus(dt_ref[...] + bias)
    adt = dt * a
    h3 = _split3(adt)
    cs = _dot(tri, h3[0]) + _dot(tri, h3[1]) + _dot(tri, h3[2])
    tot = cs[cs_len - 1:cs_len, :]
    if backward:
        ecs = cs - adt
        p = -ecs
        wst = dt * jnp.exp(ecs)
        indec = jnp.exp(tot - ecs)
        mask = col >= row
    else:
        p = cs
        wst = dt * jnp.exp(tot - cs)
        indec = jnp.exp(cs)
        mask = col <= row
    pt = p.T
    e = e_ref[...]
    dtx = _dot(dt.astype(BF16), e)
    wstx = _dot(wst.astype(BF16), e)
    indx = _dot(indec.astype(BF16), e)
    xs = x_ref[...]
    xd = (xs * dtx).astype(BF16)
    xw = (xs * wstx).astype(BF16)
    col_off = SSD_HEADS if backward else 0
    pair = 2 * SSD_HEADDIM
    first_of_pair = lax.broadcasted_iota(jnp.int32, (cs_len, pair), 1) < SSD_HEADDIM
    zero = jnp.zeros((), BF16)
    for g in range(SSD_GROUPS):
        gs = slice(g * GROUP_WIDTH, (g + 1) * GROUP_WIDTH)
        bg = b_ref[:, g * SSD_STATE:(g + 1) * SSD_STATE]
        cg = c_ref[:, g * SSD_STATE:(g + 1) * SSD_STATE].astype(BF16)
        cb = _dot_nt(cg, bg.astype(BF16))
        h_in = h_scr[g]
        y_off = _dot(cg, h_in.astype(BF16)) * indx[:, gs]
        st = _dot(bg.T.astype(BF16), xw[:, gs])
        for r in range(0, HEADS_PER_GROUP, 2):
            hd = g * HEADS_PER_GROUP + r
            ms = []
            for ci in (hd + col_off, hd + col_off + 1):
                seg = p[:, ci:ci + 1] - pt[ci:ci + 1, :]
                lmat = jnp.exp(jnp.where(mask, seg, -jnp.inf))
                ms.append((cb * lmat).astype(BF16))
            hs = slice(hd * SSD_HEADDIM, (hd + 2) * SSD_HEADDIM)
            x2h = xd[:, hs]
            rhs = jnp.concatenate([jnp.where(first_of_pair, x2h, zero), jnp.where(first_of_pair, zero, x2h)], axis=0)
            y = _dot(jnp.concatenate(ms, axis=1), rhs) + y_off[:, r * SSD_HEADDIM:(r + 2) * SSD_HEADDIM]
            if dskip is not None:
                y = y + xs[:, hs] * dskip[:, hs]
            y_ref[:, hs] = y
        edge = 0 if backward else cs_len - 1
        h_scr[g] = indx[edge:edge + 1, gs] * h_in + st


def _ssd_kernel(xf_ref, bf_ref, cf_ref, dtf_ref, xb_ref, bb_ref, cb_ref, dtb_ref,
                bias_ref, alog_ref, dskip_ref, ef_ref, eb_ref, yf_ref, yb_ref, hf_scr, hb_scr):
    @pl.when(pl.program_id(1) == 0)
    def _():
        hf_scr[...] = jnp.zeros_like(hf_scr)
        hb_scr[...] = jnp.zeros_like(hb_scr)

    bias = bias_ref[...]
    a = -jnp.exp(alog_ref[...])
    _ssd_direction(xf_ref, bf_ref, cf_ref, dtf_ref, ef_ref, hf_scr, yf_ref, bias, a, dskip_ref[...], False)
    _ssd_direction(xb_ref, bb_ref, cb_ref, dtb_ref, eb_ref, hb_scr, yb_ref, bias, a, None, True)


def _ssd(xbc, dt_raw, bias, alog, dskip_x, e_f, e_b, batch, seq):
    t = xbc.shape[0]
    nc = seq // SSD_CHUNK
    cl = SSD_CHUNK
    gn = SSD_GROUPS * SSD_STATE
    bcol = SSD_WIDTH // gn
    fwd = lambda b, k: b * nc + k
    bwd = lambda b, k: b * nc + (nc - 1 - k)

    def specs(ch):
        return [pl.BlockSpec((cl, SSD_WIDTH), lambda b, k: (ch(b, k), 0)),
                pl.BlockSpec((cl, gn), lambda b, k: (ch(b, k), bcol)),
                pl.BlockSpec((cl, gn), lambda b, k: (ch(b, k), bcol + 1)),
                pl.BlockSpec((cl, LANES), lambda b, k: (ch(b, k), 0))]

    const = lambda shape: pl.BlockSpec(shape, lambda b, k: (0, 0))
    state = pltpu.VMEM((SSD_GROUPS, SSD_STATE, GROUP_WIDTH), F32)
    return pl.pallas_call(
        _ssd_kernel,
        grid=(batch, nc),
        in_specs=specs(fwd) + specs(bwd) + [const((1, LANES)), const((1, LANES)), const((1, SSD_WIDTH)),
                                            const((LANES, SSD_WIDTH)), const((LANES, SSD_WIDTH))],
        out_specs=[pl.BlockSpec((cl, SSD_WIDTH), lambda b, k: (fwd(b, k), 0)),
                   pl.BlockSpec((cl, SSD_WIDTH), lambda b, k: (bwd(b, k), 0))],
        out_shape=[jax.ShapeDtypeStruct((t, SSD_WIDTH), F32)] * 2,
        scratch_shapes=[state, state],
        compiler_params=_params(("parallel", "arbitrary")),
        name="ssd",
    )(xbc, xbc, xbc, dt_raw, xbc, xbc, xbc, dt_raw, bias, alog, dskip_x, e_f, e_b)


def _group_rms(y, r_ref, e_ref, group_size):
    gsum = _dot_lhs_split2(y * y, r_ref[...])
    inv = lax.rsqrt(gsum * (1.0 / group_size) + NORM_EPS)
    return y * _dot(inv.astype(BF16), e_ref[...])


def _layer_norm(v, g, b):
    mu = jnp.mean(v, axis=-1, keepdims=True)
    vc = v - mu
    var = jnp.mean(vc * vc, axis=-1, keepdims=True)
    return vc * lax.rsqrt(var + NORM_EPS) * g + b


def _post_kernel(yf_ref, yb_ref, z_ref, gb_ref, gc_ref, gcp_ref, gcn_ref, v_ref, vp_ref, vn_ref,
                 x_ref, mod_ref, scw_ref, ssdg_ref, scg_ref, r4_ref, e4_ref, r32_ref, e32_ref,
                 wout_ref, lng_ref, lnb_ref, x1_ref, h2_ref, *, bm, seq, alpha):
    i = pl.program_id(0)
    m = mod_ref[0]
    y = (yf_ref[...] + yb_ref[...]) * _silu(z_ref[...].astype(F32))
    y_ssd = _group_rms(y, r4_ref, e4_ref, SSD_WIDTH // SSD_GROUPS) * ssdg_ref[...]

    f32 = lambda ref, rows=slice(None): ref[rows, :].astype(F32)
    u = f32(gc_ref) * f32(v_ref)
    has_prev = ((i * bm) % seq != 0).astype(F32)
    has_next = (((i + 1) * bm) % seq != 0).astype(F32)
    last = slice(HALO_ROWS - 1, HALO_ROWS)
    first = slice(0, 1)
    up, un = _shifted_rows(u, f32(gcp_ref, last) * f32(vp_ref, last) * has_prev,
                           f32(gcn_ref, first) * f32(vn_ref, first) * has_next)
    w = scw_ref[...]
    y_sc = f32(gb_ref) * (up * w[0:1] + u * w[1:2] + un * w[2:3])
    y_sc = _group_rms(y_sc, r32_ref, e32_ref, SC_WIDTH // SC_GROUPS) * scg_ref[...]

    mix = (_dot(y_ssd.astype(BF16), wout_ref[0:SSD_WIDTH, :])
           + _dot(y_sc.astype(BF16), wout_ref[SSD_WIDTH:SSD_WIDTH + SC_WIDTH, :]))
    x1 = _layer_norm(alpha * x_ref[...] + m[2:3] * mix, lng_ref[...], lnb_ref[...])
    x1_ref[...] = x1
    h2_ref[...] = (x1 * (1.0 + m[4:5]) + m[3:4]).astype(BF16)


def _single(shape, index_map):
    return pl.BlockSpec(shape, index_map, pipeline_mode=pl.Buffered(1))


def _post(yf, yb, proj, x2, mod3, scw, ssdg, scg, r4, e4, r32, e32, wout, lng, lnb, seq, alpha):
    t, d = x2.shape
    bm = min(256, seq)
    per_batch = seq // bm
    per = bm // HALO_ROWS
    last = t // HALO_ROWS - 1
    main = lambda c: pl.BlockSpec((bm, d), lambda i: (i, c))
    prev = lambda c: pl.BlockSpec((HALO_ROWS, d), lambda i: (jnp.maximum(i * per - 1, 0), c))
    nxt = lambda c: pl.BlockSpec((HALO_ROWS, d), lambda i: (jnp.minimum((i + 1) * per, last), c))
    const = lambda shape: _single(shape, lambda i: (0, 0))
    return pl.pallas_call(
        functools.partial(_post_kernel, bm=bm, seq=seq, alpha=alpha),
        grid=(t // bm,),
        in_specs=[main(0), main(0), main(0), main(1), main(2), prev(2), nxt(2), main(3), prev(3), nxt(3),
                  main(0), pl.BlockSpec((1, 6, d), lambda i: (i // per_batch, 0, 0)),
                  const((3, d)), const((1, d)), const((1, d)),
                  const((d, LANES)), const((LANES, d)), const((d, LANES)), const((LANES, d)),
                  const((2 * d, d)), const((1, d)), const((1, d))],
        out_specs=[pl.BlockSpec((bm, d), lambda i: (i, 0)), pl.BlockSpec((bm, d), lambda i: (i, 0))],
        out_shape=[jax.ShapeDtypeStruct((t, d), F32), jax.ShapeDtypeStruct((t, d), BF16)],
        compiler_params=_params(("parallel",)),
        name="post",
    )(yf, yb, proj, proj, proj, proj, proj, proj, proj, proj, x2, mod3,
      scw, ssdg, scg, r4, e4, r32, e32, wout, lng, lnb)


def _top_values(s, n, want_rank=False):
    vals = []
    rank = jnp.full(s.shape, float(n), F32) if want_rank else None
    for r in range(n):
        m = jnp.max(s, axis=0, keepdims=True)
        vals.append(m)
        hit = s == m
        if want_rank:
            rank = jnp.where(hit, float(r), rank)
        s = jnp.where(hit, -jnp.inf, s)
    return (vals, rank) if want_rank else vals


def _query_kernel(h2_ref, wq_ref, k_ref, r2_ref, e2_ref, n1_ref, c1_ref):
    q = _dot(h2_ref[...], wq_ref[...]).astype(BF16)
    tb = q.shape[0]
    for h in range(PEER_HEADS):
        q1 = q[:, h * D_QUERY:h * D_QUERY + HALF_QUERY]
        q2 = q[:, h * D_QUERY + HALF_QUERY:(h + 1) * D_QUERY]
        s1 = _dot_nt(k_ref[h, 0], q1)
        s2 = _dot_nt(k_ref[h, 1], q2)
        v1 = _top_values(s1, PEER_TOPK)
        v2, r2 = _top_values(s2, PEER_TOPK, want_rank=True)
        pairs = [(a, b) for a in range(PEER_TOPK) for b in range(PEER_TOPK) if (a + 1) * (b + 1) <= PEER_TOPK]
        sums = {ab: v1[ab[0]] + v2[ab[1]] for ab in pairs}
        rows = [sums[ab] for ab in pairs]
        rows += [jnp.full((1, tb), -jnp.inf, F32)] * ((-len(rows)) % SUBLANES)
        top = _top_values(jnp.concatenate(rows, axis=0), PEER_TOPK)
        z = jnp.ones((1, tb), F32)
        for kk in range(1, PEER_TOPK):
            z = z + jnp.exp(top[kk] - top[0])
        tau = top[PEER_TOPK - 1]
        n1 = jnp.zeros(s1.shape, F32)
        for a in range(PEER_TOPK):
            cnt = sum(jnp.where(sums[(a, b)] >= tau, 1.0, 0.0) for b in range(PEER_TOPK) if (a, b) in sums)
            n1 = jnp.where(s1 == v1[a], cnt, n1)
        r2_ref[h] = r2.astype(BF16)
        e2_ref[h] = jnp.exp(s2 - v2[0]).astype(BF16)
        n1_ref[h] = n1
        c1_ref[h] = jnp.exp(s1 - v1[0]) / z


def _query(h2, wq, keys):
    t, d = h2.shape
    tb = min(256, t)
    ospec = pl.BlockSpec((PEER_HEADS, N_KEYS, tb), lambda i: (0, 0, i))
    return pl.pallas_call(
        _query_kernel,
        grid=(t // tb,),
        in_specs=[pl.BlockSpec((tb, d), lambda i: (i, 0)),
                  _single((d, PEER_HEADS * D_QUERY), lambda i: (0, 0)),
                  _single((PEER_HEADS, 2, N_KEYS, HALF_QUERY), lambda i: (0, 0, 0, 0))],
        out_specs=[ospec] * 4,
        out_shape=[jax.ShapeDtypeStruct((PEER_HEADS, N_KEYS, t), dt) for dt in (BF16, BF16, F32, F32)],
        compiler_params=_params(("parallel",)),
        name="peerquery",
    )(h2, wq, keys)


def _gelu(a):
    return 0.5 * a * (1.0 + lax.erf(a * np.float32(math.sqrt(0.5))))


def _peer_kernel(h2_ref, u_ref, vt_ref, r2_ref, e2_ref, n1_ref, c1_ref, x1_ref, mod_ref,
                 lng_ref, lnb_ref, o_ref, acc_ref, *, alpha):
    e = pl.program_id(1)

    @pl.when(e == 0)
    def _():
        acc_ref[...] = jnp.zeros_like(acc_ref)

    chunks = [slice(s, s + PEER_TOKEN_CHUNK) for s in range(0, h2_ref.shape[0], PEER_TOKEN_CHUNK)]
    ats = [_dot_nt(u_ref[...], h2_ref[ls, :]) for ls in chunks]
    for ls, at in zip(chunks, ats):
        parts = []
        for ii in range(u_ref.shape[0] // N_KEYS):
            g = None
            for h in range(PEER_HEADS):
                n_row = n1_ref[h, ii:ii + 1, ls].astype(BF16)
                c_row = c1_ref[h, ii:ii + 1, ls].astype(BF16)
                w = jnp.where(r2_ref[h, :, ls] < n_row, e2_ref[h, :, ls], jnp.zeros((), BF16)) * c_row
                g = w if g is None else g + w
            a = at[ii * N_KEYS:(ii + 1) * N_KEYS, :]
            parts.append(_gelu(a).astype(BF16) * g)
        pt = jnp.concatenate(parts, axis=0)
        acc_ref[:, ls] += _dot(vt_ref[...], pt)

    @pl.when(e == pl.num_programs(1) - 1)
    def _():
        m = mod_ref[0]
        ffn = acc_ref[...].T
        o_ref[...] = _layer_norm(alpha * x1_ref[...] + m[5:6] * ffn, lng_ref[...], lnb_ref[...])


def _peer(h2, u, vt, r2, e2, n1, c1, x1, mod3, lng, lnb, seq, alpha):
    t, d = h2.shape
    n_exp = u.shape[0]
    tb = min(512, seq)
    eb = 1024
    per_batch = seq // tb
    kb = eb // N_KEYS
    full = _single((PEER_HEADS, N_KEYS, tb), lambda i, j: (0, 0, i))
    part = pl.BlockSpec((PEER_HEADS, kb, tb), lambda i, j: (0, j, i))
    return pl.pallas_call(
        functools.partial(_peer_kernel, alpha=alpha),
        grid=(t // tb, n_exp // eb),
        in_specs=[_single((tb, d), lambda i, j: (i, 0)),
                  pl.BlockSpec((eb, d), lambda i, j: (j, 0)),
                  pl.BlockSpec((d, eb), lambda i, j: (0, j)),
                  full, full, part, part,
                  _single((tb, d), lambda i, j: (i, 0)),
                  pl.BlockSpec((1, 6, d), lambda i, j: (i // per_batch, 0, 0)),
                  pl.BlockSpec((1, d), lambda i, j: (0, 0)),
                  pl.BlockSpec((1, d), lambda i, j: (0, 0))],
        out_specs=pl.BlockSpec((tb, d), lambda i, j: (i, 0)),
        out_shape=jax.ShapeDtypeStruct((t, d), F32),
        scratch_shapes=[pltpu.VMEM((d, tb), F32)],
        compiler_params=_params(("parallel", "arbitrary")),
        name="peer",
    )(h2, u, vt, r2, e2, n1, c1, x1, mod3, lng, lnb)


def _one_hot_cols(n_rows, n_cols, group_size, row_offset=0):
    r = np.arange(n_rows)[:, None]
    c = np.arange(n_cols)[None, :]
    return jnp.asarray(r == row_offset + c // group_size, dtype=BF16)


def _row(v):
    return v.reshape(1, -1).astype(F32)


def _pad_lanes(v):
    return jnp.pad(v, (0, LANES - v.shape[0])).reshape(1, LANES).astype(F32)


def kernel(x, c, w_ada, b_ada, w_in, conv_ssd_w, conv_ssd_b, dt_bias_f, dt_bias_b, a_log_f, a_log_b, d_skip, ssd_norm_g, short_conv_w, sc_norm_g, w_out, ln1_g, ln1_b, w_query, sub_keys, expert_u, expert_v, ln2_g, ln2_b):
    batch, seq, d = x.shape
    depth = w_ada.shape[0]
    alpha = (2.0 * depth) ** 0.25
    t = batch * seq
    x2 = x.reshape(t, d)
    c_pad = jnp.pad(c, ((0, SUBLANES - batch % SUBLANES), (0, 0))) if batch % SUBLANES else c

    e_f = _one_hot_cols(LANES, SSD_WIDTH, SSD_HEADDIM, 0)
    e_b = _one_hot_cols(LANES, SSD_WIDTH, SSD_HEADDIM, SSD_HEADS)
    e4 = _one_hot_cols(LANES, SSD_WIDTH, SSD_WIDTH // SSD_GROUPS)
    e32 = _one_hot_cols(LANES, SC_WIDTH, SC_WIDTH // SC_GROUPS)
    r4 = e4.T
    r32 = e32.T

    o1 = SSD_WIDTH
    o2 = o1 + XBC_WIDTH
    o3 = o2 + 2 * SSD_HEADS
    o4 = o3 + SC_WIDTH
    o5 = o4 + SC_WIDTH
    for i in range(depth):
        mod = _ada(c_pad, w_ada[i], b_ada[i].reshape(1, -1))
        mod3 = mod[:batch].reshape(batch, 6, d)

        w = w_in[i]
        w_main = jnp.concatenate([w[:, :o1], w[:, o3:o4], w[:, o4:o5], w[:, o5:], w[:, o1:o2]],
                                 axis=1).astype(BF16)
        w_dt = jnp.pad(w[:, o2:o3], ((0, 0), (0, LANES - 2 * SSD_HEADS)))
        wd_hi, wd_lo = _split2(w_dt)
        proj, dt_raw = _inproj(x2, mod3, w_main, wd_hi, wd_lo, seq)

        xbc = _conv(proj, conv_ssd_w[i], conv_ssd_b[i].reshape(1, -1), seq)
        bias = _pad_lanes(jnp.concatenate([dt_bias_f[i], dt_bias_b[i]]))
        alog = _pad_lanes(jnp.concatenate([a_log_f[i], a_log_b[i]]))
        dskip_x = _row(jnp.repeat(d_skip[i], SSD_HEADDIM))
        y_f, y_b = _ssd(xbc, dt_raw, bias, alog, dskip_x, e_f, e_b, batch, seq)

        x1, h2 = _post(y_f, y_b, proj, x2, mod3, short_conv_w[i], _row(ssd_norm_g[i]), _row(sc_norm_g[i]),
                       r4, e4, r32, e32, w_out[i].astype(BF16), _row(ln1_g[i]), _row(ln1_b[i]), seq, alpha)

        r2, e2, n1, c1 = _query(h2, w_query[i].astype(BF16), sub_keys[i].astype(BF16))
        x2 = _peer(h2, expert_u[i].astype(BF16), expert_v[i].T.astype(BF16), r2, e2, n1, c1,
                   x1, mod3, _row(ln2_g[i]), _row(ln2_b[i]), seq, alpha)
    return x2.reshape(batch, seq, d)
```

```python
import functools
import math

import jax
import jax.numpy as jnp
import numpy as np
from jax import lax
from jax.experimental import pallas as pl
from jax.experimental.pallas import tpu as pltpu

F32 = jnp.float32
BF16 = jnp.bfloat16
FP8 = jnp.float8_e4m3fn
FP8_TARGET = 256.0
FP8_SCALE_CAP = 2.0 ** 60

D_MODEL = 2048
SSD_WIDTH = D_MODEL
SSD_HEADDIM = 64
SSD_HEADS = SSD_WIDTH // SSD_HEADDIM
SSD_GROUPS = 4
SSD_STATE = 128
SSD_CHUNK = 128
HEADS_PER_GROUP = SSD_HEADS // SSD_GROUPS
GROUP_WIDTH = HEADS_PER_GROUP * SSD_HEADDIM
SC_WIDTH = D_MODEL
SC_GROUPS = 32
XBC_WIDTH = SSD_WIDTH + 2 * SSD_GROUPS * SSD_STATE
MAIN_COLS = 4 * D_MODEL + XBC_WIDTH
XBC_COL0 = 4 * D_MODEL
PEER_HEADS = 8
N_KEYS = 128
PEER_TOPK = 16
D_QUERY = 512
HALF_QUERY = D_QUERY // 2
PEER_TOKEN_CHUNK = 256
NORM_EPS = 1e-5
LANES = 128
SUBLANES = 8
HALO_ROWS = 16
VMEM_LIMIT = 56 * 1024 * 1024


def _dot(a, b):
    return jnp.dot(a, b, preferred_element_type=F32)


def _dot_nt(a, b):
    return lax.dot_general(a, b, (((1,), (1,)), ((), ())), preferred_element_type=F32)


def _split2(x):
    hi = x.astype(BF16)
    lo = (x - hi.astype(F32)).astype(BF16)
    return hi, lo


def _split3(x):
    hi = x.astype(BF16)
    r = x - hi.astype(F32)
    mid = r.astype(BF16)
    lo = (r - mid.astype(F32)).astype(BF16)
    return hi, mid, lo


def _dot_lhs_split2(x, m_bf16):
    hi, lo = _split2(x)
    return _dot(hi, m_bf16) + _dot(lo, m_bf16)


def _softplus(x):
    return jnp.maximum(x, 0.0) + jnp.log1p(jnp.exp(-jnp.abs(x)))


def _silu(x):
    return x * (1.0 / (1.0 + jnp.exp(-x)))


def _params(sem):
    return pltpu.CompilerParams(dimension_semantics=sem, vmem_limit_bytes=VMEM_LIMIT)


def _ada_kernel(c_ref, w_ref, b_ref, o_ref):
    sc = _silu(c_ref[...])
    o_ref[...] = _dot(sc.astype(BF16), w_ref[...].astype(BF16)) + b_ref[...]


def _ada(c_pad, w, b):
    rows, d = c_pad.shape
    n = w.shape[1]
    bn = 1024
    return pl.pallas_call(
        _ada_kernel,
        grid=(n // bn,),
        in_specs=[pl.BlockSpec((rows, d), lambda j: (0, 0)),
                  pl.BlockSpec((d, bn), lambda j: (0, j)),
                  pl.BlockSpec((1, bn), lambda j: (0, j))],
        out_specs=pl.BlockSpec((rows, bn), lambda j: (0, j)),
        out_shape=jax.ShapeDtypeStruct((rows, n), F32),
        compiler_params=_params(("arbitrary",)),
        name="ada",
    )(c_pad, w, b)


def _inproj_kernel(x_ref, mod_ref, w_ref, wdh_ref, wdl_ref, o_ref, dt_ref, h_scr):
    @pl.when(pl.program_id(1) == 0)
    def _():
        m = mod_ref[0]
        h = x_ref[...] * (1.0 + m[1:2]) + m[0:1]
        hi, lo = _split2(h)
        h_scr[...] = hi
        dt_ref[...] = _dot(hi, wdh_ref[...]) + _dot(lo, wdh_ref[...]) + _dot(hi, wdl_ref[...])

    o_ref[...] = _dot(h_scr[...], w_ref[...]).astype(o_ref.dtype)


def _inproj(x2, mod3, w_main, wd_hi, wd_lo, seq):
    t, d = x2.shape
    n = w_main.shape[1]
    bm = min(1024, seq)
    bn = 1024
    per_batch = seq // bm
    return pl.pallas_call(
        _inproj_kernel,
        grid=(t // bm, n // bn),
        in_specs=[pl.BlockSpec((bm, d), lambda i, j: (i, 0)),
                  pl.BlockSpec((1, 6, d), lambda i, j: (i // per_batch, 0, 0)),
                  pl.BlockSpec((d, bn), lambda i, j: (0, j)),
                  pl.BlockSpec((d, LANES), lambda i, j: (0, 0)),
                  pl.BlockSpec((d, LANES), lambda i, j: (0, 0))],
        out_specs=[pl.BlockSpec((bm, bn), lambda i, j: (i, j)),
                   pl.BlockSpec((bm, LANES), lambda i, j: (i, 0))],
        out_shape=[jax.ShapeDtypeStruct((t, n), BF16),
                   jax.ShapeDtypeStruct((t, LANES), F32)],
        scratch_shapes=[pltpu.VMEM((bm, d), BF16)],
        compiler_params=_params(("parallel", "arbitrary")),
        name="inproj",
    )(x2, mod3, w_main, wd_hi, wd_lo)


def _shifted_rows(u, prev_row, next_row):
    rows = u.shape[0]
    ridx = lax.broadcasted_iota(jnp.int32, u.shape, 0)
    up = jnp.where(ridx == 0, prev_row, pltpu.roll(u, 1, axis=0))
    un = jnp.where(ridx == rows - 1, next_row, pltpu.roll(u, rows - 1, axis=0))
    return up, un


def _halo_specs(bm, bw, col_of, t):
    per = bm // HALO_ROWS
    last = t // HALO_ROWS - 1
    prev = pl.BlockSpec((HALO_ROWS, bw), lambda i, j: (jnp.maximum(i * per - 1, 0), col_of(j)))
    nxt = pl.BlockSpec((HALO_ROWS, bw), lambda i, j: (jnp.minimum((i + 1) * per, last), col_of(j)))
    return prev, nxt


def _conv_kernel(u_ref, p_ref, n_ref, w_ref, b_ref, o_ref, *, bm, seq):
    i = pl.program_id(0)
    u = u_ref[...].astype(F32)
    has_prev = ((i * bm) % seq != 0).astype(F32)
    has_next = (((i + 1) * bm) % seq != 0).astype(F32)
    up, un = _shifted_rows(u, p_ref[HALO_ROWS - 1:HALO_ROWS, :].astype(F32) * has_prev,
                           n_ref[0:1, :].astype(F32) * has_next)
    w = w_ref[...]
    o_ref[...] = _silu(up * w[0:1] + u * w[1:2] + un * w[2:3] + b_ref[...])


def _conv(proj, conv_w, conv_b, seq):
    t = proj.shape[0]
    bm = min(512, seq)
    bw = 1024
    col0 = XBC_COL0 // bw
    col_of = lambda j: col0 + j
    prev, nxt = _halo_specs(bm, bw, col_of, t)
    return pl.pallas_call(
        functools.partial(_conv_kernel, bm=bm, seq=seq),
        grid=(t // bm, XBC_WIDTH // bw),
        in_specs=[pl.BlockSpec((bm, bw), lambda i, j: (i, col_of(j))), prev, nxt,
                  pl.BlockSpec((3, bw), lambda i, j: (0, j)),
                  pl.BlockSpec((1, bw), lambda i, j: (0, j))],
        out_specs=pl.BlockSpec((bm, bw), lambda i, j: (i, j)),
        out_shape=jax.ShapeDtypeStruct((t, XBC_WIDTH), F32),
        compiler_params=_params(("parallel", "parallel")),
        name="ssdconv",
    )(proj, proj, proj, conv_w, conv_b)


def _ssd_direction(x_ref, b_ref, c_ref, dt_ref, e_ref, h_scr, y_ref, bias, a, dskip, backward):
    cs_len = SSD_CHUNK
    row = lax.broadcasted_iota(jnp.int32, (cs_len, cs_len), 0)
    col = lax.broadcasted_iota(jnp.int32, (cs_len, cs_len), 1)
    tri = (col <= row).astype(BF16)
    dt = _softplus(dt_ref[...] + bias)
    adt = dt * a
    h3 = _split3(adt)
    cs = _dot(tri, h3[0]) + _dot(tri, h3[1]) + _dot(tri, h3[2])
    tot = cs[cs_len - 1:cs_len, :]
    if backward:
        ecs = cs - adt
        p = -ecs
        wst = dt * jnp.exp(ecs)
        indec = jnp.exp(tot - ecs)
        mask = col >= row
    else:
        p = cs
        wst = dt * jnp.exp(tot - cs)
        indec = jnp.exp(cs)
        mask = col <= row
    pt = p.T
    e = e_ref[...]
    dtx = _dot(dt.astype(BF16), e)
    wstx = _dot(wst.astype(BF16), e)
    indx = _dot(indec.astype(BF16), e)
    xs = x_ref[...]
    xd = (xs * dtx).astype(BF16)
    xw = (xs * wstx).astype(BF16)
    col_off = SSD_HEADS if backward else 0
    pair = 2 * SSD_HEADDIM
    first_of_pair = lax.broadcasted_iota(jnp.int32, (cs_len, pair), 1) < SSD_HEADDIM
    zero = jnp.zeros((), BF16)
    for g in range(SSD_GROUPS):
        gs = slice(g * GROUP_WIDTH, (g + 1) * GROUP_WIDTH)
        bg = b_ref[:, g * SSD_STATE:(g + 1) * SSD_STATE]
        cg = c_ref[:, g * SSD_STATE:(g + 1) * SSD_STATE].astype(BF16)
        cb = _dot_nt(cg, bg.astype(BF16))
        h_in = h_scr[g]
        y_off = _dot(cg, h_in.astype(BF16)) * indx[:, gs]
        st = _dot(bg.T.astype(BF16), xw[:, gs])
        for r in range(0, HEADS_PER_GROUP, 2):
            hd = g * HEADS_PER_GROUP + r
            ms = []
            for ci in (hd + col_off, hd + col_off + 1):
                seg = p[:, ci:ci + 1] - pt[ci:ci + 1, :]
                lmat = jnp.exp(jnp.where(mask, seg, -jnp.inf))
                ms.append((cb * lmat).astype(BF16))
            hs = slice(hd * SSD_HEADDIM, (hd + 2) * SSD_HEADDIM)
            x2h = xd[:, hs]
            rhs = jnp.concatenate([jnp.where(first_of_pair, x2h, zero), jnp.where(first_of_pair, zero, x2h)], axis=0)
            y = _dot(jnp.concatenate(ms, axis=1), rhs) + y_off[:, r * SSD_HEADDIM:(r + 2) * SSD_HEADDIM]
            if dskip is not None:
                y = y + xs[:, hs] * dskip[:, hs]
            y_ref[:, hs] = y
        edge = 0 if backward else cs_len - 1
        h_scr[g] = indx[edge:edge + 1, gs] * h_in + st


def _ssd_kernel(xf_ref, bf_ref, cf_ref, dtf_ref, xb_ref, bb_ref, cb_ref, dtb_ref,
                bias_ref, alog_ref, dskip_ref, ef_ref, eb_ref, yf_ref, yb_ref, hf_scr, hb_scr):
    @pl.when(pl.program_id(1) == 0)
    def _():
        hf_scr[...] = jnp.zeros_like(hf_scr)
        hb_scr[...] = jnp.zeros_like(hb_scr)

    bias = bias_ref[...]
    a = -jnp.exp(alog_ref[...])
    _ssd_direction(xf_ref, bf_ref, cf_ref, dtf_ref, ef_ref, hf_scr, yf_ref, bias, a, dskip_ref[...], False)
    _ssd_direction(xb_ref, bb_ref, cb_ref, dtb_ref, eb_ref, hb_scr, yb_ref, bias, a, None, True)


def _ssd(xbc, dt_raw, bias, alog, dskip_x, e_f, e_b, batch, seq):
    t = xbc.shape[0]
    nc = seq // SSD_CHUNK
    cl = SSD_CHUNK
    gn = SSD_GROUPS * SSD_STATE
    bcol = SSD_WIDTH // gn
    fwd = lambda b, k: b * nc + k
    bwd = lambda b, k: b * nc + (nc - 1 - k)

    def specs(ch):
        return [pl.BlockSpec((cl, SSD_WIDTH), lambda b, k: (ch(b, k), 0)),
                pl.BlockSpec((cl, gn), lambda b, k: (ch(b, k), bcol)),
                pl.BlockSpec((cl, gn), lambda b, k: (ch(b, k), bcol + 1)),
                pl.BlockSpec((cl, LANES), lambda b, k: (ch(b, k), 0))]

    const = lambda shape: pl.BlockSpec(shape, lambda b, k: (0, 0))
    state = pltpu.VMEM((SSD_GROUPS, SSD_STATE, GROUP_WIDTH), F32)
    return pl.pallas_call(
        _ssd_kernel,
        grid=(batch, nc),
        in_specs=specs(fwd) + specs(bwd) + [const((1, LANES)), const((1, LANES)), const((1, SSD_WIDTH)),
                                            const((LANES, SSD_WIDTH)), const((LANES, SSD_WIDTH))],
        out_specs=[pl.BlockSpec((cl, SSD_WIDTH), lambda b, k: (fwd(b, k), 0)),
                   pl.BlockSpec((cl, SSD_WIDTH), lambda b, k: (bwd(b, k), 0))],
        out_shape=[jax.ShapeDtypeStruct((t, SSD_WIDTH), F32)] * 2,
        scratch_shapes=[state, state],
        compiler_params=_params(("parallel", "arbitrary")),
        name="ssd",
    )(xbc, xbc, xbc, dt_raw, xbc, xbc, xbc, dt_raw, bias, alog, dskip_x, e_f, e_b)


def _group_rms(y, r_ref, e_ref, group_size):
    gsum = _dot_lhs_split2(y * y, r_ref[...])
    inv = lax.rsqrt(gsum * (1.0 / group_size) + NORM_EPS)
    return y * _dot(inv.astype(BF16), e_ref[...])


def _layer_norm(v, g, b):
    mu = jnp.mean(v, axis=-1, keepdims=True)
    vc = v - mu
    var = jnp.mean(vc * vc, axis=-1, keepdims=True)
    return vc * lax.rsqrt(var + NORM_EPS) * g + b


def _post_kernel(yf_ref, yb_ref, z_ref, gb_ref, gc_ref, gcp_ref, gcn_ref, v_ref, vp_ref, vn_ref,
                 x_ref, mod_ref, scw_ref, ssdg_ref, scg_ref, r4_ref, e4_ref, r32_ref, e32_ref,
                 wout_ref, lng_ref, lnb_ref, x1_ref, h2_ref, amax_ref, sumsq_ref, *, bm, seq, alpha):
    i = pl.program_id(0)
    m = mod_ref[0]
    y = (yf_ref[...] + yb_ref[...]) * _silu(z_ref[...].astype(F32))
    y_ssd = _group_rms(y, r4_ref, e4_ref, SSD_WIDTH // SSD_GROUPS) * ssdg_ref[...]

    f32 = lambda ref, rows=slice(None): ref[rows, :].astype(F32)
    u = f32(gc_ref) * f32(v_ref)
    has_prev = ((i * bm) % seq != 0).astype(F32)
    has_next = (((i + 1) * bm) % seq != 0).astype(F32)
    last = slice(HALO_ROWS - 1, HALO_ROWS)
    first = slice(0, 1)
    up, un = _shifted_rows(u, f32(gcp_ref, last) * f32(vp_ref, last) * has_prev,
                           f32(gcn_ref, first) * f32(vn_ref, first) * has_next)
    w = scw_ref[...]
    y_sc = f32(gb_ref) * (up * w[0:1] + u * w[1:2] + un * w[2:3])
    y_sc = _group_rms(y_sc, r32_ref, e32_ref, SC_WIDTH // SC_GROUPS) * scg_ref[...]

    mix = (_dot(y_ssd.astype(BF16), wout_ref[0:SSD_WIDTH, :])
           + _dot(y_sc.astype(BF16), wout_ref[SSD_WIDTH:SSD_WIDTH + SC_WIDTH, :]))
    x1 = _layer_norm(alpha * x_ref[...] + m[2:3] * mix, lng_ref[...], lnb_ref[...])
    x1_ref[...] = x1
    h2 = x1 * (1.0 + m[4:5]) + m[3:4]
    h2_ref[...] = h2.astype(BF16)
    amax_ref[...] = jnp.max(jnp.abs(h2), axis=-1, keepdims=True)
    sumsq_ref[...] = jnp.sum(h2 * h2, axis=-1, keepdims=True)


def _single(shape, index_map):
    return pl.BlockSpec(shape, index_map, pipeline_mode=pl.Buffered(1))


def _post(yf, yb, proj, x2, mod3, scw, ssdg, scg, r4, e4, r32, e32, wout, lng, lnb, seq, alpha):
    t, d = x2.shape
    bm = min(256, seq)
    per_batch = seq // bm
    per = bm // HALO_ROWS
    last = t // HALO_ROWS - 1
    main = lambda c: pl.BlockSpec((bm, d), lambda i: (i, c))
    prev = lambda c: pl.BlockSpec((HALO_ROWS, d), lambda i: (jnp.maximum(i * per - 1, 0), c))
    nxt = lambda c: pl.BlockSpec((HALO_ROWS, d), lambda i: (jnp.minimum((i + 1) * per, last), c))
    const = lambda shape: _single(shape, lambda i: (0, 0))
    return pl.pallas_call(
        functools.partial(_post_kernel, bm=bm, seq=seq, alpha=alpha),
        grid=(t // bm,),
        in_specs=[main(0), main(0), main(0), main(1), main(2), prev(2), nxt(2), main(3), prev(3), nxt(3),
                  main(0), pl.BlockSpec((1, 6, d), lambda i: (i // per_batch, 0, 0)),
                  const((3, d)), const((1, d)), const((1, d)),
                  const((d, LANES)), const((LANES, d)), const((d, LANES)), const((LANES, d)),
                  const((2 * d, d)), const((1, d)), const((1, d))],
        out_specs=[pl.BlockSpec((bm, d), lambda i: (i, 0)), pl.BlockSpec((bm, d), lambda i: (i, 0)),
                   pl.BlockSpec((bm, 1), lambda i: (i, 0)), pl.BlockSpec((bm, 1), lambda i: (i, 0))],
        out_shape=[jax.ShapeDtypeStruct((t, d), F32), jax.ShapeDtypeStruct((t, d), BF16),
                   jax.ShapeDtypeStruct((t, 1), F32), jax.ShapeDtypeStruct((t, 1), F32)],
        compiler_params=_params(("parallel",)),
        name="post",
    )(yf, yb, proj, proj, proj, proj, proj, proj, proj, proj, x2, mod3,
      scw, ssdg, scg, r4, e4, r32, e32, wout, lng, lnb)


def _top_values(s, n, want_rank=False):
    vals = []
    rank = jnp.full(s.shape, float(n), F32) if want_rank else None
    for r in range(n):
        m = jnp.max(s, axis=0, keepdims=True)
        vals.append(m)
        hit = s == m
        if want_rank:
            rank = jnp.where(hit, float(r), rank)
        s = jnp.where(hit, -jnp.inf, s)
    return (vals, rank) if want_rank else vals


def _query_kernel(h2_ref, wq_ref, k_ref, r2_ref, e2_ref, n1_ref, c1_ref):
    q = _dot(h2_ref[...], wq_ref[...]).astype(BF16)
    tb = q.shape[0]
    for h in range(PEER_HEADS):
        q1 = q[:, h * D_QUERY:h * D_QUERY + HALF_QUERY]
        q2 = q[:, h * D_QUERY + HALF_QUERY:(h + 1) * D_QUERY]
        s1 = _dot_nt(k_ref[h, 0], q1)
        s2 = _dot_nt(k_ref[h, 1], q2)
        v1 = _top_values(s1, PEER_TOPK)
        v2, r2 = _top_values(s2, PEER_TOPK, want_rank=True)
        pairs = [(a, b) for a in range(PEER_TOPK) for b in range(PEER_TOPK) if (a + 1) * (b + 1) <= PEER_TOPK]
        sums = {ab: v1[ab[0]] + v2[ab[1]] for ab in pairs}
        rows = [sums[ab] for ab in pairs]
        rows += [jnp.full((1, tb), -jnp.inf, F32)] * ((-len(rows)) % SUBLANES)
        top = _top_values(jnp.concatenate(rows, axis=0), PEER_TOPK)
        z = jnp.ones((1, tb), F32)
        for kk in range(1, PEER_TOPK):
            z = z + jnp.exp(top[kk] - top[0])
        tau = top[PEER_TOPK - 1]
        n1 = jnp.zeros(s1.shape, F32)
        for a in range(PEER_TOPK):
            cnt = sum(jnp.where(sums[(a, b)] >= tau, 1.0, 0.0) for b in range(PEER_TOPK) if (a, b) in sums)
            n1 = jnp.where(s1 == v1[a], cnt, n1)
        r2_ref[h] = r2.astype(BF16)
        e2_ref[h] = jnp.exp(s2 - v2[0]).astype(BF16)
        n1_ref[h] = n1
        c1_ref[h] = jnp.exp(s1 - v1[0]) / z


def _query(h2, wq, keys):
    t, d = h2.shape
    tb = min(256, t)
    ospec = pl.BlockSpec((PEER_HEADS, N_KEYS, tb), lambda i: (0, 0, i))
    return pl.pallas_call(
        _query_kernel,
        grid=(t // tb,),
        in_specs=[pl.BlockSpec((tb, d), lambda i: (i, 0)),
                  _single((d, PEER_HEADS * D_QUERY), lambda i: (0, 0)),
                  _single((PEER_HEADS, 2, N_KEYS, HALF_QUERY), lambda i: (0, 0, 0, 0))],
        out_specs=[ospec] * 4,
        out_shape=[jax.ShapeDtypeStruct((PEER_HEADS, N_KEYS, t), dt) for dt in (BF16, BF16, F32, F32)],
        compiler_params=_params(("parallel",)),
        name="peerquery",
    )(h2, wq, keys)


def _gelu(a):
    return 0.5 * a * (1.0 + lax.erf(a * np.float32(math.sqrt(0.5))))


def _pow2_scale(bound):
    safe = jnp.where(bound > 0, bound, FP8_TARGET)
    return jnp.minimum(jnp.exp2(jnp.floor(jnp.log2(FP8_TARGET / safe))), FP8_SCALE_CAP)


def _peer_kernel(h2_ref, u_ref, vt_ref, r2_ref, e2_ref, n1_ref, c1_ref, x1_ref, mod_ref, lng_ref, lnb_ref,
                 iu_ref, iv_ref, sh_ref, ish_ref, sp_ref, isp_ref, o_ref, acc_ref, h8_ref, *, alpha):
    e = pl.program_id(1)

    @pl.when(e == 0)
    def _():
        acc_ref[...] = jnp.zeros_like(acc_ref)
        h8_ref[...] = (h2_ref[...].astype(F32) * sh_ref[...]).astype(FP8)

    chunks = [slice(s, s + PEER_TOKEN_CHUNK) for s in range(0, h2_ref.shape[0], PEER_TOKEN_CHUNK)]
    ats = [_dot_nt(u_ref[...], h8_ref[ls, :]) for ls in chunks]
    for ls, at in zip(chunks, ats):
        a_scale = iu_ref[...] * ish_ref[:, ls]
        p_scale = iv_ref[...] * sp_ref[:, ls]
        parts = []
        for ii in range(u_ref.shape[0] // N_KEYS):
            g = None
            for h in range(PEER_HEADS):
                n_row = n1_ref[h, ii:ii + 1, ls].astype(BF16)
                c_row = c1_ref[h, ii:ii + 1, ls].astype(BF16)
                w = jnp.where(r2_ref[h, :, ls] < n_row, e2_ref[h, :, ls], jnp.zeros((), BF16)) * c_row
                g = w if g is None else g + w
            rows = slice(ii * N_KEYS, (ii + 1) * N_KEYS)
            a = at[rows, :] * a_scale[rows, :]
            parts.append(((_gelu(a) * p_scale[rows, :]).astype(BF16) * g).astype(FP8))
        pt = jnp.concatenate(parts, axis=0)
        acc_ref[:, ls] += _dot(vt_ref[...], pt)

    @pl.when(e == pl.num_programs(1) - 1)
    def _():
        m = mod_ref[0]
        ffn = acc_ref[...].T * isp_ref[...]
        o_ref[...] = _layer_norm(alpha * x1_ref[...] + m[5:6] * ffn, lng_ref[...], lnb_ref[...])


def _peer(h2, u8, vt8, r2, e2, n1, c1, x1, mod3, lng, lnb, iu, iv, sh, ish, sp, isp, seq, alpha):
    t, d = h2.shape
    n_exp = u8.shape[0]
    tb = min(512, seq)
    eb = 1024
    per_batch = seq // tb
    kb = eb // N_KEYS
    full = _single((PEER_HEADS, N_KEYS, tb), lambda i, j: (0, 0, i))
    part = pl.BlockSpec((PEER_HEADS, kb, tb), lambda i, j: (0, j, i))
    ecol = pl.BlockSpec((eb, 1), lambda i, j: (j, 0))
    tcol = _single((tb, 1), lambda i, j: (i, 0))
    trow = _single((1, tb), lambda i, j: (0, i))
    return pl.pallas_call(
        functools.partial(_peer_kernel, alpha=alpha),
        grid=(t // tb, n_exp // eb),
        in_specs=[_single((tb, d), lambda i, j: (i, 0)),
                  pl.BlockSpec((eb, d), lambda i, j: (j, 0)),
                  pl.BlockSpec((d, eb), lambda i, j: (0, j)),
                  full, full, part, part,
                  _single((tb, d), lambda i, j: (i, 0)),
                  pl.BlockSpec((1, 6, d), lambda i, j: (i // per_batch, 0, 0)),
                  pl.BlockSpec((1, d), lambda i, j: (0, 0)),
                  pl.BlockSpec((1, d), lambda i, j: (0, 0)),
                  ecol, ecol, tcol, trow, trow, tcol],
        out_specs=pl.BlockSpec((tb, d), lambda i, j: (i, 0)),
        out_shape=jax.ShapeDtypeStruct((t, d), F32),
        scratch_shapes=[pltpu.VMEM((d, tb), F32), pltpu.VMEM((tb, d), FP8)],
        compiler_params=_params(("parallel", "arbitrary")),
        name="peer",
    )(h2, u8, vt8, r2, e2, n1, c1, x1, mod3, lng, lnb, iu, iv, sh, ish, sp, isp)


def _ffn(h2, h_amax, h_sumsq, x1, mod3, w_query, sub_keys, expert_u, expert_v, ln2_g, ln2_b, seq, alpha):
    t = h2.shape[0]
    r2, e2, n1, c1 = _query(h2, w_query.astype(BF16), sub_keys.astype(BF16))
    su = _pow2_scale(jnp.max(jnp.abs(expert_u), axis=1, keepdims=True))
    sv = _pow2_scale(jnp.max(jnp.abs(expert_v), axis=1, keepdims=True))
    u8 = (expert_u * su).astype(FP8)
    vt8 = (expert_v * sv).T.astype(FP8)
    sh = _pow2_scale(h_amax)
    u_norm = jnp.sqrt(jnp.max(jnp.sum(expert_u * expert_u, axis=1)))
    sp = _pow2_scale(u_norm * jnp.sqrt(h_sumsq) * (PEER_HEADS * jnp.max(1.0 / sv)))
    return _peer(h2, u8, vt8, r2, e2, n1, c1, x1, mod3, _row(ln2_g), _row(ln2_b),
                 1.0 / su, 1.0 / sv, sh, (1.0 / sh).reshape(1, t), sp.reshape(1, t), 1.0 / sp, seq, alpha)


def _one_hot_cols(n_rows, n_cols, group_size, row_offset=0):
    r = np.arange(n_rows)[:, None]
    c = np.arange(n_cols)[None, :]
    return jnp.asarray(r == row_offset + c // group_size, dtype=BF16)


def _row(v):
    return v.reshape(1, -1).astype(F32)


def _pad_lanes(v):
    return jnp.pad(v, (0, LANES - v.shape[0])).reshape(1, LANES).astype(F32)


def kernel(x, c, w_ada, b_ada, w_in, conv_ssd_w, conv_ssd_b, dt_bias_f, dt_bias_b, a_log_f, a_log_b, d_skip, ssd_norm_g, short_conv_w, sc_norm_g, w_out, ln1_g, ln1_b, w_query, sub_keys, expert_u, expert_v, ln2_g, ln2_b):
    batch, seq, d = x.shape
    depth = w_ada.shape[0]
    alpha = (2.0 * depth) ** 0.25
    t = batch * seq
    x2 = x.reshape(t, d)
    c_pad = jnp.pad(c, ((0, SUBLANES - batch % SUBLANES), (0, 0))) if batch % SUBLANES else c

    e_f = _one_hot_cols(LANES, SSD_WIDTH, SSD_HEADDIM, 0)
    e_b = _one_hot_cols(LANES, SSD_WIDTH, SSD_HEADDIM, SSD_HEADS)
    e4 = _one_hot_cols(LANES, SSD_WIDTH, SSD_WIDTH // SSD_GROUPS)
    e32 = _one_hot_cols(LANES, SC_WIDTH, SC_WIDTH // SC_GROUPS)
    r4 = e4.T
    r32 = e32.T

    o1 = SSD_WIDTH
    o2 = o1 + XBC_WIDTH
    o3 = o2 + 2 * SSD_HEADS
    o4 = o3 + SC_WIDTH
    o5 = o4 + SC_WIDTH
    for i in range(depth):
        mod = _ada(c_pad, w_ada[i], b_ada[i].reshape(1, -1))
        mod3 = mod[:batch].reshape(batch, 6, d)

        w = w_in[i]
        w_main = jnp.concatenate([w[:, :o1], w[:, o3:o4], w[:, o4:o5], w[:, o5:], w[:, o1:o2]],
                                 axis=1).astype(BF16)
        w_dt = jnp.pad(w[:, o2:o3], ((0, 0), (0, LANES - 2 * SSD_HEADS)))
        wd_hi, wd_lo = _split2(w_dt)
        proj, dt_raw = _inproj(x2, mod3, w_main, wd_hi, wd_lo, seq)

        xbc = _conv(proj, conv_ssd_w[i], conv_ssd_b[i].reshape(1, -1), seq)
        bias = _pad_lanes(jnp.concatenate([dt_bias_f[i], dt_bias_b[i]]))
        alog = _pad_lanes(jnp.concatenate([a_log_f[i], a_log_b[i]]))
        dskip_x = _row(jnp.repeat(d_skip[i], SSD_HEADDIM))
        y_f, y_b = _ssd(xbc, dt_raw, bias, alog, dskip_x, e_f, e_b, batch, seq)

        x1, h2, h_amax, h_sumsq = _post(y_f, y_b, proj, x2, mod3, short_conv_w[i], _row(ssd_norm_g[i]),
                                        _row(sc_norm_g[i]), r4, e4, r32, e32, w_out[i].astype(BF16),
                                        _row(ln1_g[i]), _row(ln1_b[i]), seq, alpha)

        x2 = _ffn(h2, h_amax, h_sumsq, x1, mod3, w_query[i], sub_keys[i], expert_u[i], expert_v[i],
                  ln2_g[i], ln2_b[i], seq, alpha)
    return x2.reshape(batch, seq, d)
```

```python
import functools
import math

import jax
import jax.numpy as jnp
import numpy as np
from jax import lax
from jax.experimental import pallas as pl
from jax.experimental.pallas import tpu as pltpu

F32 = jnp.float32
BF16 = jnp.bfloat16
FP8 = jnp.float8_e4m3fn
FP8_TARGET = 256.0
FP8_SCALE_CAP = 2.0 ** 60

D_MODEL = 2048
SSD_WIDTH = D_MODEL
SSD_HEADDIM = 64
SSD_HEADS = SSD_WIDTH // SSD_HEADDIM
SSD_GROUPS = 4
SSD_STATE = 128
SSD_CHUNK = 128
HEADS_PER_GROUP = SSD_HEADS // SSD_GROUPS
GROUP_WIDTH = HEADS_PER_GROUP * SSD_HEADDIM
SC_WIDTH = D_MODEL
SC_GROUPS = 32
XBC_WIDTH = SSD_WIDTH + 2 * SSD_GROUPS * SSD_STATE
MAIN_COLS = 4 * D_MODEL + XBC_WIDTH
XBC_COL0 = 4 * D_MODEL
PEER_HEADS = 8
N_KEYS = 128
PEER_TOPK = 16
D_QUERY = 512
HALF_QUERY = D_QUERY // 2
PEER_TOKEN_CHUNK = 256
NORM_EPS = 1e-5
LANES = 128
SUBLANES = 8
HALO_ROWS = 16
VMEM_LIMIT = 56 * 1024 * 1024


def _dot(a, b):
    return jnp.dot(a, b, preferred_element_type=F32)


def _dot_nt(a, b):
    return lax.dot_general(a, b, (((1,), (1,)), ((), ())), preferred_element_type=F32)


def _split2(x):
    hi = x.astype(BF16)
    lo = (x - hi.astype(F32)).astype(BF16)
    return hi, lo


def _split3(x):
    hi = x.astype(BF16)
    r = x - hi.astype(F32)
    mid = r.astype(BF16)
    lo = (r - mid.astype(F32)).astype(BF16)
    return hi, mid, lo


def _dot_lhs_split2(x, m_bf16):
    hi, lo = _split2(x)
    return _dot(hi, m_bf16) + _dot(lo, m_bf16)


def _softplus(x):
    return jnp.maximum(x, 0.0) + jnp.log1p(jnp.exp(-jnp.abs(x)))


def _silu(x):
    return x * (1.0 / (1.0 + jnp.exp(-x)))


def _params(sem):
    return pltpu.CompilerParams(dimension_semantics=sem, vmem_limit_bytes=VMEM_LIMIT)


def _ada_kernel(c_ref, w_ref, b_ref, o_ref):
    sc = _silu(c_ref[...])
    o_ref[...] = _dot(sc.astype(BF16), w_ref[...].astype(BF16)) + b_ref[...]


def _ada(c_pad, w, b):
    rows, d = c_pad.shape
    n = w.shape[1]
    bn = 1024
    return pl.pallas_call(
        _ada_kernel,
        grid=(n // bn,),
        in_specs=[pl.BlockSpec((rows, d), lambda j: (0, 0)),
                  pl.BlockSpec((d, bn), lambda j: (0, j)),
                  pl.BlockSpec((1, bn), lambda j: (0, j))],
        out_specs=pl.BlockSpec((rows, bn), lambda j: (0, j)),
        out_shape=jax.ShapeDtypeStruct((rows, n), F32),
        compiler_params=_params(("arbitrary",)),
        name="ada",
    )(c_pad, w, b)


def _inproj_kernel(x_ref, mod_ref, w_ref, wdh_ref, wdl_ref, o_ref, dt_ref, h_scr):
    @pl.when(pl.program_id(1) == 0)
    def _():
        m = mod_ref[0]
        h = x_ref[...] * (1.0 + m[1:2]) + m[0:1]
        hi, lo = _split2(h)
        h_scr[...] = hi
        dt_ref[...] = _dot(hi, wdh_ref[...]) + _dot(lo, wdh_ref[...]) + _dot(hi, wdl_ref[...])

    o_ref[...] = _dot(h_scr[...], w_ref[...]).astype(o_ref.dtype)


def _inproj(x2, mod3, w_main, wd_hi, wd_lo, seq):
    t, d = x2.shape
    n = w_main.shape[1]
    bm = min(1024, seq)
    bn = 1024
    per_batch = seq // bm
    return pl.pallas_call(
        _inproj_kernel,
        grid=(t // bm, n // bn),
        in_specs=[pl.BlockSpec((bm, d), lambda i, j: (i, 0)),
                  pl.BlockSpec((1, 6, d), lambda i, j: (i // per_batch, 0, 0)),
                  pl.BlockSpec((d, bn), lambda i, j: (0, j)),
                  pl.BlockSpec((d, LANES), lambda i, j: (0, 0)),
                  pl.BlockSpec((d, LANES), lambda i, j: (0, 0))],
        out_specs=[pl.BlockSpec((bm, bn), lambda i, j: (i, j)),
                   pl.BlockSpec((bm, LANES), lambda i, j: (i, 0))],
        out_shape=[jax.ShapeDtypeStruct((t, n), BF16),
                   jax.ShapeDtypeStruct((t, LANES), F32)],
        scratch_shapes=[pltpu.VMEM((bm, d), BF16)],
        compiler_params=_params(("parallel", "arbitrary")),
        name="inproj",
    )(x2, mod3, w_main, wd_hi, wd_lo)


def _shifted_rows(u, prev_row, next_row):
    rows = u.shape[0]
    ridx = lax.broadcasted_iota(jnp.int32, u.shape, 0)
    up = jnp.where(ridx == 0, prev_row, pltpu.roll(u, 1, axis=0))
    un = jnp.where(ridx == rows - 1, next_row, pltpu.roll(u, rows - 1, axis=0))
    return up, un


def _halo_specs(bm, bw, col_of, t):
    per = bm // HALO_ROWS
    last = t // HALO_ROWS - 1
    prev = pl.BlockSpec((HALO_ROWS, bw), lambda i, j: (jnp.maximum(i * per - 1, 0), col_of(j)))
    nxt = pl.BlockSpec((HALO_ROWS, bw), lambda i, j: (jnp.minimum((i + 1) * per, last), col_of(j)))
    return prev, nxt


def _conv_kernel(u_ref, p_ref, n_ref, w_ref, b_ref, o_ref, *, bm, seq):
    i = pl.program_id(0)
    u = u_ref[...].astype(F32)
    has_prev = ((i * bm) % seq != 0).astype(F32)
    has_next = (((i + 1) * bm) % seq != 0).astype(F32)
    up, un = _shifted_rows(u, p_ref[HALO_ROWS - 1:HALO_ROWS, :].astype(F32) * has_prev,
                           n_ref[0:1, :].astype(F32) * has_next)
    w = w_ref[...]
    o_ref[...] = _silu(up * w[0:1] + u * w[1:2] + un * w[2:3] + b_ref[...])


def _conv(proj, conv_w, conv_b, seq):
    t = proj.shape[0]
    bm = min(512, seq)
    bw = 1024
    col0 = XBC_COL0 // bw
    col_of = lambda j: col0 + j
    prev, nxt = _halo_specs(bm, bw, col_of, t)
    return pl.pallas_call(
        functools.partial(_conv_kernel, bm=bm, seq=seq),
        grid=(t // bm, XBC_WIDTH // bw),
        in_specs=[pl.BlockSpec((bm, bw), lambda i, j: (i, col_of(j))), prev, nxt,
                  pl.BlockSpec((3, bw), lambda i, j: (0, j)),
                  pl.BlockSpec((1, bw), lambda i, j: (0, j))],
        out_specs=pl.BlockSpec((bm, bw), lambda i, j: (i, j)),
        out_shape=jax.ShapeDtypeStruct((t, XBC_WIDTH), F32),
        compiler_params=_params(("parallel", "parallel")),
        name="ssdconv",
    )(proj, proj, proj, conv_w, conv_b)


def _ssd_direction(x_ref, b_ref, c_ref, dt_ref, e_ref, h_scr, y_ref, bias, a, dskip, backward):
    cs_len = SSD_CHUNK
    row = lax.broadcasted_iota(jnp.int32, (cs_len, cs_len), 0)
    col = lax.broadcasted_iota(jnp.int32, (cs_len, cs_len), 1)
    tri = (col <= row).astype(BF16)
    dt = _softplus(dt_ref[...] + bias)
    adt = dt * a
    h3 = _split3(adt)
    cs = _dot(tri, h3[0]) + _dot(tri, h3[1]) + _dot(tri, h3[2])
    tot = cs[cs_len - 1:cs_len, :]
    if backward:
        ecs = cs - adt
        p = -ecs
        wst = dt * jnp.exp(ecs)
        indec = jnp.exp(tot - ecs)
        mask = col >= row
    else:
        p = cs
        wst = dt * jnp.exp(tot - cs)
        indec = jnp.exp(cs)
        mask = col <= row
    pt = p.T
    e = e_ref[...]
    dtx = _dot(dt.astype(BF16), e)
    wstx = _dot(wst.astype(BF16), e)
    indx = _dot(indec.astype(BF16), e)
    xs = x_ref[...]
    xd = (xs * dtx).astype(BF16)
    xw = (xs * wstx).astype(BF16)
    col_off = SSD_HEADS if backward else 0
    pair = 2 * SSD_HEADDIM
    first_of_pair = lax.broadcasted_iota(jnp.int32, (cs_len, pair), 1) < SSD_HEADDIM
    zero = jnp.zeros((), BF16)
    for g in range(SSD_GROUPS):
        gs = slice(g * GROUP_WIDTH, (g + 1) * GROUP_WIDTH)
        bg = b_ref[:, g * SSD_STATE:(g + 1) * SSD_STATE]
        cg = c_ref[:, g * SSD_STATE:(g + 1) * SSD_STATE].astype(BF16)
        cb = _dot_nt(cg, bg.astype(BF16))
        h_in = h_scr[g]
        y_off = _dot(cg, h_in.astype(BF16)) * indx[:, gs]
        st = _dot(bg.T.astype(BF16), xw[:, gs])
        for r in range(0, HEADS_PER_GROUP, 2):
            hd = g * HEADS_PER_GROUP + r
            ms = []
            for ci in (hd + col_off, hd + col_off + 1):
                seg = p[:, ci:ci + 1] - pt[ci:ci + 1, :]
                lmat = jnp.exp(jnp.where(mask, seg, -jnp.inf))
                ms.append((cb * lmat).astype(BF16))
            hs = slice(hd * SSD_HEADDIM, (hd + 2) * SSD_HEADDIM)
            x2h = xd[:, hs]
            rhs = jnp.concatenate([jnp.where(first_of_pair, x2h, zero), jnp.where(first_of_pair, zero, x2h)], axis=0)
            y = _dot(jnp.concatenate(ms, axis=1), rhs) + y_off[:, r * SSD_HEADDIM:(r + 2) * SSD_HEADDIM]
            if dskip is not None:
                y = y + xs[:, hs] * dskip[:, hs]
            y_ref[:, hs] = y
        edge = 0 if backward else cs_len - 1
        h_scr[g] = indx[edge:edge + 1, gs] * h_in + st


def _ssd_kernel(xf_ref, bf_ref, cf_ref, dtf_ref, xb_ref, bb_ref, cb_ref, dtb_ref,
                bias_ref, alog_ref, dskip_ref, ef_ref, eb_ref, yf_ref, yb_ref, hf_scr, hb_scr):
    @pl.when(pl.program_id(1) == 0)
    def _():
        hf_scr[...] = jnp.zeros_like(hf_scr)
        hb_scr[...] = jnp.zeros_like(hb_scr)

    bias = bias_ref[...]
    a = -jnp.exp(alog_ref[...])
    _ssd_direction(xf_ref, bf_ref, cf_ref, dtf_ref, ef_ref, hf_scr, yf_ref, bias, a, dskip_ref[...], False)
    _ssd_direction(xb_ref, bb_ref, cb_ref, dtb_ref, eb_ref, hb_scr, yb_ref, bias, a, None, True)


def _ssd(xbc, dt_raw, bias, alog, dskip_x, e_f, e_b, batch, seq):
    t = xbc.shape[0]
    nc = seq // SSD_CHUNK
    cl = SSD_CHUNK
    gn = SSD_GROUPS * SSD_STATE
    bcol = SSD_WIDTH // gn
    fwd = lambda b, k: b * nc + k
    bwd = lambda b, k: b * nc + (nc - 1 - k)

    def specs(ch):
        return [pl.BlockSpec((cl, SSD_WIDTH), lambda b, k: (ch(b, k), 0)),
                pl.BlockSpec((cl, gn), lambda b, k: (ch(b, k), bcol)),
                pl.BlockSpec((cl, gn), lambda b, k: (ch(b, k), bcol + 1)),
                pl.BlockSpec((cl, LANES), lambda b, k: (ch(b, k), 0))]

    const = lambda shape: pl.BlockSpec(shape, lambda b, k: (0, 0))
    state = pltpu.VMEM((SSD_GROUPS, SSD_STATE, GROUP_WIDTH), F32)
    return pl.pallas_call(
        _ssd_kernel,
        grid=(batch, nc),
        in_specs=specs(fwd) + specs(bwd) + [const((1, LANES)), const((1, LANES)), const((1, SSD_WIDTH)),
                                            const((LANES, SSD_WIDTH)), const((LANES, SSD_WIDTH))],
        out_specs=[pl.BlockSpec((cl, SSD_WIDTH), lambda b, k: (fwd(b, k), 0)),
                   pl.BlockSpec((cl, SSD_WIDTH), lambda b, k: (bwd(b, k), 0))],
        out_shape=[jax.ShapeDtypeStruct((t, SSD_WIDTH), F32)] * 2,
        scratch_shapes=[state, state],
        compiler_params=_params(("parallel", "arbitrary")),
        name="ssd",
    )(xbc, xbc, xbc, dt_raw, xbc, xbc, xbc, dt_raw, bias, alog, dskip_x, e_f, e_b)


def _group_rms(y, r_ref, e_ref, group_size):
    gsum = _dot_lhs_split2(y * y, r_ref[...])
    inv = lax.rsqrt(gsum * (1.0 / group_size) + NORM_EPS)
    return y * _dot(inv.astype(BF16), e_ref[...])


def _layer_norm(v, g, b):
    mu = jnp.mean(v, axis=-1, keepdims=True)
    vc = v - mu
    var = jnp.mean(vc * vc, axis=-1, keepdims=True)
    return vc * lax.rsqrt(var + NORM_EPS) * g + b


def _post_kernel(yf_ref, yb_ref, z_ref, gb_ref, gc_ref, gcp_ref, gcn_ref, v_ref, vp_ref, vn_ref,
                 x_ref, mod_ref, scw_ref, ssdg_ref, scg_ref, r4_ref, e4_ref, r32_ref, e32_ref,
                 wout_ref, lng_ref, lnb_ref, x1_ref, h2_ref, amax_ref, sumsq_ref, *, bm, seq, alpha):
    i = pl.program_id(0)
    m = mod_ref[0]
    y = (yf_ref[...] + yb_ref[...]) * _silu(z_ref[...].astype(F32))
    y_ssd = _group_rms(y, r4_ref, e4_ref, SSD_WIDTH // SSD_GROUPS) * ssdg_ref[...]

    f32 = lambda ref, rows=slice(None): ref[rows, :].astype(F32)
    u = f32(gc_ref) * f32(v_ref)
    has_prev = ((i * bm) % seq != 0).astype(F32)
    has_next = (((i + 1) * bm) % seq != 0).astype(F32)
    last = slice(HALO_ROWS - 1, HALO_ROWS)
    first = slice(0, 1)
    up, un = _shifted_rows(u, f32(gcp_ref, last) * f32(vp_ref, last) * has_prev,
                           f32(gcn_ref, first) * f32(vn_ref, first) * has_next)
    w = scw_ref[...]
    y_sc = f32(gb_ref) * (up * w[0:1] + u * w[1:2] + un * w[2:3])
    y_sc = _group_rms(y_sc, r32_ref, e32_ref, SC_WIDTH // SC_GROUPS) * scg_ref[...]

    mix = (_dot(y_ssd.astype(BF16), wout_ref[0:SSD_WIDTH, :])
           + _dot(y_sc.astype(BF16), wout_ref[SSD_WIDTH:SSD_WIDTH + SC_WIDTH, :]))
    x1 = _layer_norm(alpha * x_ref[...] + m[2:3] * mix, lng_ref[...], lnb_ref[...])
    x1_ref[...] = x1
    h2 = x1 * (1.0 + m[4:5]) + m[3:4]
    h2_ref[...] = h2.astype(BF16)
    amax_ref[...] = jnp.max(jnp.abs(h2), axis=-1, keepdims=True)
    sumsq_ref[...] = jnp.sum(h2 * h2, axis=-1, keepdims=True)


def _single(shape, index_map):
    return pl.BlockSpec(shape, index_map, pipeline_mode=pl.Buffered(1))


def _post(yf, yb, proj, x2, mod3, scw, ssdg, scg, r4, e4, r32, e32, wout, lng, lnb, seq, alpha):
    t, d = x2.shape
    bm = min(256, seq)
    per_batch = seq // bm
    per = bm // HALO_ROWS
    last = t // HALO_ROWS - 1
    main = lambda c: pl.BlockSpec((bm, d), lambda i: (i, c))
    prev = lambda c: pl.BlockSpec((HALO_ROWS, d), lambda i: (jnp.maximum(i * per - 1, 0), c))
    nxt = lambda c: pl.BlockSpec((HALO_ROWS, d), lambda i: (jnp.minimum((i + 1) * per, last), c))
    const = lambda shape: _single(shape, lambda i: (0, 0))
    return pl.pallas_call(
        functools.partial(_post_kernel, bm=bm, seq=seq, alpha=alpha),
        grid=(t // bm,),
        in_specs=[main(0), main(0), main(0), main(1), main(2), prev(2), nxt(2), main(3), prev(3), nxt(3),
                  main(0), pl.BlockSpec((1, 6, d), lambda i: (i // per_batch, 0, 0)),
                  const((3, d)), const((1, d)), const((1, d)),
                  const((d, LANES)), const((LANES, d)), const((d, LANES)), const((LANES, d)),
                  const((2 * d, d)), const((1, d)), const((1, d))],
        out_specs=[pl.BlockSpec((bm, d), lambda i: (i, 0)), pl.BlockSpec((bm, d), lambda i: (i, 0)),
                   pl.BlockSpec((bm, 1), lambda i: (i, 0)), pl.BlockSpec((bm, 1), lambda i: (i, 0))],
        out_shape=[jax.ShapeDtypeStruct((t, d), F32), jax.ShapeDtypeStruct((t, d), BF16),
                   jax.ShapeDtypeStruct((t, 1), F32), jax.ShapeDtypeStruct((t, 1), F32)],
        compiler_params=_params(("parallel",)),
        name="post",
    )(yf, yb, proj, proj, proj, proj, proj, proj, proj, proj, x2, mod3,
      scw, ssdg, scg, r4, e4, r32, e32, wout, lng, lnb)


def _top_values(s, n, want_rank=False):
    vals = []
    rank = jnp.full(s.shape, float(n), F32) if want_rank else None
    for r in range(n):
        m = jnp.max(s, axis=0, keepdims=True)
        vals.append(m)
        hit = s == m
        if want_rank:
            rank = jnp.where(hit, float(r), rank)
        s = jnp.where(hit, -jnp.inf, s)
    return (vals, rank) if want_rank else vals


def _query_kernel(h2_ref, wq_ref, k_ref, r2_ref, e2_ref, n1_ref, c1_ref):
    q = _dot(h2_ref[...], wq_ref[...]).astype(BF16)
    tb = q.shape[0]
    for h in range(PEER_HEADS):
        q1 = q[:, h * D_QUERY:h * D_QUERY + HALF_QUERY]
        q2 = q[:, h * D_QUERY + HALF_QUERY:(h + 1) * D_QUERY]
        s1 = _dot_nt(k_ref[h, 0], q1)
        s2 = _dot_nt(k_ref[h, 1], q2)
        v1 = _top_values(s1, PEER_TOPK)
        v2, r2 = _top_values(s2, PEER_TOPK, want_rank=True)
        pairs = [(a, b) for a in range(PEER_TOPK) for b in range(PEER_TOPK) if (a + 1) * (b + 1) <= PEER_TOPK]
        sums = {ab: v1[ab[0]] + v2[ab[1]] for ab in pairs}
        rows = [sums[ab] for ab in pairs]
        rows += [jnp.full((1, tb), -jnp.inf, F32)] * ((-len(rows)) % SUBLANES)
        top = _top_values(jnp.concatenate(rows, axis=0), PEER_TOPK)
        z = jnp.ones((1, tb), F32)
        for kk in range(1, PEER_TOPK):
            z = z + jnp.exp(top[kk] - top[0])
        tau = top[PEER_TOPK - 1]
        n1 = jnp.zeros(s1.shape, F32)
        for a in range(PEER_TOPK):
            cnt = sum(jnp.where(sums[(a, b)] >= tau, 1.0, 0.0) for b in range(PEER_TOPK) if (a, b) in sums)
            n1 = jnp.where(s1 == v1[a], cnt, n1)
        r2_ref[h] = r2.astype(BF16)
        e2_ref[h] = jnp.exp(s2 - v2[0]).astype(BF16)
        n1_ref[h] = n1
        c1_ref[h] = jnp.exp(s1 - v1[0]) / z


def _query(h2, wq, keys):
    t, d = h2.shape
    tb = min(256, t)
    ospec = pl.BlockSpec((PEER_HEADS, N_KEYS, tb), lambda i: (0, 0, i))
    return pl.pallas_call(
        _query_kernel,
        grid=(t // tb,),
        in_specs=[pl.BlockSpec((tb, d), lambda i: (i, 0)),
                  _single((d, PEER_HEADS * D_QUERY), lambda i: (0, 0)),
                  _single((PEER_HEADS, 2, N_KEYS, HALF_QUERY), lambda i: (0, 0, 0, 0))],
        out_specs=[ospec] * 4,
        out_shape=[jax.ShapeDtypeStruct((PEER_HEADS, N_KEYS, t), dt) for dt in (BF16, BF16, F32, F32)],
        compiler_params=_params(("parallel",)),
        name="peerquery",
    )(h2, wq, keys)


def _pow2_scale(bound):
    ratio = FP8_TARGET / jnp.where(bound > 0, bound, FP8_TARGET)
    exponent_only = lax.bitcast_convert_type(ratio, jnp.int32) & jnp.int32(0x7F800000)
    return jnp.minimum(lax.bitcast_convert_type(exponent_only, F32), FP8_SCALE_CAP)


def _expert_prep_kernel(w_ref, q_ref, inv_ref, sumsq_ref, *, transpose):
    w = w_ref[...]
    s = _pow2_scale(jnp.max(jnp.abs(w), axis=1, keepdims=True))
    inv_ref[...] = 1.0 / s
    sumsq_ref[...] = jnp.sum(w * w, axis=1, keepdims=True)
    ws = w * s
    q_ref[...] = (ws.T if transpose else ws).astype(FP8)


def _expert_prep(table, transpose):
    n_exp, d = table.shape
    rows = 512
    col = pl.BlockSpec((rows, 1), lambda i: (i, 0))
    q_spec = pl.BlockSpec((d, rows), lambda i: (0, i)) if transpose else pl.BlockSpec((rows, d), lambda i: (i, 0))
    q_shape = (d, n_exp) if transpose else (n_exp, d)
    return pl.pallas_call(
        functools.partial(_expert_prep_kernel, transpose=transpose),
        grid=(n_exp // rows,),
        in_specs=[pl.BlockSpec((rows, d), lambda i: (i, 0))],
        out_specs=[q_spec, col, col],
        out_shape=[jax.ShapeDtypeStruct(q_shape, FP8), jax.ShapeDtypeStruct((n_exp, 1), F32),
                   jax.ShapeDtypeStruct((n_exp, 1), F32)],
        compiler_params=_params(("parallel",)),
        name="expertprep_t" if transpose else "expertprep",
    )(table)


def _peer_kernel(h2_ref, u_ref, vt_ref, r2_ref, e2_ref, n1_ref, c1_ref, x1_ref, mod_ref, lng_ref, lnb_ref,
                 iu_ref, iv_ref, sh_ref, ish_ref, sp_ref, isp_ref, o_ref, acc_ref, h8_ref, *, alpha):
    e = pl.program_id(1)

    @pl.when(e == 0)
    def _():
        acc_ref[...] = jnp.zeros_like(acc_ref)
        h8_ref[...] = (h2_ref[...].astype(F32) * sh_ref[...]).astype(FP8)

    chunks = [slice(s, s + PEER_TOKEN_CHUNK) for s in range(0, h2_ref.shape[0], PEER_TOKEN_CHUNK)]
    ats = [_dot_nt(u_ref[...], h8_ref[ls, :]) for ls in chunks]
    for ls, at in zip(chunks, ats):
        a_scale = iu_ref[...] * ish_ref[:, ls]
        p_scale = iv_ref[...] * sp_ref[:, ls]
        parts = []
        for ii in range(u_ref.shape[0] // N_KEYS):
            g = None
            for h in range(PEER_HEADS):
                n_row = n1_ref[h, ii:ii + 1, ls].astype(BF16)
                c_row = c1_ref[h, ii:ii + 1, ls].astype(BF16)
                w = jnp.where(r2_ref[h, :, ls] < n_row, e2_ref[h, :, ls], jnp.zeros((), BF16)) * c_row
                g = w if g is None else g + w
            rows = slice(ii * N_KEYS, (ii + 1) * N_KEYS)
            b = at[rows, :] * a_scale[rows, :]
            parts.append(((b * (1.0 + lax.erf(b)) * p_scale[rows, :]).astype(BF16) * g).astype(FP8))
        pt = jnp.concatenate(parts, axis=0)
        acc_ref[:, ls] += _dot(vt_ref[...], pt)

    @pl.when(e == pl.num_programs(1) - 1)
    def _():
        m = mod_ref[0]
        ffn = acc_ref[...].T * isp_ref[...]
        o_ref[...] = _layer_norm(alpha * x1_ref[...] + m[5:6] * ffn, lng_ref[...], lnb_ref[...])


def _peer(h2, u8, vt8, r2, e2, n1, c1, x1, mod3, lng, lnb, iu, iv, sh, ish, sp, isp, seq, alpha):
    t, d = h2.shape
    n_exp = u8.shape[0]
    tb = min(512, seq)
    eb = 1024
    per_batch = seq // tb
    kb = eb // N_KEYS
    full = _single((PEER_HEADS, N_KEYS, tb), lambda i, j: (0, 0, i))
    part = pl.BlockSpec((PEER_HEADS, kb, tb), lambda i, j: (0, j, i))
    ecol = pl.BlockSpec((eb, 1), lambda i, j: (j, 0))
    tcol = _single((tb, 1), lambda i, j: (i, 0))
    trow = _single((1, tb), lambda i, j: (0, i))
    return pl.pallas_call(
        functools.partial(_peer_kernel, alpha=alpha),
        grid=(t // tb, n_exp // eb),
        in_specs=[_single((tb, d), lambda i, j: (i, 0)),
                  pl.BlockSpec((eb, d), lambda i, j: (j, 0)),
                  pl.BlockSpec((d, eb), lambda i, j: (0, j)),
                  full, full, part, part,
                  _single((tb, d), lambda i, j: (i, 0)),
                  pl.BlockSpec((1, 6, d), lambda i, j: (i // per_batch, 0, 0)),
                  pl.BlockSpec((1, d), lambda i, j: (0, 0)),
                  pl.BlockSpec((1, d), lambda i, j: (0, 0)),
                  ecol, ecol, tcol, trow, trow, tcol],
        out_specs=pl.BlockSpec((tb, d), lambda i, j: (i, 0)),
        out_shape=jax.ShapeDtypeStruct((t, d), F32),
        scratch_shapes=[pltpu.VMEM((d, tb), F32), pltpu.VMEM((tb, d), FP8)],
        compiler_params=_params(("parallel", "arbitrary")),
        name="peer",
    )(h2, u8, vt8, r2, e2, n1, c1, x1, mod3, lng, lnb, iu, iv, sh, ish, sp, isp)


def _ffn(h2, h_amax, h_sumsq, x1, mod3, w_query, sub_keys, expert_u, expert_v, ln2_g, ln2_b, seq, alpha):
    t = h2.shape[0]
    r2, e2, n1, c1 = _query(h2, w_query.astype(BF16), sub_keys.astype(BF16))
    u8, inv_su, u_sumsq = _expert_prep(expert_u, transpose=False)
    vt8, inv_sv, _ = _expert_prep(expert_v, transpose=True)
    sh = _pow2_scale(h_amax)
    sp = _pow2_scale(jnp.sqrt(jnp.max(u_sumsq) * h_sumsq) * (PEER_HEADS * jnp.max(inv_sv)))
    sqrt_half = np.float32(math.sqrt(0.5))
    return _peer(h2, u8, vt8, r2, e2, n1, c1, x1, mod3, _row(ln2_g), _row(ln2_b), inv_su * sqrt_half,
                 inv_sv * sqrt_half, sh, (1.0 / sh).reshape(1, t), sp.reshape(1, t), 1.0 / sp, seq, alpha)


def _one_hot_cols(n_rows, n_cols, group_size, row_offset=0):
    r = np.arange(n_rows)[:, None]
    c = np.arange(n_cols)[None, :]
    return jnp.asarray(r == row_offset + c // group_size, dtype=BF16)


def _row(v):
    return v.reshape(1, -1).astype(F32)


def _pad_lanes(v):
    return jnp.pad(v, (0, LANES - v.shape[0])).reshape(1, LANES).astype(F32)


def kernel(x, c, w_ada, b_ada, w_in, conv_ssd_w, conv_ssd_b, dt_bias_f, dt_bias_b, a_log_f, a_log_b, d_skip, ssd_norm_g, short_conv_w, sc_norm_g, w_out, ln1_g, ln1_b, w_query, sub_keys, expert_u, expert_v, ln2_g, ln2_b):
    batch, seq, d = x.shape
    depth = w_ada.shape[0]
    alpha = (2.0 * depth) ** 0.25
    t = batch * seq
    x2 = x.reshape(t, d)
    c_pad = jnp.pad(c, ((0, SUBLANES - batch % SUBLANES), (0, 0))) if batch % SUBLANES else c

    e_f = _one_hot_cols(LANES, SSD_WIDTH, SSD_HEADDIM, 0)
    e_b = _one_hot_cols(LANES, SSD_WIDTH, SSD_HEADDIM, SSD_HEADS)
    e4 = _one_hot_cols(LANES, SSD_WIDTH, SSD_WIDTH // SSD_GROUPS)
    e32 = _one_hot_cols(LANES, SC_WIDTH, SC_WIDTH // SC_GROUPS)
    r4 = e4.T
    r32 = e32.T

    o1 = SSD_WIDTH
    o2 = o1 + XBC_WIDTH
    o3 = o2 + 2 * SSD_HEADS
    o4 = o3 + SC_WIDTH
    o5 = o4 + SC_WIDTH
    for i in range(depth):
        mod = _ada(c_pad, w_ada[i], b_ada[i].reshape(1, -1))
        mod3 = mod[:batch].reshape(batch, 6, d)

        w = w_in[i]
        w16 = w.astype(BF16)
        w_main = jnp.concatenate([w16[:, :o1], w16[:, o3:o4], w16[:, o4:o5], w16[:, o5:], w16[:, o1:o2]], axis=1)
        w_dt = jnp.pad(w[:, o2:o3], ((0, 0), (0, LANES - 2 * SSD_HEADS)))
        wd_hi, wd_lo = _split2(w_dt)
        proj, dt_raw = _inproj(x2, mod3, w_main, wd_hi, wd_lo, seq)

        xbc = _conv(proj, conv_ssd_w[i], conv_ssd_b[i].reshape(1, -1), seq)
        bias = _pad_lanes(jnp.concatenate([dt_bias_f[i], dt_bias_b[i]]))
        alog = _pad_lanes(jnp.concatenate([a_log_f[i], a_log_b[i]]))
        dskip_x = _row(jnp.repeat(d_skip[i], SSD_HEADDIM))
        y_f, y_b = _ssd(xbc, dt_raw, bias, alog, dskip_x, e_f, e_b, batch, seq)

        x1, h2, h_amax, h_sumsq = _post(y_f, y_b, proj, x2, mod3, short_conv_w[i], _row(ssd_norm_g[i]),
                                        _row(sc_norm_g[i]), r4, e4, r32, e32, w_out[i].astype(BF16),
                                        _row(ln1_g[i]), _row(ln1_b[i]), seq, alpha)

        x2 = _ffn(h2, h_amax, h_sumsq, x1, mod3, w_query[i], sub_keys[i], expert_u[i], expert_v[i],
                  ln2_g[i], ln2_b[i], seq, alpha)
    return x2.reshape(batch, seq, d)
```

```python
import functools
import math

import jax
import jax.numpy as jnp
import numpy as np
from jax import lax
from jax.experimental import pallas as pl
from jax.experimental.pallas import tpu as pltpu

F32 = jnp.float32
BF16 = jnp.bfloat16
FP8 = jnp.float8_e4m3fn
FP8_TARGET = 256.0
FP8_SCALE_CAP = 2.0 ** 60

D_MODEL = 2048
SSD_WIDTH = D_MODEL
SSD_HEADDIM = 64
SSD_HEADS = SSD_WIDTH // SSD_HEADDIM
SSD_GROUPS = 4
SSD_STATE = 128
SSD_CHUNK = 128
HEADS_PER_GROUP = SSD_HEADS // SSD_GROUPS
GROUP_WIDTH = HEADS_PER_GROUP * SSD_HEADDIM
SC_WIDTH = D_MODEL
SC_GROUPS = 32
XBC_WIDTH = SSD_WIDTH + 2 * SSD_GROUPS * SSD_STATE
MAIN_COLS = 4 * D_MODEL + XBC_WIDTH
XBC_COL0 = 4 * D_MODEL
PEER_HEADS = 8
N_KEYS = 128
PEER_TOPK = 16
D_QUERY = 512
HALF_QUERY = D_QUERY // 2
PEER_TOKEN_CHUNK = 256
NORM_EPS = 1e-5
LANES = 128
SUBLANES = 8
HALO_ROWS = 16
VMEM_LIMIT = 56 * 1024 * 1024


def _dot(a, b):
    return jnp.dot(a, b, preferred_element_type=F32)


def _dot_nt(a, b):
    return lax.dot_general(a, b, (((1,), (1,)), ((), ())), preferred_element_type=F32)


def _split2(x):
    hi = x.astype(BF16)
    lo = (x - hi.astype(F32)).astype(BF16)
    return hi, lo


def _split3(x):
    hi = x.astype(BF16)
    r = x - hi.astype(F32)
    mid = r.astype(BF16)
    lo = (r - mid.astype(F32)).astype(BF16)
    return hi, mid, lo


def _softplus(x):
    return jnp.maximum(x, 0.0) + jnp.log1p(jnp.exp(-jnp.abs(x)))


def _silu(x):
    return x * (1.0 / (1.0 + jnp.exp(-x)))


def _params(sem):
    return pltpu.CompilerParams(dimension_semantics=sem, vmem_limit_bytes=VMEM_LIMIT)


def _ada_kernel(c_ref, w_ref, b_ref, o_ref):
    sc = _silu(c_ref[...])
    o_ref[...] = _dot(sc.astype(BF16), w_ref[...].astype(BF16)) + b_ref[...]


def _ada(c_pad, w, b):
    rows, d = c_pad.shape
    n = w.shape[1]
    bn = 1024
    return pl.pallas_call(
        _ada_kernel,
        grid=(n // bn,),
        in_specs=[pl.BlockSpec((rows, d), lambda j: (0, 0)),
                  pl.BlockSpec((d, bn), lambda j: (0, j)),
                  pl.BlockSpec((1, bn), lambda j: (0, j))],
        out_specs=pl.BlockSpec((rows, bn), lambda j: (0, j)),
        out_shape=jax.ShapeDtypeStruct((rows, n), F32),
        compiler_params=_params(("arbitrary",)),
        name="ada",
    )(c_pad, w, b)


DT_COL0 = SSD_WIDTH + XBC_WIDTH
GATES_COL0 = DT_COL0 + 2 * SSD_HEADS


def _win_prep_kernel(w_ref, main_ref, dh_ref, dl_ref):
    w = w_ref[...]
    main_ref[:, 0:SSD_WIDTH] = w[:, 0:SSD_WIDTH].astype(BF16)
    main_ref[:, SSD_WIDTH:XBC_COL0] = w[:, GATES_COL0:GATES_COL0 + 3 * SC_WIDTH].astype(BF16)
    main_ref[:, XBC_COL0:MAIN_COLS] = w[:, SSD_WIDTH:DT_COL0].astype(BF16)
    dt = w[:, DT_COL0:DT_COL0 + LANES]
    lane = lax.broadcasted_iota(jnp.int32, dt.shape, 1)
    hi, lo = _split2(jnp.where(lane < 2 * SSD_HEADS, dt, 0.0))
    dh_ref[...] = hi
    dl_ref[...] = lo


def _win_prep(w):
    d, n = w.shape
    rows = 128
    return pl.pallas_call(
        _win_prep_kernel,
        grid=(d // rows,),
        in_specs=[pl.BlockSpec((rows, n), lambda i: (i, 0))],
        out_specs=[pl.BlockSpec((rows, MAIN_COLS), lambda i: (i, 0)),
                   pl.BlockSpec((rows, LANES), lambda i: (i, 0)),
                   pl.BlockSpec((rows, LANES), lambda i: (i, 0))],
        out_shape=[jax.ShapeDtypeStruct((d, MAIN_COLS), BF16), jax.ShapeDtypeStruct((d, LANES), BF16),
                   jax.ShapeDtypeStruct((d, LANES), BF16)],
        compiler_params=_params(("parallel",)),
        name="winprep",
    )(w)


def _inproj_kernel(x_ref, mod_ref, w_ref, wdh_ref, wdl_ref, o_ref, dt_ref, h_scr):
    @pl.when(pl.program_id(1) == 0)
    def _():
        m = mod_ref[0]
        h = x_ref[...] * (1.0 + m[1:2]) + m[0:1]
        hi, lo = _split2(h)
        h_scr[...] = hi
        dt_ref[...] = _dot(hi, wdh_ref[...]) + _dot(lo, wdh_ref[...]) + _dot(hi, wdl_ref[...])

    o_ref[...] = _dot(h_scr[...], w_ref[...]).astype(o_ref.dtype)


def _inproj(x2, mod3, w_main, wd_hi, wd_lo, seq):
    t, d = x2.shape
    n = w_main.shape[1]
    bm = min(1024, seq)
    bn = 1024
    per_batch = seq // bm
    return pl.pallas_call(
        _inproj_kernel,
        grid=(t // bm, n // bn),
        in_specs=[pl.BlockSpec((bm, d), lambda i, j: (i, 0)),
                  pl.BlockSpec((1, 6, d), lambda i, j: (i // per_batch, 0, 0)),
                  pl.BlockSpec((d, bn), lambda i, j: (0, j)),
                  pl.BlockSpec((d, LANES), lambda i, j: (0, 0)),
                  pl.BlockSpec((d, LANES), lambda i, j: (0, 0))],
        out_specs=[pl.BlockSpec((bm, bn), lambda i, j: (i, j)),
                   pl.BlockSpec((bm, LANES), lambda i, j: (i, 0))],
        out_shape=[jax.ShapeDtypeStruct((t, n), BF16),
                   jax.ShapeDtypeStruct((t, LANES), F32)],
        scratch_shapes=[pltpu.VMEM((bm, d), BF16)],
        compiler_params=_params(("parallel", "arbitrary")),
        name="inproj",
    )(x2, mod3, w_main, wd_hi, wd_lo)


def _shifted_rows(u, prev_row, next_row):
    rows = u.shape[0]
    ridx = lax.broadcasted_iota(jnp.int32, u.shape, 0)
    up = jnp.where(ridx == 0, prev_row, pltpu.roll(u, 1, axis=0))
    un = jnp.where(ridx == rows - 1, next_row, pltpu.roll(u, rows - 1, axis=0))
    return up, un


def _halo_specs(bm, bw, col_of, t):
    per = bm // HALO_ROWS
    last = t // HALO_ROWS - 1
    prev = pl.BlockSpec((HALO_ROWS, bw), lambda i, j: (jnp.maximum(i * per - 1, 0), col_of(j)))
    nxt = pl.BlockSpec((HALO_ROWS, bw), lambda i, j: (jnp.minimum((i + 1) * per, last), col_of(j)))
    return prev, nxt


def _conv_kernel(u_ref, p_ref, n_ref, w_ref, b_ref, o_ref, *, bm, seq):
    i = pl.program_id(0)
    u = u_ref[...].astype(F32)
    has_prev = ((i * bm) % seq != 0).astype(F32)
    has_next = (((i + 1) * bm) % seq != 0).astype(F32)
    up, un = _shifted_rows(u, p_ref[HALO_ROWS - 1:HALO_ROWS, :].astype(F32) * has_prev,
                           n_ref[0:1, :].astype(F32) * has_next)
    w = w_ref[...]
    o_ref[...] = _silu(up * w[0:1] + u * w[1:2] + un * w[2:3] + b_ref[...])


def _conv(proj, conv_w, conv_b, seq):
    t = proj.shape[0]
    bm = min(512, seq)
    bw = 1024
    col0 = XBC_COL0 // bw
    col_of = lambda j: col0 + j
    prev, nxt = _halo_specs(bm, bw, col_of, t)
    return pl.pallas_call(
        functools.partial(_conv_kernel, bm=bm, seq=seq),
        grid=(t // bm, XBC_WIDTH // bw),
        in_specs=[pl.BlockSpec((bm, bw), lambda i, j: (i, col_of(j))), prev, nxt,
                  pl.BlockSpec((3, bw), lambda i, j: (0, j)),
                  pl.BlockSpec((1, bw), lambda i, j: (0, j))],
        out_specs=pl.BlockSpec((bm, bw), lambda i, j: (i, j)),
        out_shape=jax.ShapeDtypeStruct((t, XBC_WIDTH), F32),
        compiler_params=_params(("parallel", "parallel")),
        name="ssdconv",
    )(proj, proj, proj, conv_w, conv_b)


def _ssd_direction(x_ref, b_ref, c_ref, dt_ref, e_ref, h_scr, y_ref, bias, a, dskip, backward):
    cs_len = SSD_CHUNK
    row = lax.broadcasted_iota(jnp.int32, (cs_len, cs_len), 0)
    col = lax.broadcasted_iota(jnp.int32, (cs_len, cs_len), 1)
    tri = (col <= row).astype(BF16)
    dt = _softplus(dt_ref[...] + bias)
    adt = dt * a
    h3 = _split3(adt)
    cs = _dot(tri, h3[0]) + _dot(tri, h3[1]) + _dot(tri, h3[2])
    tot = cs[cs_len - 1:cs_len, :]
    if backward:
        ecs = cs - adt
        p = -ecs
        wst = dt * jnp.exp(ecs)
        indec = jnp.exp(tot - ecs)
        mask = col >= row
    else:
        p = cs
        wst = dt * jnp.exp(tot - cs)
        indec = jnp.exp(cs)
        mask = col <= row
    pt = p.T
    e = e_ref[...]
    dtx = _dot(dt.astype(BF16), e)
    wstx = _dot(wst.astype(BF16), e)
    indx = _dot(indec.astype(BF16), e)
    xs = x_ref[...]
    xd = (xs * dtx).astype(BF16)
    xw = (xs * wstx).astype(BF16)
    col_off = SSD_HEADS if backward else 0
    pair = 2 * SSD_HEADDIM
    first_of_pair = lax.broadcasted_iota(jnp.int32, (cs_len, pair), 1) < SSD_HEADDIM
    zero = jnp.zeros((), BF16)
    for g in range(SSD_GROUPS):
        gs = slice(g * GROUP_WIDTH, (g + 1) * GROUP_WIDTH)
        bg = b_ref[:, g * SSD_STATE:(g + 1) * SSD_STATE]
        cg = c_ref[:, g * SSD_STATE:(g + 1) * SSD_STATE].astype(BF16)
        cb = _dot_nt(cg, bg.astype(BF16))
        h_in = h_scr[g]
        y_off = _dot(cg, h_in.astype(BF16)) * indx[:, gs]
        st = _dot(bg.T.astype(BF16), xw[:, gs])
        for r in range(0, HEADS_PER_GROUP, 2):
            hd = g * HEADS_PER_GROUP + r
            ms = []
            for ci in (hd + col_off, hd + col_off + 1):
                seg = p[:, ci:ci + 1] - pt[ci:ci + 1, :]
                lmat = jnp.exp(jnp.where(mask, seg, -jnp.inf))
                ms.append((cb * lmat).astype(BF16))
            hs = slice(hd * SSD_HEADDIM, (hd + 2) * SSD_HEADDIM)
            x2h = xd[:, hs]
            rhs = jnp.concatenate([jnp.where(first_of_pair, x2h, zero), jnp.where(first_of_pair, zero, x2h)], axis=0)
            y = _dot(jnp.concatenate(ms, axis=1), rhs) + y_off[:, r * SSD_HEADDIM:(r + 2) * SSD_HEADDIM]
            if dskip is not None:
                y = y + xs[:, hs] * dskip[:, hs]
            y_ref[:, hs] = y
        edge = 0 if backward else cs_len - 1
        h_scr[g] = indx[edge:edge + 1, gs] * h_in + st


def _ssd_kernel(xf_ref, bf_ref, cf_ref, dtf_ref, xb_ref, bb_ref, cb_ref, dtb_ref,
                bias_ref, alog_ref, dskip_ref, ef_ref, eb_ref, yf_ref, yb_ref, hf_scr, hb_scr):
    @pl.when(pl.program_id(1) == 0)
    def _():
        hf_scr[...] = jnp.zeros_like(hf_scr)
        hb_scr[...] = jnp.zeros_like(hb_scr)

    bias = bias_ref[...]
    a = -jnp.exp(alog_ref[...])
    _ssd_direction(xf_ref, bf_ref, cf_ref, dtf_ref, ef_ref, hf_scr, yf_ref, bias, a, dskip_ref[...], False)
    _ssd_direction(xb_ref, bb_ref, cb_ref, dtb_ref, eb_ref, hb_scr, yb_ref, bias, a, None, True)


def _ssd(xbc, dt_raw, bias, alog, dskip_x, e_f, e_b, batch, seq):
    t = xbc.shape[0]
    nc = seq // SSD_CHUNK
    cl = SSD_CHUNK
    gn = SSD_GROUPS * SSD_STATE
    bcol = SSD_WIDTH // gn
    fwd = lambda b, k: b * nc + k
    bwd = lambda b, k: b * nc + (nc - 1 - k)

    def specs(ch):
        return [pl.BlockSpec((cl, SSD_WIDTH), lambda b, k: (ch(b, k), 0)),
                pl.BlockSpec((cl, gn), lambda b, k: (ch(b, k), bcol)),
                pl.BlockSpec((cl, gn), lambda b, k: (ch(b, k), bcol + 1)),
                pl.BlockSpec((cl, LANES), lambda b, k: (ch(b, k), 0))]

    const = lambda shape: pl.BlockSpec(shape, lambda b, k: (0, 0))
    state = pltpu.VMEM((SSD_GROUPS, SSD_STATE, GROUP_WIDTH), F32)
    return pl.pallas_call(
        _ssd_kernel,
        grid=(batch, nc),
        in_specs=specs(fwd) + specs(bwd) + [const((1, LANES)), const((1, LANES)), const((1, SSD_WIDTH)),
                                            const((LANES, SSD_WIDTH)), const((LANES, SSD_WIDTH))],
        out_specs=[pl.BlockSpec((cl, SSD_WIDTH), lambda b, k: (fwd(b, k), 0)),
                   pl.BlockSpec((cl, SSD_WIDTH), lambda b, k: (bwd(b, k), 0))],
        out_shape=[jax.ShapeDtypeStruct((t, SSD_WIDTH), F32)] * 2,
        scratch_shapes=[state, state],
        compiler_params=_params(("parallel", "arbitrary")),
        name="ssd",
    )(xbc, xbc, xbc, dt_raw, xbc, xbc, xbc, dt_raw, bias, alog, dskip_x, e_f, e_b)


def _group_rms(y, r_ref, e_ref, group_size):
    gsum = _dot((y * y).astype(BF16), r_ref[...])
    inv = lax.rsqrt(gsum * (1.0 / group_size) + NORM_EPS)
    return y * _dot(inv.astype(BF16), e_ref[...])


def _layer_norm(v, g, b):
    mu = jnp.mean(v, axis=-1, keepdims=True)
    vc = v - mu
    var = jnp.mean(vc * vc, axis=-1, keepdims=True)
    return vc * lax.rsqrt(var + NORM_EPS) * g + b


def _post_kernel(yf_ref, yb_ref, z_ref, gb_ref, gc_ref, gcp_ref, gcn_ref, v_ref, vp_ref, vn_ref,
                 x_ref, mod_ref, scw_ref, ssdg_ref, scg_ref, r4_ref, e4_ref, r32_ref, e32_ref,
                 wout_ref, lng_ref, lnb_ref, x1_ref, h2_ref, amax_ref, sumsq_ref, *, bm, seq, alpha):
    i = pl.program_id(0)
    m = mod_ref[0]
    y = (yf_ref[...] + yb_ref[...]) * _silu(z_ref[...].astype(F32))
    y_ssd = _group_rms(y, r4_ref, e4_ref, SSD_WIDTH // SSD_GROUPS) * ssdg_ref[...]

    f32 = lambda ref, rows=slice(None): ref[rows, :].astype(F32)
    u = f32(gc_ref) * f32(v_ref)
    has_prev = ((i * bm) % seq != 0).astype(F32)
    has_next = (((i + 1) * bm) % seq != 0).astype(F32)
    last = slice(HALO_ROWS - 1, HALO_ROWS)
    first = slice(0, 1)
    up, un = _shifted_rows(u, f32(gcp_ref, last) * f32(vp_ref, last) * has_prev,
                           f32(gcn_ref, first) * f32(vn_ref, first) * has_next)
    w = scw_ref[...]
    y_sc = f32(gb_ref) * (up * w[0:1] + u * w[1:2] + un * w[2:3])
    y_sc = _group_rms(y_sc, r32_ref, e32_ref, SC_WIDTH // SC_GROUPS) * scg_ref[...]

    mix = (_dot(y_ssd.astype(BF16), wout_ref[0:SSD_WIDTH, :])
           + _dot(y_sc.astype(BF16), wout_ref[SSD_WIDTH:SSD_WIDTH + SC_WIDTH, :]))
    x1 = _layer_norm(alpha * x_ref[...] + m[2:3] * mix, lng_ref[...], lnb_ref[...])
    x1_ref[...] = x1
    h2 = x1 * (1.0 + m[4:5]) + m[3:4]
    h2_ref[...] = h2.astype(BF16)
    amax_ref[...] = jnp.max(jnp.abs(h2), axis=-1, keepdims=True)
    sumsq_ref[...] = jnp.sum(h2 * h2, axis=-1, keepdims=True)


def _single(shape, index_map):
    return pl.BlockSpec(shape, index_map, pipeline_mode=pl.Buffered(1))


def _post(yf, yb, proj, x2, mod3, scw, ssdg, scg, r4, e4, r32, e32, wout, lng, lnb, seq, alpha):
    t, d = x2.shape
    bm = min(256, seq)
    per_batch = seq // bm
    per = bm // HALO_ROWS
    last = t // HALO_ROWS - 1
    main = lambda c: pl.BlockSpec((bm, d), lambda i: (i, c))
    prev = lambda c: pl.BlockSpec((HALO_ROWS, d), lambda i: (jnp.maximum(i * per - 1, 0), c))
    nxt = lambda c: pl.BlockSpec((HALO_ROWS, d), lambda i: (jnp.minimum((i + 1) * per, last), c))
    const = lambda shape: _single(shape, lambda i: (0, 0))
    return pl.pallas_call(
        functools.partial(_post_kernel, bm=bm, seq=seq, alpha=alpha),
        grid=(t // bm,),
        in_specs=[main(0), main(0), main(0), main(1), main(2), prev(2), nxt(2), main(3), prev(3), nxt(3),
                  main(0), pl.BlockSpec((1, 6, d), lambda i: (i // per_batch, 0, 0)),
                  const((3, d)), const((1, d)), const((1, d)),
                  const((d, LANES)), const((LANES, d)), const((d, LANES)), const((LANES, d)),
                  const((2 * d, d)), const((1, d)), const((1, d))],
        out_specs=[pl.BlockSpec((bm, d), lambda i: (i, 0)), pl.BlockSpec((bm, d), lambda i: (i, 0)),
                   pl.BlockSpec((bm, 1), lambda i: (i, 0)), pl.BlockSpec((bm, 1), lambda i: (i, 0))],
        out_shape=[jax.ShapeDtypeStruct((t, d), F32), jax.ShapeDtypeStruct((t, d), BF16),
                   jax.ShapeDtypeStruct((t, 1), F32), jax.ShapeDtypeStruct((t, 1), F32)],
        compiler_params=_params(("parallel",)),
        name="post",
    )(yf, yb, proj, proj, proj, proj, proj, proj, proj, proj, x2, mod3,
      scw, ssdg, scg, r4, e4, r32, e32, wout, lng, lnb)


def _top_values(s, n, want_rank=False):
    vals = []
    rank = jnp.full(s.shape, float(n), F32) if want_rank else None
    for r in range(n):
        m = jnp.max(s, axis=0, keepdims=True)
        vals.append(m)
        hit = s == m
        if want_rank:
            rank = jnp.where(hit, float(r), rank)
        s = jnp.where(hit, -jnp.inf, s)
    return (vals, rank) if want_rank else vals


def _query_kernel(h2_ref, wq_ref, k_ref, r2_ref, e2_ref, n1_ref, c1_ref):
    q = _dot(h2_ref[...], wq_ref[...]).astype(BF16)
    tb = q.shape[0]
    for h in range(PEER_HEADS):
        q1 = q[:, h * D_QUERY:h * D_QUERY + HALF_QUERY]
        q2 = q[:, h * D_QUERY + HALF_QUERY:(h + 1) * D_QUERY]
        s1 = _dot_nt(k_ref[h, 0], q1)
        s2 = _dot_nt(k_ref[h, 1], q2)
        v1 = _top_values(s1, PEER_TOPK)
        v2, r2 = _top_values(s2, PEER_TOPK, want_rank=True)
        pairs = [(a, b) for a in range(PEER_TOPK) for b in range(PEER_TOPK) if (a + 1) * (b + 1) <= PEER_TOPK]
        sums = {ab: v1[ab[0]] + v2[ab[1]] for ab in pairs}
        rows = [sums[ab] for ab in pairs]
        rows += [jnp.full((1, tb), -jnp.inf, F32)] * ((-len(rows)) % SUBLANES)
        top = _top_values(jnp.concatenate(rows, axis=0), PEER_TOPK)
        z = jnp.ones((1, tb), F32)
        for kk in range(1, PEER_TOPK):
            z = z + jnp.exp(top[kk] - top[0])
        tau = top[PEER_TOPK - 1]
        n1 = jnp.zeros(s1.shape, F32)
        for a in range(PEER_TOPK):
            cnt = sum(jnp.where(sums[(a, b)] >= tau, 1.0, 0.0) for b in range(PEER_TOPK) if (a, b) in sums)
            n1 = jnp.where(s1 == v1[a], cnt, n1)
        r2_ref[h] = r2.astype(BF16)
        e2_ref[h] = jnp.exp(s2 - v2[0]).astype(BF16)
        n1_ref[h] = n1
        c1_ref[h] = jnp.exp(s1 - v1[0]) / z


def _query(h2, wq, keys):
    t, d = h2.shape
    tb = min(256, t)
    ospec = pl.BlockSpec((PEER_HEADS, N_KEYS, tb), lambda i: (0, 0, i))
    return pl.pallas_call(
        _query_kernel,
        grid=(t // tb,),
        in_specs=[pl.BlockSpec((tb, d), lambda i: (i, 0)),
                  _single((d, PEER_HEADS * D_QUERY), lambda i: (0, 0)),
                  _single((PEER_HEADS, 2, N_KEYS, HALF_QUERY), lambda i: (0, 0, 0, 0))],
        out_specs=[ospec] * 4,
        out_shape=[jax.ShapeDtypeStruct((PEER_HEADS, N_KEYS, t), dt) for dt in (BF16, BF16, F32, F32)],
        compiler_params=_params(("parallel",)),
        name="peerquery",
    )(h2, wq, keys)


def _pow2_scale(bound):
    ratio = FP8_TARGET / jnp.where(bound > 0, bound, FP8_TARGET)
    exponent_only = lax.bitcast_convert_type(ratio, jnp.int32) & jnp.int32(0x7F800000)
    return jnp.minimum(lax.bitcast_convert_type(exponent_only, F32), FP8_SCALE_CAP)


def _expert_prep_kernel(w_ref, q_ref, inv_ref, sumsq_ref, *, transpose):
    w = w_ref[...]
    s = _pow2_scale(jnp.max(jnp.abs(w), axis=1, keepdims=True))
    inv_ref[...] = 1.0 / s
    sumsq_ref[...] = jnp.sum(w * w, axis=1, keepdims=True)
    ws = w * s
    q_ref[...] = (ws.T if transpose else ws).astype(FP8)


def _expert_prep(table, transpose):
    n_exp, d = table.shape
    rows = 512
    col = pl.BlockSpec((rows, 1), lambda i: (i, 0))
    q_spec = pl.BlockSpec((d, rows), lambda i: (0, i)) if transpose else pl.BlockSpec((rows, d), lambda i: (i, 0))
    q_shape = (d, n_exp) if transpose else (n_exp, d)
    return pl.pallas_call(
        functools.partial(_expert_prep_kernel, transpose=transpose),
        grid=(n_exp // rows,),
        in_specs=[pl.BlockSpec((rows, d), lambda i: (i, 0))],
        out_specs=[q_spec, col, col],
        out_shape=[jax.ShapeDtypeStruct(q_shape, FP8), jax.ShapeDtypeStruct((n_exp, 1), F32),
                   jax.ShapeDtypeStruct((n_exp, 1), F32)],
        compiler_params=_params(("parallel",)),
        name="expertprep_t" if transpose else "expertprep",
    )(table)


def _peer_kernel(h2_ref, u_ref, vt_ref, r2_ref, e2_ref, n1_ref, c1_ref, x1_ref, mod_ref, lng_ref, lnb_ref,
                 iu_ref, iv_ref, sh_ref, ish_ref, sp_ref, isp_ref, o_ref, acc_ref, h8_ref, *, alpha):
    e = pl.program_id(1)

    @pl.when(e == 0)
    def _():
        acc_ref[...] = jnp.zeros_like(acc_ref)
        h8_ref[...] = (h2_ref[...].astype(F32) * sh_ref[...]).astype(FP8)

    chunks = [slice(s, s + PEER_TOKEN_CHUNK) for s in range(0, h2_ref.shape[0], PEER_TOKEN_CHUNK)]
    ats = [_dot_nt(u_ref[...], h8_ref[ls, :]) for ls in chunks]
    for ls, at in zip(chunks, ats):
        a_scale = iu_ref[...] * ish_ref[:, ls]
        p_scale = iv_ref[...] * sp_ref[:, ls]
        parts = []
        for ii in range(u_ref.shape[0] // N_KEYS):
            g = None
            for h in range(PEER_HEADS):
                n_row = n1_ref[h, ii:ii + 1, ls].astype(BF16)
                c_row = c1_ref[h, ii:ii + 1, ls].astype(BF16)
                w = jnp.where(r2_ref[h, :, ls] < n_row, e2_ref[h, :, ls], jnp.zeros((), BF16)) * c_row
                g = w if g is None else g + w
            rows = slice(ii * N_KEYS, (ii + 1) * N_KEYS)
            b = at[rows, :] * a_scale[rows, :]
            parts.append(((b * (1.0 + lax.erf(b)) * p_scale[rows, :]).astype(BF16) * g).astype(FP8))
        pt = jnp.concatenate(parts, axis=0)
        acc_ref[:, ls] += _dot(vt_ref[...], pt)

    @pl.when(e == pl.num_programs(1) - 1)
    def _():
        m = mod_ref[0]
        ffn = acc_ref[...].T * isp_ref[...]
        o_ref[...] = _layer_norm(alpha * x1_ref[...] + m[5:6] * ffn, lng_ref[...], lnb_ref[...])


def _peer(h2, u8, vt8, r2, e2, n1, c1, x1, mod3, lng, lnb, iu, iv, sh, ish, sp, isp, seq, alpha):
    t, d = h2.shape
    n_exp = u8.shape[0]
    tb = min(512, seq)
    eb = 1024
    per_batch = seq // tb
    kb = eb // N_KEYS
    full = _single((PEER_HEADS, N_KEYS, tb), lambda i, j: (0, 0, i))
    part = pl.BlockSpec((PEER_HEADS, kb, tb), lambda i, j: (0, j, i))
    ecol = pl.BlockSpec((eb, 1), lambda i, j: (j, 0))
    tcol = _single((tb, 1), lambda i, j: (i, 0))
    trow = _single((1, tb), lambda i, j: (0, i))
    return pl.pallas_call(
        functools.partial(_peer_kernel, alpha=alpha),
        grid=(t // tb, n_exp // eb),
        in_specs=[_single((tb, d), lambda i, j: (i, 0)),
                  pl.BlockSpec((eb, d), lambda i, j: (j, 0)),
                  pl.BlockSpec((d, eb), lambda i, j: (0, j)),
                  full, full, part, part,
                  _single((tb, d), lambda i, j: (i, 0)),
                  pl.BlockSpec((1, 6, d), lambda i, j: (i // per_batch, 0, 0)),
                  pl.BlockSpec((1, d), lambda i, j: (0, 0)),
                  pl.BlockSpec((1, d), lambda i, j: (0, 0)),
                  ecol, ecol, tcol, trow, trow, tcol],
        out_specs=pl.BlockSpec((tb, d), lambda i, j: (i, 0)),
        out_shape=jax.ShapeDtypeStruct((t, d), F32),
        scratch_shapes=[pltpu.VMEM((d, tb), F32), pltpu.VMEM((tb, d), FP8)],
        compiler_params=_params(("parallel", "arbitrary")),
        name="peer",
    )(h2, u8, vt8, r2, e2, n1, c1, x1, mod3, lng, lnb, iu, iv, sh, ish, sp, isp)


def _ffn(h2, h_amax, h_sumsq, x1, mod3, w_query, sub_keys, expert_u, expert_v, ln2_g, ln2_b, seq, alpha):
    t = h2.shape[0]
    r2, e2, n1, c1 = _query(h2, w_query.astype(BF16), sub_keys.astype(BF16))
    u8, inv_su, u_sumsq = _expert_prep(expert_u, transpose=False)
    vt8, inv_sv, _ = _expert_prep(expert_v, transpose=True)
    sh = _pow2_scale(h_amax)
    sp = _pow2_scale(jnp.sqrt(jnp.max(u_sumsq) * h_sumsq) * (PEER_HEADS * jnp.max(inv_sv)))
    sqrt_half = np.float32(math.sqrt(0.5))
    return _peer(h2, u8, vt8, r2, e2, n1, c1, x1, mod3, _row(ln2_g), _row(ln2_b), inv_su * sqrt_half,
                 inv_sv * sqrt_half, sh, (1.0 / sh).reshape(1, t), sp.reshape(1, t), 1.0 / sp, seq, alpha)


def _one_hot_cols(n_rows, n_cols, group_size, row_offset=0):
    r = np.arange(n_rows)[:, None]
    c = np.arange(n_cols)[None, :]
    return jnp.asarray(r == row_offset + c // group_size, dtype=BF16)


def _row(v):
    return v.reshape(1, -1).astype(F32)


def _pad_lanes(v):
    return jnp.pad(v, (0, LANES - v.shape[0])).reshape(1, LANES).astype(F32)


def kernel(x, c, w_ada, b_ada, w_in, conv_ssd_w, conv_ssd_b, dt_bias_f, dt_bias_b, a_log_f, a_log_b, d_skip, ssd_norm_g, short_conv_w, sc_norm_g, w_out, ln1_g, ln1_b, w_query, sub_keys, expert_u, expert_v, ln2_g, ln2_b):
    batch, seq, d = x.shape
    depth = w_ada.shape[0]
    alpha = (2.0 * depth) ** 0.25
    t = batch * seq
    x2 = x.reshape(t, d)
    c_pad = jnp.pad(c, ((0, SUBLANES - batch % SUBLANES), (0, 0))) if batch % SUBLANES else c

    e_f = _one_hot_cols(LANES, SSD_WIDTH, SSD_HEADDIM, 0)
    e_b = _one_hot_cols(LANES, SSD_WIDTH, SSD_HEADDIM, SSD_HEADS)
    e4 = _one_hot_cols(LANES, SSD_WIDTH, SSD_WIDTH // SSD_GROUPS)
    e32 = _one_hot_cols(LANES, SC_WIDTH, SC_WIDTH // SC_GROUPS)
    r4 = e4.T
    r32 = e32.T

    for i in range(depth):
        mod = _ada(c_pad, w_ada[i], b_ada[i].reshape(1, -1))
        mod3 = mod[:batch].reshape(batch, 6, d)

        w_main, wd_hi, wd_lo = _win_prep(w_in[i])
        proj, dt_raw = _inproj(x2, mod3, w_main, wd_hi, wd_lo, seq)

        xbc = _conv(proj, conv_ssd_w[i], conv_ssd_b[i].reshape(1, -1), seq)
        bias = _pad_lanes(jnp.concatenate([dt_bias_f[i], dt_bias_b[i]]))
        alog = _pad_lanes(jnp.concatenate([a_log_f[i], a_log_b[i]]))
        dskip_x = _row(jnp.repeat(d_skip[i], SSD_HEADDIM))
        y_f, y_b = _ssd(xbc, dt_raw, bias, alog, dskip_x, e_f, e_b, batch, seq)

        x1, h2, h_amax, h_sumsq = _post(y_f, y_b, proj, x2, mod3, short_conv_w[i], _row(ssd_norm_g[i]),
                                        _row(sc_norm_g[i]), r4, e4, r32, e32, w_out[i].astype(BF16),
                                        _row(ln1_g[i]), _row(ln1_b[i]), seq, alpha)

        x2 = _ffn(h2, h_amax, h_sumsq, x1, mod3, w_query[i], sub_keys[i], expert_u[i], expert_v[i],
                  ln2_g[i], ln2_b[i], seq, alpha)
    return x2.reshape(batch, seq, d)
```

```python
import functools
import math

import jax
import jax.numpy as jnp
import numpy as np
from jax import lax
from jax.experimental import pallas as pl
from jax.experimental.pallas import tpu as pltpu

F32 = jnp.float32
BF16 = jnp.bfloat16
FP8 = jnp.float8_e4m3fn
FP8_TARGET = 256.0
FP8_SCALE_CAP = 2.0 ** 60

D_MODEL = 2048
SSD_WIDTH = D_MODEL
SSD_HEADDIM = 64
SSD_HEADS = SSD_WIDTH // SSD_HEADDIM
SSD_GROUPS = 4
SSD_STATE = 128
SSD_CHUNK = 128
HEADS_PER_GROUP = SSD_HEADS // SSD_GROUPS
GROUP_WIDTH = HEADS_PER_GROUP * SSD_HEADDIM
SC_WIDTH = D_MODEL
SC_GROUPS = 32
XBC_WIDTH = SSD_WIDTH + 2 * SSD_GROUPS * SSD_STATE
MAIN_COLS = 4 * D_MODEL + XBC_WIDTH
XBC_COL0 = 4 * D_MODEL
PEER_HEADS = 8
N_KEYS = 128
PEER_TOPK = 16
D_QUERY = 512
HALF_QUERY = D_QUERY // 2
NORM_EPS = 1e-5
LANES = 128
SUBLANES = 8
HALO_ROWS = 16
POST_COL_PIECE = 256
VMEM_LIMIT = 56 * 1024 * 1024


def _dot(a, b):
    return jnp.dot(a, b, preferred_element_type=F32)


def _dot_nt(a, b):
    return lax.dot_general(a, b, (((1,), (1,)), ((), ())), preferred_element_type=F32)


def _split2(x):
    hi = x.astype(BF16)
    lo = (x - hi.astype(F32)).astype(BF16)
    return hi, lo


def _split3(x):
    hi = x.astype(BF16)
    r = x - hi.astype(F32)
    mid = r.astype(BF16)
    lo = (r - mid.astype(F32)).astype(BF16)
    return hi, mid, lo


def _softplus(x):
    return jnp.maximum(x, 0.0) + jnp.log1p(jnp.exp(-jnp.abs(x)))


def _silu(x):
    return x * (1.0 / (1.0 + jnp.exp(-x)))


def _params(sem):
    return pltpu.CompilerParams(dimension_semantics=sem, vmem_limit_bytes=VMEM_LIMIT)


def _ada_kernel(c_ref, w_ref, b_ref, o_ref):
    sc = _silu(c_ref[...])
    o_ref[...] = _dot(sc.astype(BF16), w_ref[...].astype(BF16)) + b_ref[...]


def _ada(c_pad, w, b):
    rows, d = c_pad.shape
    n = w.shape[1]
    bn = 1024
    return pl.pallas_call(
        _ada_kernel,
        grid=(n // bn,),
        in_specs=[pl.BlockSpec((rows, d), lambda j: (0, 0)),
                  pl.BlockSpec((d, bn), lambda j: (0, j)),
                  pl.BlockSpec((1, bn), lambda j: (0, j))],
        out_specs=pl.BlockSpec((rows, bn), lambda j: (0, j)),
        out_shape=jax.ShapeDtypeStruct((rows, n), F32),
        compiler_params=_params(("arbitrary",)),
        name="ada",
    )(c_pad, w, b)


DT_COL0 = SSD_WIDTH + XBC_WIDTH
GATES_COL0 = DT_COL0 + 2 * SSD_HEADS


def _win_prep_kernel(w_ref, main_ref, dh_ref, dl_ref):
    w = w_ref[...]
    main_ref[:, 0:SSD_WIDTH] = w[:, 0:SSD_WIDTH].astype(BF16)
    main_ref[:, SSD_WIDTH:XBC_COL0] = w[:, GATES_COL0:GATES_COL0 + 3 * SC_WIDTH].astype(BF16)
    main_ref[:, XBC_COL0:MAIN_COLS] = w[:, SSD_WIDTH:DT_COL0].astype(BF16)
    dt = w[:, DT_COL0:DT_COL0 + LANES]
    lane = lax.broadcasted_iota(jnp.int32, dt.shape, 1)
    hi, lo = _split2(jnp.where(lane < 2 * SSD_HEADS, dt, 0.0))
    dh_ref[...] = hi
    dl_ref[...] = lo


def _win_prep(w):
    d, n = w.shape
    rows = 128
    return pl.pallas_call(
        _win_prep_kernel,
        grid=(d // rows,),
        in_specs=[pl.BlockSpec((rows, n), lambda i: (i, 0))],
        out_specs=[pl.BlockSpec((rows, MAIN_COLS), lambda i: (i, 0)),
                   pl.BlockSpec((rows, LANES), lambda i: (i, 0)),
                   pl.BlockSpec((rows, LANES), lambda i: (i, 0))],
        out_shape=[jax.ShapeDtypeStruct((d, MAIN_COLS), BF16), jax.ShapeDtypeStruct((d, LANES), BF16),
                   jax.ShapeDtypeStruct((d, LANES), BF16)],
        compiler_params=_params(("parallel",)),
        name="winprep",
    )(w)


def _inproj_kernel(x_ref, mod_ref, w_ref, wdh_ref, wdl_ref, o_ref, dt_ref, h_scr):
    @pl.when(pl.program_id(1) == 0)
    def _():
        m = mod_ref[0]
        h = x_ref[...] * (1.0 + m[1:2]) + m[0:1]
        hi, lo = _split2(h)
        h_scr[...] = hi
        dt_ref[...] = _dot(hi, wdh_ref[...]) + _dot(lo, wdh_ref[...]) + _dot(hi, wdl_ref[...])

    o_ref[...] = _dot(h_scr[...], w_ref[...]).astype(o_ref.dtype)


def _inproj(x2, mod3, w_main, wd_hi, wd_lo, seq):
    t, d = x2.shape
    n = w_main.shape[1]
    bm = min(1024, seq)
    bn = 1024
    per_batch = seq // bm
    return pl.pallas_call(
        _inproj_kernel,
        grid=(t // bm, n // bn),
        in_specs=[pl.BlockSpec((bm, d), lambda i, j: (i, 0)),
                  pl.BlockSpec((1, 6, d), lambda i, j: (i // per_batch, 0, 0)),
                  pl.BlockSpec((d, bn), lambda i, j: (0, j)),
                  pl.BlockSpec((d, LANES), lambda i, j: (0, 0)),
                  pl.BlockSpec((d, LANES), lambda i, j: (0, 0))],
        out_specs=[pl.BlockSpec((bm, bn), lambda i, j: (i, j)),
                   pl.BlockSpec((bm, LANES), lambda i, j: (i, 0))],
        out_shape=[jax.ShapeDtypeStruct((t, n), BF16),
                   jax.ShapeDtypeStruct((t, LANES), F32)],
        scratch_shapes=[pltpu.VMEM((bm, d), BF16)],
        compiler_params=_params(("parallel", "arbitrary")),
        name="inproj",
    )(x2, mod3, w_main, wd_hi, wd_lo)


def _shifted_rows(u, prev_row, next_row):
    rows = u.shape[0]
    ridx = lax.broadcasted_iota(jnp.int32, u.shape, 0)
    up = jnp.where(ridx == 0, prev_row, pltpu.roll(u, 1, axis=0))
    un = jnp.where(ridx == rows - 1, next_row, pltpu.roll(u, rows - 1, axis=0))
    return up, un


def _halo_specs(bm, bw, col_of, t):
    per = bm // HALO_ROWS
    last = t // HALO_ROWS - 1
    prev = pl.BlockSpec((HALO_ROWS, bw), lambda i, j: (jnp.maximum(i * per - 1, 0), col_of(j)))
    nxt = pl.BlockSpec((HALO_ROWS, bw), lambda i, j: (jnp.minimum((i + 1) * per, last), col_of(j)))
    return prev, nxt


def _conv_kernel(u_ref, p_ref, n_ref, w_ref, b_ref, o_ref, *, bm, seq):
    i = pl.program_id(0)
    u = u_ref[...].astype(F32)
    has_prev = ((i * bm) % seq != 0).astype(F32)
    has_next = (((i + 1) * bm) % seq != 0).astype(F32)
    up, un = _shifted_rows(u, p_ref[HALO_ROWS - 1:HALO_ROWS, :].astype(F32) * has_prev,
                           n_ref[0:1, :].astype(F32) * has_next)
    w = w_ref[...]
    o_ref[...] = _silu(up * w[0:1] + u * w[1:2] + un * w[2:3] + b_ref[...])


def _conv(proj, conv_w, conv_b, seq):
    t = proj.shape[0]
    bm = min(512, seq)
    bw = 1024
    col0 = XBC_COL0 // bw
    col_of = lambda j: col0 + j
    prev, nxt = _halo_specs(bm, bw, col_of, t)
    return pl.pallas_call(
        functools.partial(_conv_kernel, bm=bm, seq=seq),
        grid=(t // bm, XBC_WIDTH // bw),
        in_specs=[pl.BlockSpec((bm, bw), lambda i, j: (i, col_of(j))), prev, nxt,
                  pl.BlockSpec((3, bw), lambda i, j: (0, j)),
                  pl.BlockSpec((1, bw), lambda i, j: (0, j))],
        out_specs=pl.BlockSpec((bm, bw), lambda i, j: (i, j)),
        out_shape=jax.ShapeDtypeStruct((t, XBC_WIDTH), F32),
        compiler_params=_params(("parallel", "parallel")),
        name="ssdconv",
    )(proj, proj, proj, conv_w, conv_b)


def _ssd_direction(x_ref, b_ref, c_ref, dt_ref, e_ref, h_scr, y_ref, bias, a, dskip, backward):
    cs_len = SSD_CHUNK
    row = lax.broadcasted_iota(jnp.int32, (cs_len, cs_len), 0)
    col = lax.broadcasted_iota(jnp.int32, (cs_len, cs_len), 1)
    tri = (col <= row).astype(BF16)
    dt = _softplus(dt_ref[...] + bias)
    adt = dt * a
    h3 = _split3(adt)
    cs = _dot(tri, h3[0]) + _dot(tri, h3[1]) + _dot(tri, h3[2])
    tot = cs[cs_len - 1:cs_len, :]
    if backward:
        ecs = cs - adt
        p = -ecs
        wst = dt * jnp.exp(ecs)
        indec = jnp.exp(tot - ecs)
        mask = col >= row
    else:
        p = cs
        wst = dt * jnp.exp(tot - cs)
        indec = jnp.exp(cs)
        mask = col <= row
    pt = p.T
    e = e_ref[...]
    dtx = _dot(dt.astype(BF16), e)
    wstx = _dot(wst.astype(BF16), e)
    indx = _dot(indec.astype(BF16), e)
    xs = x_ref[...]
    xd = (xs * dtx).astype(BF16)
    xw = (xs * wstx).astype(BF16)
    col_off = SSD_HEADS if backward else 0
    pair = 2 * SSD_HEADDIM
    first_of_pair = lax.broadcasted_iota(jnp.int32, (cs_len, pair), 1) < SSD_HEADDIM
    zero = jnp.zeros((), BF16)
    for g in range(SSD_GROUPS):
        gs = slice(g * GROUP_WIDTH, (g + 1) * GROUP_WIDTH)
        bg = b_ref[:, g * SSD_STATE:(g + 1) * SSD_STATE]
        cg = c_ref[:, g * SSD_STATE:(g + 1) * SSD_STATE].astype(BF16)
        cb = _dot_nt(cg, bg.astype(BF16))
        h_in = h_scr[g]
        y_off = _dot(cg, h_in.astype(BF16)) * indx[:, gs]
        st = _dot(bg.T.astype(BF16), xw[:, gs])
        for r in range(0, HEADS_PER_GROUP, 2):
            hd = g * HEADS_PER_GROUP + r
            ms = []
            for ci in (hd + col_off, hd + col_off + 1):
                seg = p[:, ci:ci + 1] - pt[ci:ci + 1, :]
                lmat = jnp.exp(jnp.where(mask, seg, -jnp.inf))
                ms.append((cb * lmat).astype(BF16))
            hs = slice(hd * SSD_HEADDIM, (hd + 2) * SSD_HEADDIM)
            x2h = xd[:, hs]
            rhs = jnp.concatenate([jnp.where(first_of_pair, x2h, zero), jnp.where(first_of_pair, zero, x2h)], axis=0)
            y = _dot(jnp.concatenate(ms, axis=1), rhs) + y_off[:, r * SSD_HEADDIM:(r + 2) * SSD_HEADDIM]
            if dskip is not None:
                y = y + xs[:, hs] * dskip[:, hs]
            y_ref[:, hs] = y
        edge = 0 if backward else cs_len - 1
        h_scr[g] = indx[edge:edge + 1, gs] * h_in + st


def _ssd_kernel(xf_ref, bf_ref, cf_ref, dtf_ref, xb_ref, bb_ref, cb_ref, dtb_ref,
                bias_ref, alog_ref, dskip_ref, ef_ref, eb_ref, yf_ref, yb_ref, hf_scr, hb_scr):
    @pl.when(pl.program_id(1) == 0)
    def _():
        hf_scr[...] = jnp.zeros_like(hf_scr)
        hb_scr[...] = jnp.zeros_like(hb_scr)

    bias = bias_ref[...]
    a = -jnp.exp(alog_ref[...])
    _ssd_direction(xf_ref, bf_ref, cf_ref, dtf_ref, ef_ref, hf_scr, yf_ref, bias, a, dskip_ref[...], False)
    _ssd_direction(xb_ref, bb_ref, cb_ref, dtb_ref, eb_ref, hb_scr, yb_ref, bias, a, None, True)


def _ssd(xbc, dt_raw, bias, alog, dskip_x, e_f, e_b, batch, seq):
    t = xbc.shape[0]
    nc = seq // SSD_CHUNK
    cl = SSD_CHUNK
    gn = SSD_GROUPS * SSD_STATE
    bcol = SSD_WIDTH // gn
    fwd = lambda b, k: b * nc + k
    bwd = lambda b, k: b * nc + (nc - 1 - k)

    def specs(ch):
        return [pl.BlockSpec((cl, SSD_WIDTH), lambda b, k: (ch(b, k), 0)),
                pl.BlockSpec((cl, gn), lambda b, k: (ch(b, k), bcol)),
                pl.BlockSpec((cl, gn), lambda b, k: (ch(b, k), bcol + 1)),
                pl.BlockSpec((cl, LANES), lambda b, k: (ch(b, k), 0))]

    const = lambda shape: pl.BlockSpec(shape, lambda b, k: (0, 0))
    state = pltpu.VMEM((SSD_GROUPS, SSD_STATE, GROUP_WIDTH), F32)
    return pl.pallas_call(
        _ssd_kernel,
        grid=(batch, nc),
        in_specs=specs(fwd) + specs(bwd) + [const((1, LANES)), const((1, LANES)), const((1, SSD_WIDTH)),
                                            const((LANES, SSD_WIDTH)), const((LANES, SSD_WIDTH))],
        out_specs=[pl.BlockSpec((cl, SSD_WIDTH), lambda b, k: (fwd(b, k), 0)),
                   pl.BlockSpec((cl, SSD_WIDTH), lambda b, k: (bwd(b, k), 0))],
        out_shape=[jax.ShapeDtypeStruct((t, SSD_WIDTH), F32)] * 2,
        scratch_shapes=[state, state],
        compiler_params=_params(("parallel", "arbitrary")),
        name="ssd",
    )(xbc, xbc, xbc, dt_raw, xbc, xbc, xbc, dt_raw, bias, alog, dskip_x, e_f, e_b)


def _group_rms(y, r_ref, e_ref, group_size):
    gsum = _dot((y * y).astype(BF16), r_ref[...])
    inv = lax.rsqrt(gsum * (1.0 / group_size) + NORM_EPS)
    return y * _dot(inv.astype(BF16), e_ref[...])


def _layer_norm(v, g, b):
    mu = jnp.mean(v, axis=-1, keepdims=True)
    vc = v - mu
    var = jnp.mean(vc * vc, axis=-1, keepdims=True)
    return vc * lax.rsqrt(var + NORM_EPS) * g + b


def _post_kernel(yf_ref, yb_ref, z_ref, gb_ref, gc_ref, gcp_ref, gcn_ref, v_ref, vp_ref, vn_ref,
                 x_ref, mod_ref, scw_ref, ssdg_ref, scg_ref, r4_ref, e4_ref, r32_ref, e32_ref,
                 wout_ref, lng_ref, lnb_ref, x1_ref, h2_ref, amax_ref, sumsq_ref, *, bm, seq, alpha):
    i = pl.program_id(0)
    m = mod_ref[0]
    y = (yf_ref[...] + yb_ref[...]) * _silu(z_ref[...].astype(F32))
    y_ssd = (_group_rms(y, r4_ref, e4_ref, SSD_WIDTH // SSD_GROUPS) * ssdg_ref[...]).astype(BF16)
    n_cols = [slice(c, c + POST_COL_PIECE) for c in range(0, wout_ref.shape[1], POST_COL_PIECE)]
    mix_ssd = [_dot(y_ssd, wout_ref[0:SSD_WIDTH, cs]) for cs in n_cols]

    f32 = lambda ref, rows=slice(None): ref[rows, :].astype(F32)
    u = f32(gc_ref) * f32(v_ref)
    has_prev = ((i * bm) % seq != 0).astype(F32)
    has_next = (((i + 1) * bm) % seq != 0).astype(F32)
    last = slice(HALO_ROWS - 1, HALO_ROWS)
    first = slice(0, 1)
    up, un = _shifted_rows(u, f32(gcp_ref, last) * f32(vp_ref, last) * has_prev,
                           f32(gcn_ref, first) * f32(vn_ref, first) * has_next)
    w = scw_ref[...]
    y_sc = f32(gb_ref) * (up * w[0:1] + u * w[1:2] + un * w[2:3])
    y_sc = (_group_rms(y_sc, r32_ref, e32_ref, SC_WIDTH // SC_GROUPS) * scg_ref[...]).astype(BF16)
    mix = jnp.concatenate([part + _dot(y_sc, wout_ref[SSD_WIDTH:SSD_WIDTH + SC_WIDTH, cs])
                           for part, cs in zip(mix_ssd, n_cols)], axis=1)
    x1 = _layer_norm(alpha * x_ref[...] + m[2:3] * mix, lng_ref[...], lnb_ref[...])
    x1_ref[...] = x1
    h2 = x1 * (1.0 + m[4:5]) + m[3:4]
    h2_ref[...] = h2.astype(BF16)
    amax_ref[...] = jnp.max(jnp.abs(h2), axis=-1, keepdims=True)
    sumsq_ref[...] = jnp.sum(h2 * h2, axis=-1, keepdims=True)


def _single(shape, index_map):
    return pl.BlockSpec(shape, index_map, pipeline_mode=pl.Buffered(1))


def _post(yf, yb, proj, x2, mod3, scw, ssdg, scg, r4, e4, r32, e32, wout, lng, lnb, seq, alpha):
    t, d = x2.shape
    bm = min(256, seq)
    per_batch = seq // bm
    per = bm // HALO_ROWS
    last = t // HALO_ROWS - 1
    main = lambda c: pl.BlockSpec((bm, d), lambda i: (i, c))
    prev = lambda c: pl.BlockSpec((HALO_ROWS, d), lambda i: (jnp.maximum(i * per - 1, 0), c))
    nxt = lambda c: pl.BlockSpec((HALO_ROWS, d), lambda i: (jnp.minimum((i + 1) * per, last), c))
    const = lambda shape: _single(shape, lambda i: (0, 0))
    return pl.pallas_call(
        functools.partial(_post_kernel, bm=bm, seq=seq, alpha=alpha),
        grid=(t // bm,),
        in_specs=[main(0), main(0), main(0), main(1), main(2), prev(2), nxt(2), main(3), prev(3), nxt(3),
                  main(0), pl.BlockSpec((1, 6, d), lambda i: (i // per_batch, 0, 0)),
                  const((3, d)), const((1, d)), const((1, d)),
                  const((d, LANES)), const((LANES, d)), const((d, LANES)), const((LANES, d)),
                  const((2 * d, d)), const((1, d)), const((1, d))],
        out_specs=[pl.BlockSpec((bm, d), lambda i: (i, 0)), pl.BlockSpec((bm, d), lambda i: (i, 0)),
                   pl.BlockSpec((bm, 1), lambda i: (i, 0)), pl.BlockSpec((bm, 1), lambda i: (i, 0))],
        out_shape=[jax.ShapeDtypeStruct((t, d), F32), jax.ShapeDtypeStruct((t, d), BF16),
                   jax.ShapeDtypeStruct((t, 1), F32), jax.ShapeDtypeStruct((t, 1), F32)],
        compiler_params=_params(("parallel",)),
        name="post",
    )(yf, yb, proj, proj, proj, proj, proj, proj, proj, proj, x2, mod3,
      scw, ssdg, scg, r4, e4, r32, e32, wout, lng, lnb)


def _top_values(s, n, want_rank=False):
    vals = []
    rank = jnp.full(s.shape, float(n), F32) if want_rank else None
    for r in range(n):
        m = jnp.max(s, axis=0, keepdims=True)
        vals.append(m)
        hit = s == m
        if want_rank:
            rank = jnp.where(hit, float(r), rank)
        s = jnp.where(hit, -jnp.inf, s)
    return (vals, rank) if want_rank else vals


def _query_kernel(h2_ref, wq_ref, k_ref, r2_ref, e2_ref, n1_ref, c1_ref):
    tb = h2_ref.shape[0]
    for h in range(PEER_HEADS):
        q = _dot(h2_ref[...], wq_ref[:, h * D_QUERY:(h + 1) * D_QUERY]).astype(BF16)
        s1 = _dot_nt(k_ref[h, 0], q[:, :HALF_QUERY])
        s2 = _dot_nt(k_ref[h, 1], q[:, HALF_QUERY:])
        v1 = _top_values(s1, PEER_TOPK)
        v2, r2 = _top_values(s2, PEER_TOPK, want_rank=True)
        pairs = [(a, b) for a in range(PEER_TOPK) for b in range(PEER_TOPK) if (a + 1) * (b + 1) <= PEER_TOPK]
        sums = {ab: v1[ab[0]] + v2[ab[1]] for ab in pairs}
        rows = [sums[ab] for ab in pairs]
        rows += [jnp.full((1, tb), -jnp.inf, F32)] * ((-len(rows)) % SUBLANES)
        top = _top_values(jnp.concatenate(rows, axis=0), PEER_TOPK)
        z = jnp.ones((1, tb), F32)
        for kk in range(1, PEER_TOPK):
            z = z + jnp.exp(top[kk] - top[0])
        tau = top[PEER_TOPK - 1]
        n1 = jnp.zeros(s1.shape, F32)
        for a in range(PEER_TOPK):
            cnt = sum(jnp.where(sums[(a, b)] >= tau, 1.0, 0.0) for b in range(PEER_TOPK) if (a, b) in sums)
            n1 = jnp.where(s1 == v1[a], cnt, n1)
        r2_ref[h] = r2.astype(BF16)
        e2_ref[h] = jnp.exp(s2 - v2[0]).astype(BF16)
        n1_ref[h] = n1
        c1_ref[h] = jnp.exp(s1 - v1[0]) / z


def _query(h2, wq, keys):
    t, d = h2.shape
    tb = min(256, t)
    ospec = pl.BlockSpec((PEER_HEADS, N_KEYS, tb), lambda i: (0, 0, i))
    return pl.pallas_call(
        _query_kernel,
        grid=(t // tb,),
        in_specs=[pl.BlockSpec((tb, d), lambda i: (i, 0)),
                  _single((d, PEER_HEADS * D_QUERY), lambda i: (0, 0)),
                  _single((PEER_HEADS, 2, N_KEYS, HALF_QUERY), lambda i: (0, 0, 0, 0))],
        out_specs=[ospec] * 4,
        out_shape=[jax.ShapeDtypeStruct((PEER_HEADS, N_KEYS, t), dt) for dt in (BF16, BF16, F32, F32)],
        compiler_params=_params(("parallel",)),
        name="peerquery",
    )(h2, wq, keys)


def _pow2_scale(bound):
    ratio = FP8_TARGET / jnp.where(bound > 0, bound, FP8_TARGET)
    exponent_only = lax.bitcast_convert_type(ratio, jnp.int32) & jnp.int32(0x7F800000)
    return jnp.minimum(lax.bitcast_convert_type(exponent_only, F32), FP8_SCALE_CAP)


def _expert_prep_kernel(w_ref, q_ref, inv_ref, sumsq_ref, *, transpose):
    w = w_ref[...]
    s = _pow2_scale(jnp.max(jnp.abs(w), axis=1, keepdims=True))
    inv_ref[...] = 1.0 / s
    sumsq_ref[...] = jnp.sum(w * w, axis=1, keepdims=True)
    ws = w * s
    q_ref[...] = (ws.T if transpose else ws).astype(FP8)


def _expert_prep(table, transpose):
    n_exp, d = table.shape
    rows = 512
    col = pl.BlockSpec((rows, 1), lambda i: (i, 0))
    q_spec = pl.BlockSpec((d, rows), lambda i: (0, i)) if transpose else pl.BlockSpec((rows, d), lambda i: (i, 0))
    q_shape = (d, n_exp) if transpose else (n_exp, d)
    return pl.pallas_call(
        functools.partial(_expert_prep_kernel, transpose=transpose),
        grid=(n_exp // rows,),
        in_specs=[pl.BlockSpec((rows, d), lambda i: (i, 0))],
        out_specs=[q_spec, col, col],
        out_shape=[jax.ShapeDtypeStruct(q_shape, FP8), jax.ShapeDtypeStruct((n_exp, 1), F32),
                   jax.ShapeDtypeStruct((n_exp, 1), F32)],
        compiler_params=_params(("parallel",)),
        name="expertprep_t" if transpose else "expertprep",
    )(table)


def _peer_kernel(h2_ref, u_ref, vt_ref, r2_ref, e2_ref, n1_ref, c1_ref, x1_ref, mod_ref, lng_ref, lnb_ref,
                 iu_ref, iv_ref, sh_ref, ish_ref, sp_ref, isp_ref, o_ref, acc_ref, h8_ref, *, alpha):
    e = pl.program_id(1)

    @pl.when(e == 0)
    def _():
        acc_ref[...] = jnp.zeros_like(acc_ref)
        h8_ref[...] = (h2_ref[...].astype(F32) * sh_ref[...]).T.astype(FP8)

    parts = []
    for ii in range(u_ref.shape[0] // N_KEYS):
        rows = slice(ii * N_KEYS, (ii + 1) * N_KEYS)
        at = _dot(u_ref[rows, :], h8_ref[...])
        g = None
        for h in range(PEER_HEADS):
            n_row = n1_ref[h, ii:ii + 1, :].astype(BF16)
            c_row = c1_ref[h, ii:ii + 1, :].astype(BF16)
            w = jnp.where(r2_ref[h] < n_row, e2_ref[h], jnp.zeros((), BF16)) * c_row
            g = w if g is None else g + w
        b = at * (iu_ref[rows, :] * ish_ref[...])
        p = b * (1.0 + lax.erf(b)) * (iv_ref[rows, :] * sp_ref[...])
        parts.append((p.astype(BF16) * g).astype(FP8))
    acc_ref[...] += _dot(vt_ref[...], jnp.concatenate(parts, axis=0))

    @pl.when(e == pl.num_programs(1) - 1)
    def _():
        m = mod_ref[0]
        ffn = acc_ref[...].T * isp_ref[...]
        o_ref[...] = _layer_norm(alpha * x1_ref[...] + m[5:6] * ffn, lng_ref[...], lnb_ref[...])


def _peer(h2, u8, vt8, r2, e2, n1, c1, x1, mod3, lng, lnb, iu, iv, sh, ish, sp, isp, seq, alpha):
    t, d = h2.shape
    n_exp = u8.shape[0]
    tb = min(512, seq)
    eb = 1024
    per_batch = seq // tb
    kb = eb // N_KEYS
    full = _single((PEER_HEADS, N_KEYS, tb), lambda i, j: (0, 0, i))
    part = pl.BlockSpec((PEER_HEADS, kb, tb), lambda i, j: (0, j, i))
    ecol = pl.BlockSpec((eb, 1), lambda i, j: (j, 0))
    tcol = _single((tb, 1), lambda i, j: (i, 0))
    trow = _single((1, tb), lambda i, j: (0, i))
    return pl.pallas_call(
        functools.partial(_peer_kernel, alpha=alpha),
        grid=(t // tb, n_exp // eb),
        in_specs=[_single((tb, d), lambda i, j: (i, 0)),
                  pl.BlockSpec((eb, d), lambda i, j: (j, 0)),
                  pl.BlockSpec((d, eb), lambda i, j: (0, j)),
                  full, full, part, part,
                  _single((tb, d), lambda i, j: (i, 0)),
                  pl.BlockSpec((1, 6, d), lambda i, j: (i // per_batch, 0, 0)),
                  pl.BlockSpec((1, d), lambda i, j: (0, 0)),
                  pl.BlockSpec((1, d), lambda i, j: (0, 0)),
                  ecol, ecol, tcol, trow, trow, tcol],
        out_specs=pl.BlockSpec((tb, d), lambda i, j: (i, 0)),
        out_shape=jax.ShapeDtypeStruct((t, d), F32),
        scratch_shapes=[pltpu.VMEM((d, tb), F32), pltpu.VMEM((d, tb), FP8)],
        compiler_params=_params(("parallel", "arbitrary")),
        name="peer",
    )(h2, u8, vt8, r2, e2, n1, c1, x1, mod3, lng, lnb, iu, iv, sh, ish, sp, isp)


def _ffn(h2, h_amax, h_sumsq, x1, mod3, w_query, sub_keys, expert_u, expert_v, ln2_g, ln2_b, seq, alpha):
    t = h2.shape[0]
    r2, e2, n1, c1 = _query(h2, w_query.astype(BF16), sub_keys.astype(BF16))
    u8, inv_su, u_sumsq = _expert_prep(expert_u, transpose=False)
    vt8, inv_sv, _ = _expert_prep(expert_v, transpose=True)
    sh = _pow2_scale(h_amax)
    sp = _pow2_scale(jnp.sqrt(jnp.max(u_sumsq) * h_sumsq) * (PEER_HEADS * jnp.max(inv_sv)))
    sqrt_half = np.float32(math.sqrt(0.5))
    return _peer(h2, u8, vt8, r2, e2, n1, c1, x1, mod3, _row(ln2_g), _row(ln2_b), inv_su * sqrt_half,
                 inv_sv * sqrt_half, sh, (1.0 / sh).reshape(1, t), sp.reshape(1, t), 1.0 / sp, seq, alpha)


def _one_hot_cols(n_rows, n_cols, group_size, row_offset=0):
    r = np.arange(n_rows)[:, None]
    c = np.arange(n_cols)[None, :]
    return jnp.asarray(r == row_offset + c // group_size, dtype=BF16)


def _row(v):
    return v.reshape(1, -1).astype(F32)


def _pad_lanes(v):
    return jnp.pad(v, (0, LANES - v.shape[0])).reshape(1, LANES).astype(F32)


def kernel(x, c, w_ada, b_ada, w_in, conv_ssd_w, conv_ssd_b, dt_bias_f, dt_bias_b, a_log_f, a_log_b, d_skip, ssd_norm_g, short_conv_w, sc_norm_g, w_out, ln1_g, ln1_b, w_query, sub_keys, expert_u, expert_v, ln2_g, ln2_b):
    batch, seq, d = x.shape
    depth = w_ada.shape[0]
    alpha = (2.0 * depth) ** 0.25
    t = batch * seq
    x2 = x.reshape(t, d)
    c_pad = jnp.pad(c, ((0, SUBLANES - batch % SUBLANES), (0, 0))) if batch % SUBLANES else c

    e_f = _one_hot_cols(LANES, SSD_WIDTH, SSD_HEADDIM, 0)
    e_b = _one_hot_cols(LANES, SSD_WIDTH, SSD_HEADDIM, SSD_HEADS)
    e4 = _one_hot_cols(LANES, SSD_WIDTH, SSD_WIDTH // SSD_GROUPS)
    e32 = _one_hot_cols(LANES, SC_WIDTH, SC_WIDTH // SC_GROUPS)
    r4 = e4.T
    r32 = e32.T

    for i in range(depth):
        mod = _ada(c_pad, w_ada[i], b_ada[i].reshape(1, -1))
        mod3 = mod[:batch].reshape(batch, 6, d)

        w_main, wd_hi, wd_lo = _win_prep(w_in[i])
        proj, dt_raw = _inproj(x2, mod3, w_main, wd_hi, wd_lo, seq)

        xbc = _conv(proj, conv_ssd_w[i], conv_ssd_b[i].reshape(1, -1), seq)
        bias = _pad_lanes(jnp.concatenate([dt_bias_f[i], dt_bias_b[i]]))
        alog = _pad_lanes(jnp.concatenate([a_log_f[i], a_log_b[i]]))
        dskip_x = _row(jnp.repeat(d_skip[i], SSD_HEADDIM))
        y_f, y_b = _ssd(xbc, dt_raw, bias, alog, dskip_x, e_f, e_b, batch, seq)

        x1, h2, h_amax, h_sumsq = _post(y_f, y_b, proj, x2, mod3, short_conv_w[i], _row(ssd_norm_g[i]),
                                        _row(sc_norm_g[i]), r4, e4, r32, e32, w_out[i].astype(BF16),
                                        _row(ln1_g[i]), _row(ln1_b[i]), seq, alpha)

        x2 = _ffn(h2, h_amax, h_sumsq, x1, mod3, w_query[i], sub_keys[i], expert_u[i], expert_v[i],
                  ln2_g[i], ln2_b[i], seq, alpha)
    return x2.reshape(batch, seq, d)
```

```python
import functools
import math

import jax
import jax.numpy as jnp
import numpy as np
from jax import lax
from jax.experimental import pallas as pl
from jax.experimental.pallas import tpu as pltpu

F32 = jnp.float32
BF16 = jnp.bfloat16
FP8 = jnp.float8_e4m3fn
FP8_TARGET = 256.0
FP8_SCALE_CAP = 2.0 ** 60

D_MODEL = 2048
SSD_WIDTH = D_MODEL
SSD_HEADDIM = 64
SSD_HEADS = SSD_WIDTH // SSD_HEADDIM
SSD_GROUPS = 4
SSD_STATE = 128
SSD_CHUNK = 128
HEADS_PER_GROUP = SSD_HEADS // SSD_GROUPS
GROUP_WIDTH = HEADS_PER_GROUP * SSD_HEADDIM
SC_WIDTH = D_MODEL
SC_GROUPS = 32
XBC_WIDTH = SSD_WIDTH + 2 * SSD_GROUPS * SSD_STATE
MAIN_COLS = 4 * D_MODEL + XBC_WIDTH
XBC_COL0 = 4 * D_MODEL
PEER_HEADS = 8
N_KEYS = 128
PEER_TOPK = 16
D_QUERY = 512
HALF_QUERY = D_QUERY // 2
NORM_EPS = 1e-5
LANES = 128
SUBLANES = 8
HALO_ROWS = 16
POST_COL_PIECE = 256
PEER_TOKEN_BLOCK = 512
PEER_OUT_PIECE = 256
VMEM_LIMIT = 56 * 1024 * 1024


def _dot(a, b):
    return jnp.dot(a, b, preferred_element_type=F32)


def _dot_nt(a, b):
    return lax.dot_general(a, b, (((1,), (1,)), ((), ())), preferred_element_type=F32)


def _split2(x):
    hi = x.astype(BF16)
    lo = (x - hi.astype(F32)).astype(BF16)
    return hi, lo


def _split3(x):
    hi = x.astype(BF16)
    r = x - hi.astype(F32)
    mid = r.astype(BF16)
    lo = (r - mid.astype(F32)).astype(BF16)
    return hi, mid, lo


def _softplus(x):
    return jnp.maximum(x, 0.0) + jnp.log1p(jnp.exp(-jnp.abs(x)))


def _silu(x):
    return x * (1.0 / (1.0 + jnp.exp(-x)))


def _params(sem):
    return pltpu.CompilerParams(dimension_semantics=sem, vmem_limit_bytes=VMEM_LIMIT)


def _ada_kernel(c_ref, w_ref, b_ref, o_ref):
    sc = _silu(c_ref[...])
    o_ref[...] = _dot(sc.astype(BF16), w_ref[...].astype(BF16)) + b_ref[...]


def _ada(c_pad, w, b):
    rows, d = c_pad.shape
    n = w.shape[1]
    bn = 1024
    return pl.pallas_call(
        _ada_kernel,
        grid=(n // bn,),
        in_specs=[pl.BlockSpec((rows, d), lambda j: (0, 0)),
                  pl.BlockSpec((d, bn), lambda j: (0, j)),
                  pl.BlockSpec((1, bn), lambda j: (0, j))],
        out_specs=pl.BlockSpec((rows, bn), lambda j: (0, j)),
        out_shape=jax.ShapeDtypeStruct((rows, n), F32),
        compiler_params=_params(("arbitrary",)),
        name="ada",
    )(c_pad, w, b)


DT_COL0 = SSD_WIDTH + XBC_WIDTH
GATES_COL0 = DT_COL0 + 2 * SSD_HEADS


def _win_prep_kernel(w_ref, main_ref, dh_ref, dl_ref):
    w = w_ref[...]
    main_ref[:, 0:SSD_WIDTH] = w[:, 0:SSD_WIDTH].astype(BF16)
    main_ref[:, SSD_WIDTH:XBC_COL0] = w[:, GATES_COL0:GATES_COL0 + 3 * SC_WIDTH].astype(BF16)
    main_ref[:, XBC_COL0:MAIN_COLS] = w[:, SSD_WIDTH:DT_COL0].astype(BF16)
    dt = w[:, DT_COL0:DT_COL0 + LANES]
    lane = lax.broadcasted_iota(jnp.int32, dt.shape, 1)
    hi, lo = _split2(jnp.where(lane < 2 * SSD_HEADS, dt, 0.0))
    dh_ref[...] = hi
    dl_ref[...] = lo


def _win_prep(w):
    d, n = w.shape
    rows = 128
    return pl.pallas_call(
        _win_prep_kernel,
        grid=(d // rows,),
        in_specs=[pl.BlockSpec((rows, n), lambda i: (i, 0))],
        out_specs=[pl.BlockSpec((rows, MAIN_COLS), lambda i: (i, 0)),
                   pl.BlockSpec((rows, LANES), lambda i: (i, 0)),
                   pl.BlockSpec((rows, LANES), lambda i: (i, 0))],
        out_shape=[jax.ShapeDtypeStruct((d, MAIN_COLS), BF16), jax.ShapeDtypeStruct((d, LANES), BF16),
                   jax.ShapeDtypeStruct((d, LANES), BF16)],
        compiler_params=_params(("parallel",)),
        name="winprep",
    )(w)


def _inproj_kernel(x_ref, mod_ref, w_ref, wdh_ref, wdl_ref, o_ref, dt_ref, h_scr):
    @pl.when(pl.program_id(1) == 0)
    def _():
        m = mod_ref[0]
        h = x_ref[...] * (1.0 + m[1:2]) + m[0:1]
        hi, lo = _split2(h)
        h_scr[...] = hi
        dt_ref[...] = _dot(hi, wdh_ref[...]) + _dot(lo, wdh_ref[...]) + _dot(hi, wdl_ref[...])

    o_ref[...] = _dot(h_scr[...], w_ref[...]).astype(o_ref.dtype)


def _inproj(x2, mod3, w_main, wd_hi, wd_lo, seq):
    t, d = x2.shape
    n = w_main.shape[1]
    bm = min(1024, seq)
    bn = 1024
    per_batch = seq // bm
    return pl.pallas_call(
        _inproj_kernel,
        grid=(t // bm, n // bn),
        in_specs=[pl.BlockSpec((bm, d), lambda i, j: (i, 0)),
                  pl.BlockSpec((1, 6, d), lambda i, j: (i // per_batch, 0, 0)),
                  pl.BlockSpec((d, bn), lambda i, j: (0, j)),
                  pl.BlockSpec((d, LANES), lambda i, j: (0, 0)),
                  pl.BlockSpec((d, LANES), lambda i, j: (0, 0))],
        out_specs=[pl.BlockSpec((bm, bn), lambda i, j: (i, j)),
                   pl.BlockSpec((bm, LANES), lambda i, j: (i, 0))],
        out_shape=[jax.ShapeDtypeStruct((t, n), BF16),
                   jax.ShapeDtypeStruct((t, LANES), F32)],
        scratch_shapes=[pltpu.VMEM((bm, d), BF16)],
        compiler_params=_params(("parallel", "arbitrary")),
        name="inproj",
    )(x2, mod3, w_main, wd_hi, wd_lo)


def _shifted_rows(u, prev_row, next_row):
    rows = u.shape[0]
    ridx = lax.broadcasted_iota(jnp.int32, u.shape, 0)
    up = jnp.where(ridx == 0, prev_row, pltpu.roll(u, 1, axis=0))
    un = jnp.where(ridx == rows - 1, next_row, pltpu.roll(u, rows - 1, axis=0))
    return up, un


def _halo_specs(bm, bw, col_of, t):
    per = bm // HALO_ROWS
    last = t // HALO_ROWS - 1
    prev = pl.BlockSpec((HALO_ROWS, bw), lambda i, j: (jnp.maximum(i * per - 1, 0), col_of(j)))
    nxt = pl.BlockSpec((HALO_ROWS, bw), lambda i, j: (jnp.minimum((i + 1) * per, last), col_of(j)))
    return prev, nxt


def _conv_kernel(u_ref, p_ref, n_ref, w_ref, b_ref, o_ref, *, bm, seq):
    i = pl.program_id(0)
    u = u_ref[...].astype(F32)
    has_prev = ((i * bm) % seq != 0).astype(F32)
    has_next = (((i + 1) * bm) % seq != 0).astype(F32)
    up, un = _shifted_rows(u, p_ref[HALO_ROWS - 1:HALO_ROWS, :].astype(F32) * has_prev,
                           n_ref[0:1, :].astype(F32) * has_next)
    w = w_ref[...]
    o_ref[...] = _silu(up * w[0:1] + u * w[1:2] + un * w[2:3] + b_ref[...])


def _conv(proj, conv_w, conv_b, seq):
    t = proj.shape[0]
    bm = min(512, seq)
    bw = 1024
    col0 = XBC_COL0 // bw
    col_of = lambda j: col0 + j
    prev, nxt = _halo_specs(bm, bw, col_of, t)
    return pl.pallas_call(
        functools.partial(_conv_kernel, bm=bm, seq=seq),
        grid=(t // bm, XBC_WIDTH // bw),
        in_specs=[pl.BlockSpec((bm, bw), lambda i, j: (i, col_of(j))), prev, nxt,
                  pl.BlockSpec((3, bw), lambda i, j: (0, j)),
                  pl.BlockSpec((1, bw), lambda i, j: (0, j))],
        out_specs=pl.BlockSpec((bm, bw), lambda i, j: (i, j)),
        out_shape=jax.ShapeDtypeStruct((t, XBC_WIDTH), F32),
        compiler_params=_params(("parallel", "parallel")),
        name="ssdconv",
    )(proj, proj, proj, conv_w, conv_b)


def _ssd_direction(x_ref, b_ref, c_ref, dt_ref, e_ref, h_scr, y_ref, bias, a, dskip, backward):
    cs_len = SSD_CHUNK
    row = lax.broadcasted_iota(jnp.int32, (cs_len, cs_len), 0)
    col = lax.broadcasted_iota(jnp.int32, (cs_len, cs_len), 1)
    tri = (col <= row).astype(BF16)
    dt = _softplus(dt_ref[...] + bias)
    adt = dt * a
    h3 = _split3(adt)
    cs = _dot(tri, h3[0]) + _dot(tri, h3[1]) + _dot(tri, h3[2])
    tot = cs[cs_len - 1:cs_len, :]
    if backward:
        ecs = cs - adt
        p = -ecs
        wst = dt * jnp.exp(ecs)
        indec = jnp.exp(tot - ecs)
        mask = col >= row
    else:
        p = cs
        wst = dt * jnp.exp(tot - cs)
        indec = jnp.exp(cs)
        mask = col <= row
    pt = p.T
    e = e_ref[...]
    dtx = _dot(dt.astype(BF16), e)
    wstx = _dot(wst.astype(BF16), e)
    indx = _dot(indec.astype(BF16), e)
    xs = x_ref[...]
    xd = (xs * dtx).astype(BF16)
    xw = (xs * wstx).astype(BF16)
    col_off = SSD_HEADS if backward else 0
    pair = 2 * SSD_HEADDIM
    first_of_pair = lax.broadcasted_iota(jnp.int32, (cs_len, pair), 1) < SSD_HEADDIM
    zero = jnp.zeros((), BF16)
    for g in range(SSD_GROUPS):
        gs = slice(g * GROUP_WIDTH, (g + 1) * GROUP_WIDTH)
        bg = b_ref[:, g * SSD_STATE:(g + 1) * SSD_STATE]
        cg = c_ref[:, g * SSD_STATE:(g + 1) * SSD_STATE].astype(BF16)
        cb = _dot_nt(cg, bg.astype(BF16))
        h_in = h_scr[g]
        y_off = _dot(cg, h_in.astype(BF16)) * indx[:, gs]
        st = _dot(bg.T.astype(BF16), xw[:, gs])
        for r in range(0, HEADS_PER_GROUP, 2):
            hd = g * HEADS_PER_GROUP + r
            ms = []
            for ci in (hd + col_off, hd + col_off + 1):
                seg = p[:, ci:ci + 1] - pt[ci:ci + 1, :]
                lmat = jnp.exp(jnp.where(mask, seg, -jnp.inf))
                ms.append((cb * lmat).astype(BF16))
            hs = slice(hd * SSD_HEADDIM, (hd + 2) * SSD_HEADDIM)
            x2h = xd[:, hs]
            rhs = jnp.concatenate([jnp.where(first_of_pair, x2h, zero), jnp.where(first_of_pair, zero, x2h)], axis=0)
            y = _dot(jnp.concatenate(ms, axis=1), rhs) + y_off[:, r * SSD_HEADDIM:(r + 2) * SSD_HEADDIM]
            if dskip is not None:
                y = y + xs[:, hs] * dskip[:, hs]
            y_ref[:, hs] = y
        edge = 0 if backward else cs_len - 1
        h_scr[g] = indx[edge:edge + 1, gs] * h_in + st


def _ssd_kernel(xf_ref, bf_ref, cf_ref, dtf_ref, xb_ref, bb_ref, cb_ref, dtb_ref,
                bias_ref, alog_ref, dskip_ref, ef_ref, eb_ref, yf_ref, yb_ref, hf_scr, hb_scr):
    @pl.when(pl.program_id(1) == 0)
    def _():
        hf_scr[...] = jnp.zeros_like(hf_scr)
        hb_scr[...] = jnp.zeros_like(hb_scr)

    bias = bias_ref[...]
    a = -jnp.exp(alog_ref[...])
    _ssd_direction(xf_ref, bf_ref, cf_ref, dtf_ref, ef_ref, hf_scr, yf_ref, bias, a, dskip_ref[...], False)
    _ssd_direction(xb_ref, bb_ref, cb_ref, dtb_ref, eb_ref, hb_scr, yb_ref, bias, a, None, True)


def _ssd(xbc, dt_raw, bias, alog, dskip_x, e_f, e_b, batch, seq):
    t = xbc.shape[0]
    nc = seq // SSD_CHUNK
    cl = SSD_CHUNK
    gn = SSD_GROUPS * SSD_STATE
    bcol = SSD_WIDTH // gn
    fwd = lambda b, k: b * nc + k
    bwd = lambda b, k: b * nc + (nc - 1 - k)

    def specs(ch):
        return [pl.BlockSpec((cl, SSD_WIDTH), lambda b, k: (ch(b, k), 0)),
                pl.BlockSpec((cl, gn), lambda b, k: (ch(b, k), bcol)),
                pl.BlockSpec((cl, gn), lambda b, k: (ch(b, k), bcol + 1)),
                pl.BlockSpec((cl, LANES), lambda b, k: (ch(b, k), 0))]

    const = lambda shape: pl.BlockSpec(shape, lambda b, k: (0, 0))
    state = pltpu.VMEM((SSD_GROUPS, SSD_STATE, GROUP_WIDTH), F32)
    return pl.pallas_call(
        _ssd_kernel,
        grid=(batch, nc),
        in_specs=specs(fwd) + specs(bwd) + [const((1, LANES)), const((1, LANES)), const((1, SSD_WIDTH)),
                                            const((LANES, SSD_WIDTH)), const((LANES, SSD_WIDTH))],
        out_specs=[pl.BlockSpec((cl, SSD_WIDTH), lambda b, k: (fwd(b, k), 0)),
                   pl.BlockSpec((cl, SSD_WIDTH), lambda b, k: (bwd(b, k), 0))],
        out_shape=[jax.ShapeDtypeStruct((t, SSD_WIDTH), F32)] * 2,
        scratch_shapes=[state, state],
        compiler_params=_params(("parallel", "arbitrary")),
        name="ssd",
    )(xbc, xbc, xbc, dt_raw, xbc, xbc, xbc, dt_raw, bias, alog, dskip_x, e_f, e_b)


def _group_rms(y, r_ref, e_ref, group_size):
    gsum = _dot((y * y).astype(BF16), r_ref[...])
    inv = lax.rsqrt(gsum * (1.0 / group_size) + NORM_EPS)
    return y * _dot(inv.astype(BF16), e_ref[...])


def _layer_norm(v, g, b):
    mu = jnp.mean(v, axis=-1, keepdims=True)
    vc = v - mu
    var = jnp.mean(vc * vc, axis=-1, keepdims=True)
    return vc * lax.rsqrt(var + NORM_EPS) * g + b


def _post_kernel(yf_ref, yb_ref, z_ref, gb_ref, gc_ref, gcp_ref, gcn_ref, v_ref, vp_ref, vn_ref,
                 x_ref, mod_ref, scw_ref, ssdg_ref, scg_ref, r4_ref, e4_ref, r32_ref, e32_ref,
                 wout_ref, lng_ref, lnb_ref, x1_ref, h2_ref, amax_ref, sumsq_ref, *, bm, seq, alpha):
    i = pl.program_id(0)
    m = mod_ref[0]
    y = (yf_ref[...] + yb_ref[...]) * _silu(z_ref[...].astype(F32))
    y_ssd = (_group_rms(y, r4_ref, e4_ref, SSD_WIDTH // SSD_GROUPS) * ssdg_ref[...]).astype(BF16)
    n_cols = [slice(c, c + POST_COL_PIECE) for c in range(0, wout_ref.shape[1], POST_COL_PIECE)]
    mix_ssd = [_dot(y_ssd, wout_ref[0:SSD_WIDTH, cs]) for cs in n_cols]

    f32 = lambda ref, rows=slice(None): ref[rows, :].astype(F32)
    u = f32(gc_ref) * f32(v_ref)
    has_prev = ((i * bm) % seq != 0).astype(F32)
    has_next = (((i + 1) * bm) % seq != 0).astype(F32)
    last = slice(HALO_ROWS - 1, HALO_ROWS)
    first = slice(0, 1)
    up, un = _shifted_rows(u, f32(gcp_ref, last) * f32(vp_ref, last) * has_prev,
                           f32(gcn_ref, first) * f32(vn_ref, first) * has_next)
    w = scw_ref[...]
    y_sc = f32(gb_ref) * (up * w[0:1] + u * w[1:2] + un * w[2:3])
    y_sc = (_group_rms(y_sc, r32_ref, e32_ref, SC_WIDTH // SC_GROUPS) * scg_ref[...]).astype(BF16)
    mix = jnp.concatenate([part + _dot(y_sc, wout_ref[SSD_WIDTH:SSD_WIDTH + SC_WIDTH, cs])
                           for part, cs in zip(mix_ssd, n_cols)], axis=1)
    x1 = _layer_norm(alpha * x_ref[...] + m[2:3] * mix, lng_ref[...], lnb_ref[...])
    x1_ref[...] = x1
    h2 = x1 * (1.0 + m[4:5]) + m[3:4]
    h2_ref[...] = h2.astype(BF16)
    amax_ref[...] = jnp.max(jnp.abs(h2), axis=-1, keepdims=True)
    sumsq_ref[...] = jnp.sum(h2 * h2, axis=-1, keepdims=True)


def _single(shape, index_map):
    return pl.BlockSpec(shape, index_map, pipeline_mode=pl.Buffered(1))


def _post(yf, yb, proj, x2, mod3, scw, ssdg, scg, r4, e4, r32, e32, wout, lng, lnb, seq, alpha):
    t, d = x2.shape
    bm = min(256, seq)
    per_batch = seq // bm
    per = bm // HALO_ROWS
    last = t // HALO_ROWS - 1
    main = lambda c: pl.BlockSpec((bm, d), lambda i: (i, c))
    prev = lambda c: pl.BlockSpec((HALO_ROWS, d), lambda i: (jnp.maximum(i * per - 1, 0), c))
    nxt = lambda c: pl.BlockSpec((HALO_ROWS, d), lambda i: (jnp.minimum((i + 1) * per, last), c))
    const = lambda shape: _single(shape, lambda i: (0, 0))
    return pl.pallas_call(
        functools.partial(_post_kernel, bm=bm, seq=seq, alpha=alpha),
        grid=(t // bm,),
        in_specs=[main(0), main(0), main(0), main(1), main(2), prev(2), nxt(2), main(3), prev(3), nxt(3),
                  main(0), pl.BlockSpec((1, 6, d), lambda i: (i // per_batch, 0, 0)),
                  const((3, d)), const((1, d)), const((1, d)),
                  const((d, LANES)), const((LANES, d)), const((d, LANES)), const((LANES, d)),
                  const((2 * d, d)), const((1, d)), const((1, d))],
        out_specs=[pl.BlockSpec((bm, d), lambda i: (i, 0)), pl.BlockSpec((bm, d), lambda i: (i, 0)),
                   pl.BlockSpec((bm, 1), lambda i: (i, 0)), pl.BlockSpec((bm, 1), lambda i: (i, 0))],
        out_shape=[jax.ShapeDtypeStruct((t, d), F32), jax.ShapeDtypeStruct((t, d), BF16),
                   jax.ShapeDtypeStruct((t, 1), F32), jax.ShapeDtypeStruct((t, 1), F32)],
        compiler_params=_params(("parallel",)),
        name="post",
    )(yf, yb, proj, proj, proj, proj, proj, proj, proj, proj, x2, mod3,
      scw, ssdg, scg, r4, e4, r32, e32, wout, lng, lnb)


def _top_values(s, n, want_rank=False):
    vals = []
    rank = jnp.full(s.shape, float(n), F32) if want_rank else None
    for r in range(n):
        m = jnp.max(s, axis=0, keepdims=True)
        vals.append(m)
        hit = s == m
        if want_rank:
            rank = jnp.where(hit, float(r), rank)
        s = jnp.where(hit, -jnp.inf, s)
    return (vals, rank) if want_rank else vals


def _query_kernel(h2_ref, wq_ref, k_ref, r2_ref, e2_ref, n1_ref, c1_ref):
    tb = h2_ref.shape[0]
    for h in range(PEER_HEADS):
        q = _dot(h2_ref[...], wq_ref[:, h * D_QUERY:(h + 1) * D_QUERY]).astype(BF16)
        s1 = _dot_nt(k_ref[h, 0], q[:, :HALF_QUERY])
        s2 = _dot_nt(k_ref[h, 1], q[:, HALF_QUERY:])
        v1 = _top_values(s1, PEER_TOPK)
        v2, r2 = _top_values(s2, PEER_TOPK, want_rank=True)
        pairs = [(a, b) for a in range(PEER_TOPK) for b in range(PEER_TOPK) if (a + 1) * (b + 1) <= PEER_TOPK]
        sums = {ab: v1[ab[0]] + v2[ab[1]] for ab in pairs}
        rows = [sums[ab] for ab in pairs]
        rows += [jnp.full((1, tb), -jnp.inf, F32)] * ((-len(rows)) % SUBLANES)
        top = _top_values(jnp.concatenate(rows, axis=0), PEER_TOPK)
        z = jnp.ones((1, tb), F32)
        for kk in range(1, PEER_TOPK):
            z = z + jnp.exp(top[kk] - top[0])
        tau = top[PEER_TOPK - 1]
        n1 = jnp.zeros(s1.shape, F32)
        for a in range(PEER_TOPK):
            cnt = sum(jnp.where(sums[(a, b)] >= tau, 1.0, 0.0) for b in range(PEER_TOPK) if (a, b) in sums)
            n1 = jnp.where(s1 == v1[a], cnt, n1)
        r2_ref[h] = r2.astype(BF16)
        e2_ref[h] = jnp.exp(s2 - v2[0]).astype(BF16)
        n1_ref[h] = n1
        c1_ref[h] = jnp.exp(s1 - v1[0]) / z


def _query(h2, wq, keys):
    t, d = h2.shape
    tb = min(256, t)
    ospec = pl.BlockSpec((PEER_HEADS, N_KEYS, tb), lambda i: (0, 0, i))
    return pl.pallas_call(
        _query_kernel,
        grid=(t // tb,),
        in_specs=[pl.BlockSpec((tb, d), lambda i: (i, 0)),
                  _single((d, PEER_HEADS * D_QUERY), lambda i: (0, 0)),
                  _single((PEER_HEADS, 2, N_KEYS, HALF_QUERY), lambda i: (0, 0, 0, 0))],
        out_specs=[ospec] * 4,
        out_shape=[jax.ShapeDtypeStruct((PEER_HEADS, N_KEYS, t), dt) for dt in (BF16, BF16, F32, F32)],
        compiler_params=_params(("parallel",)),
        name="peerquery",
    )(h2, wq, keys)


def _pow2_scale(bound):
    ratio = FP8_TARGET / jnp.where(bound > 0, bound, FP8_TARGET)
    exponent_only = lax.bitcast_convert_type(ratio, jnp.int32) & jnp.int32(0x7F800000)
    return jnp.minimum(lax.bitcast_convert_type(exponent_only, F32), FP8_SCALE_CAP)


def _expert_prep_kernel(w_ref, q_ref, inv_ref, sumsq_ref, *, transpose):
    w = w_ref[...]
    s = _pow2_scale(jnp.max(jnp.abs(w), axis=1, keepdims=True))
    inv_ref[...] = 1.0 / s
    sumsq_ref[...] = jnp.sum(w * w, axis=1, keepdims=True)
    ws = w * s
    q_ref[...] = (ws.T if transpose else ws).astype(FP8)


def _expert_prep(table, transpose):
    n_exp, d = table.shape
    rows = 512
    col = pl.BlockSpec((rows, 1), lambda i: (i, 0))
    q_spec = pl.BlockSpec((d, rows), lambda i: (0, i)) if transpose else pl.BlockSpec((rows, d), lambda i: (i, 0))
    q_shape = (d, n_exp) if transpose else (n_exp, d)
    return pl.pallas_call(
        functools.partial(_expert_prep_kernel, transpose=transpose),
        grid=(n_exp // rows,),
        in_specs=[pl.BlockSpec((rows, d), lambda i: (i, 0))],
        out_specs=[q_spec, col, col],
        out_shape=[jax.ShapeDtypeStruct(q_shape, FP8), jax.ShapeDtypeStruct((n_exp, 1), F32),
                   jax.ShapeDtypeStruct((n_exp, 1), F32)],
        compiler_params=_params(("parallel",)),
        name="expertprep_t" if transpose else "expertprep",
    )(table)


def _peer_kernel(h2_ref, u_ref, vt_ref, r2_ref, e2_ref, n1_ref, c1_ref, x1_ref, mod_ref, lng_ref, lnb_ref,
                 acol_ref, pcol_ref, sh_ref, isp_ref, o_ref, acc_ref, h8_ref, *, alpha):
    e = pl.program_id(1)

    @pl.when(e == 0)
    def _():
        acc_ref[...] = jnp.zeros_like(acc_ref)
        h8_ref[...] = (h2_ref[...].astype(F32) * sh_ref[...]).T.astype(FP8)

    parts = []
    for ii in range(u_ref.shape[0] // N_KEYS):
        rows = slice(ii * N_KEYS, (ii + 1) * N_KEYS)
        at = _dot(u_ref[rows, :], h8_ref[...])
        g = None
        for h in range(PEER_HEADS):
            n_row = n1_ref[h, ii:ii + 1, :].astype(BF16)
            c_row = c1_ref[h, ii:ii + 1, :].astype(BF16)
            w = jnp.where(r2_ref[h] < n_row, e2_ref[h], jnp.zeros((), BF16)) * c_row
            g = w if g is None else g + w
        b = at * acol_ref[rows, :]
        p = b * (1.0 + lax.erf(b)) * pcol_ref[rows, :]
        parts.append((p.astype(BF16) * g).astype(FP8))
    pt = jnp.concatenate(parts, axis=0)
    for c in range(0, vt_ref.shape[0], PEER_OUT_PIECE):
        dr = slice(c, c + PEER_OUT_PIECE)
        acc_ref[dr, :] += _dot(vt_ref[dr, :], pt)

    @pl.when(e == pl.num_programs(1) - 1)
    def _():
        m = mod_ref[0]
        ffn = acc_ref[...].T * isp_ref[...]
        o_ref[...] = _layer_norm(alpha * x1_ref[...] + m[5:6] * ffn, lng_ref[...], lnb_ref[...])


def _peer(h2, u8, vt8, r2, e2, n1, c1, x1, mod3, lng, lnb, a_col, p_col, sh, isp, seq, alpha):
    t, d = h2.shape
    n_exp = u8.shape[0]
    tb = min(PEER_TOKEN_BLOCK, seq)
    eb = 1024
    per_batch = seq // tb
    kb = eb // N_KEYS
    full = _single((PEER_HEADS, N_KEYS, tb), lambda i, j: (0, 0, i))
    part = pl.BlockSpec((PEER_HEADS, kb, tb), lambda i, j: (0, j, i))
    ecol = pl.BlockSpec((None, eb, 1), lambda i, j: (i, j, 0))
    tcol = _single((tb, 1), lambda i, j: (i, 0))
    return pl.pallas_call(
        functools.partial(_peer_kernel, alpha=alpha),
        grid=(t // tb, n_exp // eb),
        in_specs=[_single((tb, d), lambda i, j: (i, 0)),
                  pl.BlockSpec((eb, d), lambda i, j: (j, 0)),
                  pl.BlockSpec((d, eb), lambda i, j: (0, j)),
                  full, full, part, part,
                  _single((tb, d), lambda i, j: (i, 0)),
                  pl.BlockSpec((1, 6, d), lambda i, j: (i // per_batch, 0, 0)),
                  pl.BlockSpec((1, d), lambda i, j: (0, 0)),
                  pl.BlockSpec((1, d), lambda i, j: (0, 0)),
                  ecol, ecol, tcol, tcol],
        out_specs=pl.BlockSpec((tb, d), lambda i, j: (i, 0)),
        out_shape=jax.ShapeDtypeStruct((t, d), F32),
        scratch_shapes=[pltpu.VMEM((d, tb), F32), pltpu.VMEM((d, tb), FP8)],
        compiler_params=_params(("parallel", "arbitrary")),
        name="peer",
    )(h2, u8, vt8, r2, e2, n1, c1, x1, mod3, lng, lnb, a_col, p_col, sh, isp)


def _ffn(h2, h_amax, h_sumsq, x1, mod3, w_query, sub_keys, expert_u, expert_v, ln2_g, ln2_b, seq, alpha):
    t = h2.shape[0]
    r2, e2, n1, c1 = _query(h2, w_query.astype(BF16), sub_keys.astype(BF16))
    u8, inv_su, u_sumsq = _expert_prep(expert_u, transpose=False)
    vt8, inv_sv, _ = _expert_prep(expert_v, transpose=True)
    tb = min(PEER_TOKEN_BLOCK, seq)
    per_block = lambda v: jnp.max(v.reshape(t // tb, tb), axis=1).reshape(-1, 1, 1)
    sh = _pow2_scale(per_block(h_amax))
    sp = _pow2_scale(jnp.sqrt(jnp.max(u_sumsq) * per_block(h_sumsq)) * (PEER_HEADS * jnp.max(inv_sv)))
    sqrt_half = np.float32(math.sqrt(0.5))
    a_col = (inv_su * sqrt_half)[None] / sh
    p_col = (inv_sv * sqrt_half)[None] * sp
    per_token = lambda s: jnp.broadcast_to(s, (t // tb, tb, 1)).reshape(t, 1)
    return _peer(h2, u8, vt8, r2, e2, n1, c1, x1, mod3, _row(ln2_g), _row(ln2_b), a_col, p_col,
                 per_token(sh), per_token(1.0 / sp), seq, alpha)


def _one_hot_cols(n_rows, n_cols, group_size, row_offset=0):
    r = np.arange(n_rows)[:, None]
    c = np.arange(n_cols)[None, :]
    return jnp.asarray(r == row_offset + c // group_size, dtype=BF16)


def _row(v):
    return v.reshape(1, -1).astype(F32)


def _pad_lanes(v):
    return jnp.pad(v, (0, LANES - v.shape[0])).reshape(1, LANES).astype(F32)


def kernel(x, c, w_ada, b_ada, w_in, conv_ssd_w, conv_ssd_b, dt_bias_f, dt_bias_b, a_log_f, a_log_b, d_skip, ssd_norm_g, short_conv_w, sc_norm_g, w_out, ln1_g, ln1_b, w_query, sub_keys, expert_u, expert_v, ln2_g, ln2_b):
    batch, seq, d = x.shape
    depth = w_ada.shape[0]
    alpha = (2.0 * depth) ** 0.25
    t = batch * seq
    x2 = x.reshape(t, d)
    c_pad = jnp.pad(c, ((0, SUBLANES - batch % SUBLANES), (0, 0))) if batch % SUBLANES else c

    e_f = _one_hot_cols(LANES, SSD_WIDTH, SSD_HEADDIM, 0)
    e_b = _one_hot_cols(LANES, SSD_WIDTH, SSD_HEADDIM, SSD_HEADS)
    e4 = _one_hot_cols(LANES, SSD_WIDTH, SSD_WIDTH // SSD_GROUPS)
    e32 = _one_hot_cols(LANES, SC_WIDTH, SC_WIDTH // SC_GROUPS)
    r4 = e4.T
    r32 = e32.T

    for i in range(depth):
        mod = _ada(c_pad, w_ada[i], b_ada[i].reshape(1, -1))
        mod3 = mod[:batch].reshape(batch, 6, d)

        w_main, wd_hi, wd_lo = _win_prep(w_in[i])
        proj, dt_raw = _inproj(x2, mod3, w_main, wd_hi, wd_lo, seq)

        xbc = _conv(proj, conv_ssd_w[i], conv_ssd_b[i].reshape(1, -1), seq)
        bias = _pad_lanes(jnp.concatenate([dt_bias_f[i], dt_bias_b[i]]))
        alog = _pad_lanes(jnp.concatenate([a_log_f[i], a_log_b[i]]))
        dskip_x = _row(jnp.repeat(d_skip[i], SSD_HEADDIM))
        y_f, y_b = _ssd(xbc, dt_raw, bias, alog, dskip_x, e_f, e_b, batch, seq)

        x1, h2, h_amax, h_sumsq = _post(y_f, y_b, proj, x2, mod3, short_conv_w[i], _row(ssd_norm_g[i]),
                                        _row(sc_norm_g[i]), r4, e4, r32, e32, w_out[i].astype(BF16),
                                        _row(ln1_g[i]), _row(ln1_b[i]), seq, alpha)

        x2 = _ffn(h2, h_amax, h_sumsq, x1, mod3, w_query[i], sub_keys[i], expert_u[i], expert_v[i],
                  ln2_g[i], ln2_b[i], seq, alpha)
    return x2.reshape(batch, seq, d)
```

```python
import functools
import math

import jax
import jax.numpy as jnp
import numpy as np
from jax import lax
from jax.experimental import pallas as pl
from jax.experimental.pallas import tpu as pltpu

F32 = jnp.float32
BF16 = jnp.bfloat16
FP8 = jnp.float8_e4m3fn
FP8_TARGET = 256.0
FP8_SCALE_CAP = 2.0 ** 60

D_MODEL = 2048
SSD_WIDTH = D_MODEL
SSD_HEADDIM = 64
SSD_HEADS = SSD_WIDTH // SSD_HEADDIM
SSD_GROUPS = 4
SSD_STATE = 128
SSD_CHUNK = 128
HEADS_PER_GROUP = SSD_HEADS // SSD_GROUPS
GROUP_WIDTH = HEADS_PER_GROUP * SSD_HEADDIM
SC_WIDTH = D_MODEL
SC_GROUPS = 32
XBC_WIDTH = SSD_WIDTH + 2 * SSD_GROUPS * SSD_STATE
MAIN_COLS = 4 * D_MODEL + XBC_WIDTH
XBC_COL0 = 4 * D_MODEL
PEER_HEADS = 8
N_KEYS = 128
PEER_TOPK = 16
D_QUERY = 512
HALF_QUERY = D_QUERY // 2
NORM_EPS = 1e-5
LANES = 128
SUBLANES = 8
HALO_ROWS = 16
POST_COL_PIECE = 256
PEER_TOKEN_BLOCK = 512
PEER_OUT_PIECE = 256
VMEM_LIMIT = 56 * 1024 * 1024


def _dot(a, b):
    return jnp.dot(a, b, preferred_element_type=F32)


def _dot_nt(a, b):
    return lax.dot_general(a, b, (((1,), (1,)), ((), ())), preferred_element_type=F32)


def _split2(x):
    hi = x.astype(BF16)
    lo = (x - hi.astype(F32)).astype(BF16)
    return hi, lo


def _split3(x):
    hi = x.astype(BF16)
    r = x - hi.astype(F32)
    mid = r.astype(BF16)
    lo = (r - mid.astype(F32)).astype(BF16)
    return hi, mid, lo


def _softplus(x):
    return jnp.maximum(x, 0.0) + jnp.log1p(jnp.exp(-jnp.abs(x)))


def _silu(x):
    return x * (1.0 / (1.0 + jnp.exp(-x)))


def _params(sem):
    return pltpu.CompilerParams(dimension_semantics=sem, vmem_limit_bytes=VMEM_LIMIT)


def _ada_kernel(c_ref, w_ref, b_ref, o_ref):
    sc = _silu(c_ref[...])
    o_ref[...] = _dot(sc.astype(BF16), w_ref[...].astype(BF16)) + b_ref[...]


def _ada(c_pad, w, b):
    rows, d = c_pad.shape
    n = w.shape[1]
    bn = 1024
    return pl.pallas_call(
        _ada_kernel,
        grid=(n // bn,),
        in_specs=[pl.BlockSpec((rows, d), lambda j: (0, 0)),
                  pl.BlockSpec((d, bn), lambda j: (0, j)),
                  pl.BlockSpec((1, bn), lambda j: (0, j))],
        out_specs=pl.BlockSpec((rows, bn), lambda j: (0, j)),
        out_shape=jax.ShapeDtypeStruct((rows, n), F32),
        compiler_params=_params(("arbitrary",)),
        name="ada",
    )(c_pad, w, b)


DT_COL0 = SSD_WIDTH + XBC_WIDTH
GATES_COL0 = DT_COL0 + 2 * SSD_HEADS


def _win_prep_kernel(w_ref, main_ref, dh_ref, dl_ref):
    w = w_ref[...]
    main_ref[:, 0:SSD_WIDTH] = w[:, 0:SSD_WIDTH].astype(BF16)
    main_ref[:, SSD_WIDTH:XBC_COL0] = w[:, GATES_COL0:GATES_COL0 + 3 * SC_WIDTH].astype(BF16)
    main_ref[:, XBC_COL0:MAIN_COLS] = w[:, SSD_WIDTH:DT_COL0].astype(BF16)
    dt = w[:, DT_COL0:DT_COL0 + LANES]
    lane = lax.broadcasted_iota(jnp.int32, dt.shape, 1)
    hi, lo = _split2(jnp.where(lane < 2 * SSD_HEADS, dt, 0.0))
    dh_ref[...] = hi
    dl_ref[...] = lo


def _win_prep(w):
    d, n = w.shape
    rows = 128
    return pl.pallas_call(
        _win_prep_kernel,
        grid=(d // rows,),
        in_specs=[pl.BlockSpec((rows, n), lambda i: (i, 0))],
        out_specs=[pl.BlockSpec((rows, MAIN_COLS), lambda i: (i, 0)),
                   pl.BlockSpec((rows, LANES), lambda i: (i, 0)),
                   pl.BlockSpec((rows, LANES), lambda i: (i, 0))],
        out_shape=[jax.ShapeDtypeStruct((d, MAIN_COLS), BF16), jax.ShapeDtypeStruct((d, LANES), BF16),
                   jax.ShapeDtypeStruct((d, LANES), BF16)],
        compiler_params=_params(("parallel",)),
        name="winprep",
    )(w)


def _inproj_kernel(x_ref, mod_ref, w_ref, wdh_ref, wdl_ref, o_ref, dt_ref, h_scr):
    @pl.when(pl.program_id(1) == 0)
    def _():
        m = mod_ref[0]
        h = x_ref[...] * (1.0 + m[1:2]) + m[0:1]
        hi, lo = _split2(h)
        h_scr[...] = hi
        dt_ref[...] = _dot(hi, wdh_ref[...]) + _dot(lo, wdh_ref[...]) + _dot(hi, wdl_ref[...])

    o_ref[...] = _dot(h_scr[...], w_ref[...]).astype(o_ref.dtype)


def _inproj(x2, mod3, w_main, wd_hi, wd_lo, seq):
    t, d = x2.shape
    n = w_main.shape[1]
    bm = min(1024, seq)
    bn = 1024
    per_batch = seq // bm
    return pl.pallas_call(
        _inproj_kernel,
        grid=(t // bm, n // bn),
        in_specs=[pl.BlockSpec((bm, d), lambda i, j: (i, 0)),
                  pl.BlockSpec((1, 6, d), lambda i, j: (i // per_batch, 0, 0)),
                  pl.BlockSpec((d, bn), lambda i, j: (0, j)),
                  pl.BlockSpec((d, LANES), lambda i, j: (0, 0)),
                  pl.BlockSpec((d, LANES), lambda i, j: (0, 0))],
        out_specs=[pl.BlockSpec((bm, bn), lambda i, j: (i, j)),
                   pl.BlockSpec((bm, LANES), lambda i, j: (i, 0))],
        out_shape=[jax.ShapeDtypeStruct((t, n), BF16),
                   jax.ShapeDtypeStruct((t, LANES), F32)],
        scratch_shapes=[pltpu.VMEM((bm, d), BF16)],
        compiler_params=_params(("parallel", "arbitrary")),
        name="inproj",
    )(x2, mod3, w_main, wd_hi, wd_lo)


def _shifted_rows(u, prev_row, next_row):
    rows = u.shape[0]
    ridx = lax.broadcasted_iota(jnp.int32, u.shape, 0)
    up = jnp.where(ridx == 0, prev_row, pltpu.roll(u, 1, axis=0))
    un = jnp.where(ridx == rows - 1, next_row, pltpu.roll(u, rows - 1, axis=0))
    return up, un


def _halo_specs(bm, bw, col_of, t):
    per = bm // HALO_ROWS
    last = t // HALO_ROWS - 1
    prev = pl.BlockSpec((HALO_ROWS, bw), lambda i, j: (jnp.maximum(i * per - 1, 0), col_of(j)))
    nxt = pl.BlockSpec((HALO_ROWS, bw), lambda i, j: (jnp.minimum((i + 1) * per, last), col_of(j)))
    return prev, nxt


def _conv_kernel(u_ref, p_ref, n_ref, w_ref, b_ref, o_ref, *, bm, seq):
    i = pl.program_id(0)
    u = u_ref[...].astype(F32)
    has_prev = ((i * bm) % seq != 0).astype(F32)
    has_next = (((i + 1) * bm) % seq != 0).astype(F32)
    up, un = _shifted_rows(u, p_ref[HALO_ROWS - 1:HALO_ROWS, :].astype(F32) * has_prev,
                           n_ref[0:1, :].astype(F32) * has_next)
    w = w_ref[...]
    o_ref[...] = _silu(up * w[0:1] + u * w[1:2] + un * w[2:3] + b_ref[...])


def _conv(proj, conv_w, conv_b, seq):
    t = proj.shape[0]
    bm = min(512, seq)
    bw = 1024
    col0 = XBC_COL0 // bw
    col_of = lambda j: col0 + j
    prev, nxt = _halo_specs(bm, bw, col_of, t)
    return pl.pallas_call(
        functools.partial(_conv_kernel, bm=bm, seq=seq),
        grid=(t // bm, XBC_WIDTH // bw),
        in_specs=[pl.BlockSpec((bm, bw), lambda i, j: (i, col_of(j))), prev, nxt,
                  pl.BlockSpec((3, bw), lambda i, j: (0, j)),
                  pl.BlockSpec((1, bw), lambda i, j: (0, j))],
        out_specs=pl.BlockSpec((bm, bw), lambda i, j: (i, j)),
        out_shape=jax.ShapeDtypeStruct((t, XBC_WIDTH), F32),
        compiler_params=_params(("parallel", "parallel")),
        name="ssdconv",
    )(proj, proj, proj, conv_w, conv_b)


def _ssd_direction(x_ref, b_ref, c_ref, dt_ref, e_ref, h_scr, y_ref, bias, a, dskip, backward):
    cs_len = SSD_CHUNK
    row = lax.broadcasted_iota(jnp.int32, (cs_len, cs_len), 0)
    col = lax.broadcasted_iota(jnp.int32, (cs_len, cs_len), 1)
    tri = (col <= row).astype(BF16)
    dt = _softplus(dt_ref[...] + bias)
    adt = dt * a
    h3 = _split3(adt)
    cs = _dot(tri, h3[0]) + _dot(tri, h3[1]) + _dot(tri, h3[2])
    tot = cs[cs_len - 1:cs_len, :]
    if backward:
        ecs = cs - adt
        p = -ecs
        wst = dt * jnp.exp(ecs)
        indec = jnp.exp(tot - ecs)
        mask = col >= row
    else:
        p = cs
        wst = dt * jnp.exp(tot - cs)
        indec = jnp.exp(cs)
        mask = col <= row
    pt = p.T
    e = e_ref[...]
    dtx = _dot(dt.astype(BF16), e)
    wstx = _dot(wst.astype(BF16), e)
    indx = _dot(indec.astype(BF16), e)
    xs = x_ref[...]
    xd = (xs * dtx).astype(BF16)
    xw = (xs * wstx).astype(BF16)
    col_off = SSD_HEADS if backward else 0
    pair = 2 * SSD_HEADDIM
    first_of_pair = lax.broadcasted_iota(jnp.int32, (cs_len, pair), 1) < SSD_HEADDIM
    zero = jnp.zeros((), BF16)
    for g in range(SSD_GROUPS):
        gs = slice(g * GROUP_WIDTH, (g + 1) * GROUP_WIDTH)
        bg = b_ref[:, g * SSD_STATE:(g + 1) * SSD_STATE]
        cg = c_ref[:, g * SSD_STATE:(g + 1) * SSD_STATE].astype(BF16)
        cb = _dot_nt(cg, bg.astype(BF16))
        h_in = h_scr[g]
        y_off = _dot(cg, h_in.astype(BF16)) * indx[:, gs]
        st = _dot(bg.T.astype(BF16), xw[:, gs])
        for r in range(0, HEADS_PER_GROUP, 2):
            hd = g * HEADS_PER_GROUP + r
            ms = []
            for ci in (hd + col_off, hd + col_off + 1):
                seg = p[:, ci:ci + 1] - pt[ci:ci + 1, :]
                lmat = jnp.exp(jnp.where(mask, seg, -jnp.inf))
                ms.append((cb * lmat).astype(BF16))
            hs = slice(hd * SSD_HEADDIM, (hd + 2) * SSD_HEADDIM)
            x2h = xd[:, hs]
            rhs = jnp.concatenate([jnp.where(first_of_pair, x2h, zero), jnp.where(first_of_pair, zero, x2h)], axis=0)
            y = _dot(jnp.concatenate(ms, axis=1), rhs) + y_off[:, r * SSD_HEADDIM:(r + 2) * SSD_HEADDIM]
            if dskip is not None:
                y = y + xs[:, hs] * dskip[:, hs]
            y_ref[:, hs] = y
        edge = 0 if backward else cs_len - 1
        h_scr[g] = indx[edge:edge + 1, gs] * h_in + st


def _ssd_kernel(xf_ref, bf_ref, cf_ref, dtf_ref, xb_ref, bb_ref, cb_ref, dtb_ref,
                bias_ref, alog_ref, dskip_ref, ef_ref, eb_ref, yf_ref, yb_ref, hf_scr, hb_scr):
    @pl.when(pl.program_id(1) == 0)
    def _():
        hf_scr[...] = jnp.zeros_like(hf_scr)
        hb_scr[...] = jnp.zeros_like(hb_scr)

    bias = bias_ref[...]
    a = -jnp.exp(alog_ref[...])
    _ssd_direction(xf_ref, bf_ref, cf_ref, dtf_ref, ef_ref, hf_scr, yf_ref, bias, a, dskip_ref[...], False)
    _ssd_direction(xb_ref, bb_ref, cb_ref, dtb_ref, eb_ref, hb_scr, yb_ref, bias, a, None, True)


def _ssd(xbc, dt_raw, bias, alog, dskip_x, e_f, e_b, batch, seq):
    t = xbc.shape[0]
    nc = seq // SSD_CHUNK
    cl = SSD_CHUNK
    gn = SSD_GROUPS * SSD_STATE
    bcol = SSD_WIDTH // gn
    fwd = lambda b, k: b * nc + k
    bwd = lambda b, k: b * nc + (nc - 1 - k)

    def specs(ch):
        return [pl.BlockSpec((cl, SSD_WIDTH), lambda b, k: (ch(b, k), 0)),
                pl.BlockSpec((cl, gn), lambda b, k: (ch(b, k), bcol)),
                pl.BlockSpec((cl, gn), lambda b, k: (ch(b, k), bcol + 1)),
                pl.BlockSpec((cl, LANES), lambda b, k: (ch(b, k), 0))]

    const = lambda shape: pl.BlockSpec(shape, lambda b, k: (0, 0))
    state = pltpu.VMEM((SSD_GROUPS, SSD_STATE, GROUP_WIDTH), F32)
    return pl.pallas_call(
        _ssd_kernel,
        grid=(batch, nc),
        in_specs=specs(fwd) + specs(bwd) + [const((1, LANES)), const((1, LANES)), const((1, SSD_WIDTH)),
                                            const((LANES, SSD_WIDTH)), const((LANES, SSD_WIDTH))],
        out_specs=[pl.BlockSpec((cl, SSD_WIDTH), lambda b, k: (fwd(b, k), 0)),
                   pl.BlockSpec((cl, SSD_WIDTH), lambda b, k: (bwd(b, k), 0))],
        out_shape=[jax.ShapeDtypeStruct((t, SSD_WIDTH), F32)] * 2,
        scratch_shapes=[state, state],
        compiler_params=_params(("parallel", "arbitrary")),
        name="ssd",
    )(xbc, xbc, xbc, dt_raw, xbc, xbc, xbc, dt_raw, bias, alog, dskip_x, e_f, e_b)


def _group_rms(y, r_ref, e_ref, group_size):
    gsum = _dot((y * y).astype(BF16), r_ref[...])
    inv = lax.rsqrt(gsum * (1.0 / group_size) + NORM_EPS)
    return y * _dot(inv.astype(BF16), e_ref[...])


def _layer_norm(v, g, b):
    mu = jnp.mean(v, axis=-1, keepdims=True)
    vc = v - mu
    var = jnp.mean(vc * vc, axis=-1, keepdims=True)
    return vc * lax.rsqrt(var + NORM_EPS) * g + b


def _post_kernel(yf_ref, yb_ref, z_ref, gb_ref, gc_ref, gcp_ref, gcn_ref, v_ref, vp_ref, vn_ref,
                 x_ref, mod_ref, scw_ref, ssdg_ref, scg_ref, r4_ref, e4_ref, r32_ref, e32_ref,
                 wout_ref, lng_ref, lnb_ref, x1_ref, h2_ref, amax_ref, sumsq_ref, *, bm, seq, alpha):
    i = pl.program_id(0)
    m = mod_ref[0]
    y = (yf_ref[...] + yb_ref[...]) * _silu(z_ref[...].astype(F32))
    y_ssd = (_group_rms(y, r4_ref, e4_ref, SSD_WIDTH // SSD_GROUPS) * ssdg_ref[...]).astype(BF16)
    n_cols = [slice(c, c + POST_COL_PIECE) for c in range(0, wout_ref.shape[1], POST_COL_PIECE)]
    mix_ssd = [_dot(y_ssd, wout_ref[0:SSD_WIDTH, cs]) for cs in n_cols]

    f32 = lambda ref, rows=slice(None): ref[rows, :].astype(F32)
    u = f32(gc_ref) * f32(v_ref)
    has_prev = ((i * bm) % seq != 0).astype(F32)
    has_next = (((i + 1) * bm) % seq != 0).astype(F32)
    last = slice(HALO_ROWS - 1, HALO_ROWS)
    first = slice(0, 1)
    up, un = _shifted_rows(u, f32(gcp_ref, last) * f32(vp_ref, last) * has_prev,
                           f32(gcn_ref, first) * f32(vn_ref, first) * has_next)
    w = scw_ref[...]
    y_sc = f32(gb_ref) * (up * w[0:1] + u * w[1:2] + un * w[2:3])
    y_sc = (_group_rms(y_sc, r32_ref, e32_ref, SC_WIDTH // SC_GROUPS) * scg_ref[...]).astype(BF16)
    mix = jnp.concatenate([part + _dot(y_sc, wout_ref[SSD_WIDTH:SSD_WIDTH + SC_WIDTH, cs])
                           for part, cs in zip(mix_ssd, n_cols)], axis=1)
    x1 = _layer_norm(alpha * x_ref[...] + m[2:3] * mix, lng_ref[...], lnb_ref[...])
    x1_ref[...] = x1
    h2 = x1 * (1.0 + m[4:5]) + m[3:4]
    h2_ref[...] = h2.astype(BF16)
    amax_ref[...] = jnp.max(jnp.abs(h2), axis=-1, keepdims=True)
    sumsq_ref[...] = jnp.sum(h2 * h2, axis=-1, keepdims=True)


def _single(shape, index_map):
    return pl.BlockSpec(shape, index_map, pipeline_mode=pl.Buffered(1))


def _post(yf, yb, proj, x2, mod3, scw, ssdg, scg, r4, e4, r32, e32, wout, lng, lnb, seq, alpha):
    t, d = x2.shape
    bm = min(256, seq)
    per_batch = seq // bm
    per = bm // HALO_ROWS
    last = t // HALO_ROWS - 1
    main = lambda c: pl.BlockSpec((bm, d), lambda i: (i, c))
    prev = lambda c: pl.BlockSpec((HALO_ROWS, d), lambda i: (jnp.maximum(i * per - 1, 0), c))
    nxt = lambda c: pl.BlockSpec((HALO_ROWS, d), lambda i: (jnp.minimum((i + 1) * per, last), c))
    const = lambda shape: _single(shape, lambda i: (0, 0))
    return pl.pallas_call(
        functools.partial(_post_kernel, bm=bm, seq=seq, alpha=alpha),
        grid=(t // bm,),
        in_specs=[main(0), main(0), main(0), main(1), main(2), prev(2), nxt(2), main(3), prev(3), nxt(3),
                  main(0), pl.BlockSpec((1, 6, d), lambda i: (i // per_batch, 0, 0)),
                  const((3, d)), const((1, d)), const((1, d)),
                  const((d, LANES)), const((LANES, d)), const((d, LANES)), const((LANES, d)),
                  const((2 * d, d)), const((1, d)), const((1, d))],
        out_specs=[pl.BlockSpec((bm, d), lambda i: (i, 0)), pl.BlockSpec((bm, d), lambda i: (i, 0)),
                   pl.BlockSpec((bm, 1), lambda i: (i, 0)), pl.BlockSpec((bm, 1), lambda i: (i, 0))],
        out_shape=[jax.ShapeDtypeStruct((t, d), F32), jax.ShapeDtypeStruct((t, d), BF16),
                   jax.ShapeDtypeStruct((t, 1), F32), jax.ShapeDtypeStruct((t, 1), F32)],
        compiler_params=_params(("parallel",)),
        name="post",
    )(yf, yb, proj, proj, proj, proj, proj, proj, proj, proj, x2, mod3,
      scw, ssdg, scg, r4, e4, r32, e32, wout, lng, lnb)


def _top_values(s, n, want_rank=False):
    vals = []
    rank = jnp.full(s.shape, float(n), F32) if want_rank else None
    for r in range(n):
        m = jnp.max(s, axis=0, keepdims=True)
        vals.append(m)
        hit = s == m
        if want_rank:
            rank = jnp.where(hit, float(r), rank)
        s = jnp.where(hit, -jnp.inf, s)
    return (vals, rank) if want_rank else vals


def _query_kernel(h2_ref, wq_ref, k_ref, r2_ref, e2_ref, n1_ref, c1_ref):
    tb = h2_ref.shape[0]
    for h in range(PEER_HEADS):
        q = _dot(h2_ref[...], wq_ref[:, h * D_QUERY:(h + 1) * D_QUERY]).astype(BF16)
        s1 = _dot_nt(k_ref[h, 0], q[:, :HALF_QUERY])
        s2 = _dot_nt(k_ref[h, 1], q[:, HALF_QUERY:])
        v1 = _top_values(s1, PEER_TOPK)
        v2, r2 = _top_values(s2, PEER_TOPK, want_rank=True)
        pairs = [(a, b) for a in range(PEER_TOPK) for b in range(PEER_TOPK) if (a + 1) * (b + 1) <= PEER_TOPK]
        sums = {ab: v1[ab[0]] + v2[ab[1]] for ab in pairs}
        rows = [sums[ab] for ab in pairs]
        rows += [jnp.full((1, tb), -jnp.inf, F32)] * ((-len(rows)) % SUBLANES)
        top = _top_values(jnp.concatenate(rows, axis=0), PEER_TOPK)
        z = jnp.ones((1, tb), F32)
        for kk in range(1, PEER_TOPK):
            z = z + jnp.exp(top[kk] - top[0])
        tau = top[PEER_TOPK - 1]
        n1 = jnp.zeros(s1.shape, F32)
        for a in range(PEER_TOPK):
            cnt = sum(jnp.where(sums[(a, b)] >= tau, 1.0, 0.0) for b in range(PEER_TOPK) if (a, b) in sums)
            n1 = jnp.where(s1 == v1[a], cnt, n1)
        r2_ref[h] = r2.astype(BF16)
        e2_ref[h] = jnp.exp(s2 - v2[0]).astype(BF16)
        n1_ref[h] = n1
        c1_ref[h] = jnp.exp(s1 - v1[0]) / z


def _query(h2, wq, keys):
    t, d = h2.shape
    tb = min(256, t)
    ospec = pl.BlockSpec((PEER_HEADS, N_KEYS, tb), lambda i: (0, 0, i))
    return pl.pallas_call(
        _query_kernel,
        grid=(t // tb,),
        in_specs=[pl.BlockSpec((tb, d), lambda i: (i, 0)),
                  _single((d, PEER_HEADS * D_QUERY), lambda i: (0, 0)),
                  _single((PEER_HEADS, 2, N_KEYS, HALF_QUERY), lambda i: (0, 0, 0, 0))],
        out_specs=[ospec] * 4,
        out_shape=[jax.ShapeDtypeStruct((PEER_HEADS, N_KEYS, t), dt) for dt in (BF16, BF16, F32, F32)],
        compiler_params=_params(("parallel",)),
        name="peerquery",
    )(h2, wq, keys)


def _pow2_scale(bound):
    ratio = FP8_TARGET / jnp.where(bound > 0, bound, FP8_TARGET)
    exponent_only = lax.bitcast_convert_type(ratio, jnp.int32) & jnp.int32(0x7F800000)
    return jnp.minimum(lax.bitcast_convert_type(exponent_only, F32), FP8_SCALE_CAP)


def _expert_prep_kernel(w_ref, q_ref, inv_ref, sumsq_ref, *, transpose):
    w = w_ref[...]
    s = _pow2_scale(jnp.max(jnp.abs(w), axis=1, keepdims=True))
    inv_ref[...] = 1.0 / s
    sumsq_ref[...] = jnp.sum(w * w, axis=1, keepdims=True)
    ws = w * s
    q_ref[...] = (ws.T if transpose else ws).astype(FP8)


def _expert_prep(table, transpose):
    n_exp, d = table.shape
    rows = 512
    col = pl.BlockSpec((rows, 1), lambda i: (i, 0))
    q_spec = pl.BlockSpec((d, rows), lambda i: (0, i)) if transpose else pl.BlockSpec((rows, d), lambda i: (i, 0))
    q_shape = (d, n_exp) if transpose else (n_exp, d)
    return pl.pallas_call(
        functools.partial(_expert_prep_kernel, transpose=transpose),
        grid=(n_exp // rows,),
        in_specs=[pl.BlockSpec((rows, d), lambda i: (i, 0))],
        out_specs=[q_spec, col, col],
        out_shape=[jax.ShapeDtypeStruct(q_shape, FP8), jax.ShapeDtypeStruct((n_exp, 1), F32),
                   jax.ShapeDtypeStruct((n_exp, 1), F32)],
        compiler_params=_params(("parallel",)),
        name="expertprep_t" if transpose else "expertprep",
    )(table)


def _peer_kernel(h2_ref, u_ref, vt_ref, r2_ref, e2_ref, n1_ref, c1_ref, x1_ref, mod_ref, lng_ref, lnb_ref,
                 iu_ref, iv_ref, ish_ref, sp_ref, sh_ref, isp_ref, o_ref, acc_ref, h8_ref, *, alpha):
    e = pl.program_id(1)

    @pl.when(e == 0)
    def _():
        acc_ref[...] = jnp.zeros_like(acc_ref)
        h8_ref[...] = (h2_ref[...].astype(F32) * sh_ref[...]).T.astype(FP8)

    parts = []
    for ii in range(u_ref.shape[0] // N_KEYS):
        rows = slice(ii * N_KEYS, (ii + 1) * N_KEYS)
        at = _dot(u_ref[rows, :], h8_ref[...])
        g = None
        for h in range(PEER_HEADS):
            n_row = n1_ref[h, ii:ii + 1, :].astype(BF16)
            c_row = c1_ref[h, ii:ii + 1, :].astype(BF16)
            w = jnp.where(r2_ref[h] < n_row, e2_ref[h], jnp.zeros((), BF16)) * c_row
            g = w if g is None else g + w
        b = at * (iu_ref[rows, :] * ish_ref[...])
        p = b * (1.0 + lax.erf(b)) * (iv_ref[rows, :] * sp_ref[...])
        parts.append((p.astype(BF16) * g).astype(FP8))
    pt = jnp.concatenate(parts, axis=0)
    for c in range(0, vt_ref.shape[0], PEER_OUT_PIECE):
        dr = slice(c, c + PEER_OUT_PIECE)
        acc_ref[dr, :] += _dot(vt_ref[dr, :], pt)

    @pl.when(e == pl.num_programs(1) - 1)
    def _():
        m = mod_ref[0]
        ffn = acc_ref[...].T * isp_ref[...]
        o_ref[...] = _layer_norm(alpha * x1_ref[...] + m[5:6] * ffn, lng_ref[...], lnb_ref[...])


def _peer(h2, u8, vt8, r2, e2, n1, c1, x1, mod3, lng, lnb, iu, iv, ish_blk, sp_blk, sh, isp, seq, alpha):
    t, d = h2.shape
    n_exp = u8.shape[0]
    tb = min(PEER_TOKEN_BLOCK, seq)
    eb = 1024
    per_batch = seq // tb
    kb = eb // N_KEYS
    full = _single((PEER_HEADS, N_KEYS, tb), lambda i, j: (0, 0, i))
    part = pl.BlockSpec((PEER_HEADS, kb, tb), lambda i, j: (0, j, i))
    ecol = pl.BlockSpec((eb, 1), lambda i, j: (j, 0))
    blk = _single((None, 1, 1), lambda i, j: (i, 0, 0))
    tcol = _single((tb, 1), lambda i, j: (i, 0))
    return pl.pallas_call(
        functools.partial(_peer_kernel, alpha=alpha),
        grid=(t // tb, n_exp // eb),
        in_specs=[_single((tb, d), lambda i, j: (i, 0)),
                  pl.BlockSpec((eb, d), lambda i, j: (j, 0)),
                  pl.BlockSpec((d, eb), lambda i, j: (0, j)),
                  full, full, part, part,
                  _single((tb, d), lambda i, j: (i, 0)),
                  pl.BlockSpec((1, 6, d), lambda i, j: (i // per_batch, 0, 0)),
                  pl.BlockSpec((1, d), lambda i, j: (0, 0)),
                  pl.BlockSpec((1, d), lambda i, j: (0, 0)),
                  ecol, ecol, blk, blk, tcol, tcol],
        out_specs=pl.BlockSpec((tb, d), lambda i, j: (i, 0)),
        out_shape=jax.ShapeDtypeStruct((t, d), F32),
        scratch_shapes=[pltpu.VMEM((d, tb), F32), pltpu.VMEM((d, tb), FP8)],
        compiler_params=_params(("parallel", "arbitrary")),
        name="peer",
    )(h2, u8, vt8, r2, e2, n1, c1, x1, mod3, lng, lnb, iu, iv, ish_blk, sp_blk, sh, isp)


def _ffn(h2, h_amax, h_sumsq, x1, mod3, w_query, sub_keys, expert_u, expert_v, ln2_g, ln2_b, seq, alpha):
    t = h2.shape[0]
    r2, e2, n1, c1 = _query(h2, w_query.astype(BF16), sub_keys.astype(BF16))
    u8, inv_su, u_sumsq = _expert_prep(expert_u, transpose=False)
    vt8, inv_sv, _ = _expert_prep(expert_v, transpose=True)
    tb = min(PEER_TOKEN_BLOCK, seq)
    per_block = lambda v: jnp.max(v.reshape(t // tb, tb), axis=1).reshape(-1, 1, 1)
    sh = _pow2_scale(per_block(h_amax))
    sp = _pow2_scale(jnp.sqrt(jnp.max(u_sumsq) * per_block(h_sumsq)) * (PEER_HEADS * jnp.max(inv_sv)))
    sqrt_half = np.float32(math.sqrt(0.5))
    per_token = lambda s: jnp.broadcast_to(s, (t // tb, tb, 1)).reshape(t, 1)
    return _peer(h2, u8, vt8, r2, e2, n1, c1, x1, mod3, _row(ln2_g), _row(ln2_b), inv_su * sqrt_half,
                 inv_sv * sqrt_half, 1.0 / sh, sp, per_token(sh), per_token(1.0 / sp), seq, alpha)


def _one_hot_cols(n_rows, n_cols, group_size, row_offset=0):
    r = np.arange(n_rows)[:, None]
    c = np.arange(n_cols)[None, :]
    return jnp.asarray(r == row_offset + c // group_size, dtype=BF16)


def _row(v):
    return v.reshape(1, -1).astype(F32)


def _pad_lanes(v):
    return jnp.pad(v, (0, LANES - v.shape[0])).reshape(1, LANES).astype(F32)


def kernel(x, c, w_ada, b_ada, w_in, conv_ssd_w, conv_ssd_b, dt_bias_f, dt_bias_b, a_log_f, a_log_b, d_skip, ssd_norm_g, short_conv_w, sc_norm_g, w_out, ln1_g, ln1_b, w_query, sub_keys, expert_u, expert_v, ln2_g, ln2_b):
    batch, seq, d = x.shape
    depth = w_ada.shape[0]
    alpha = (2.0 * depth) ** 0.25
    t = batch * seq
    x2 = x.reshape(t, d)
    c_pad = jnp.pad(c, ((0, SUBLANES - batch % SUBLANES), (0, 0))) if batch % SUBLANES else c

    e_f = _one_hot_cols(LANES, SSD_WIDTH, SSD_HEADDIM, 0)
    e_b = _one_hot_cols(LANES, SSD_WIDTH, SSD_HEADDIM, SSD_HEADS)
    e4 = _one_hot_cols(LANES, SSD_WIDTH, SSD_WIDTH // SSD_GROUPS)
    e32 = _one_hot_cols(LANES, SC_WIDTH, SC_WIDTH // SC_GROUPS)
    r4 = e4.T
    r32 = e32.T

    for i in range(depth):
        mod = _ada(c_pad, w_ada[i], b_ada[i].reshape(1, -1))
        mod3 = mod[:batch].reshape(batch, 6, d)

        w_main, wd_hi, wd_lo = _win_prep(w_in[i])
        proj, dt_raw = _inproj(x2, mod3, w_main, wd_hi, wd_lo, seq)

        xbc = _conv(proj, conv_ssd_w[i], conv_ssd_b[i].reshape(1, -1), seq)
        bias = _pad_lanes(jnp.concatenate([dt_bias_f[i], dt_bias_b[i]]))
        alog = _pad_lanes(jnp.concatenate([a_log_f[i], a_log_b[i]]))
        dskip_x = _row(jnp.repeat(d_skip[i], SSD_HEADDIM))
        y_f, y_b = _ssd(xbc, dt_raw, bias, alog, dskip_x, e_f, e_b, batch, seq)

        x1, h2, h_amax, h_sumsq = _post(y_f, y_b, proj, x2, mod3, short_conv_w[i], _row(ssd_norm_g[i]),
                                        _row(sc_norm_g[i]), r4, e4, r32, e32, w_out[i].astype(BF16),
                                        _row(ln1_g[i]), _row(ln1_b[i]), seq, alpha)

        x2 = _ffn(h2, h_amax, h_sumsq, x1, mod3, w_query[i], sub_keys[i], expert_u[i], expert_v[i],
                  ln2_g[i], ln2_b[i], seq, alpha)
    return x2.reshape(batch, seq, d)
```

```python
import functools
import math

import jax
import jax.numpy as jnp
import numpy as np
from jax import lax
from jax.experimental import pallas as pl
from jax.experimental.pallas import tpu as pltpu

F32 = jnp.float32
BF16 = jnp.bfloat16
FP8 = jnp.float8_e4m3fn
FP8_TARGET = 256.0
FP8_SCALE_CAP = 2.0 ** 60

D_MODEL = 2048
SSD_WIDTH = D_MODEL
SSD_HEADDIM = 64
SSD_HEADS = SSD_WIDTH // SSD_HEADDIM
SSD_GROUPS = 4
SSD_STATE = 128
SSD_CHUNK = 128
HEADS_PER_GROUP = SSD_HEADS // SSD_GROUPS
GROUP_WIDTH = HEADS_PER_GROUP * SSD_HEADDIM
SC_WIDTH = D_MODEL
SC_GROUPS = 32
XBC_WIDTH = SSD_WIDTH + 2 * SSD_GROUPS * SSD_STATE
MAIN_COLS = 4 * D_MODEL + XBC_WIDTH
XBC_COL0 = 4 * D_MODEL
PEER_HEADS = 8
N_KEYS = 128
PEER_TOPK = 16
D_QUERY = 512
HALF_QUERY = D_QUERY // 2
NORM_EPS = 1e-5
LANES = 128
SUBLANES = 8
HALO_ROWS = 16
POST_COL_PIECE = 256
PEER_TOKEN_BLOCK = 512
PEER_OUT_PIECE = 256
VMEM_LIMIT = 56 * 1024 * 1024


def _dot(a, b):
    return jnp.dot(a, b, preferred_element_type=F32)


def _dot_nt(a, b):
    return lax.dot_general(a, b, (((1,), (1,)), ((), ())), preferred_element_type=F32)


def _split2(x):
    hi = x.astype(BF16)
    lo = (x - hi.astype(F32)).astype(BF16)
    return hi, lo


def _split3(x):
    hi = x.astype(BF16)
    r = x - hi.astype(F32)
    mid = r.astype(BF16)
    lo = (r - mid.astype(F32)).astype(BF16)
    return hi, mid, lo


def _softplus(x):
    return jnp.maximum(x, 0.0) + jnp.log1p(jnp.exp(-jnp.abs(x)))


def _silu(x):
    return x * (1.0 / (1.0 + jnp.exp(-x)))


def _params(sem):
    return pltpu.CompilerParams(dimension_semantics=sem, vmem_limit_bytes=VMEM_LIMIT)


def _ada_kernel(c_ref, w_ref, b_ref, o_ref):
    sc = _silu(c_ref[...])
    o_ref[...] = _dot(sc.astype(BF16), w_ref[...].astype(BF16)) + b_ref[...]


def _ada(c_pad, w, b):
    rows, d = c_pad.shape
    n = w.shape[1]
    bn = 1024
    return pl.pallas_call(
        _ada_kernel,
        grid=(n // bn,),
        in_specs=[pl.BlockSpec((rows, d), lambda j: (0, 0)),
                  pl.BlockSpec((d, bn), lambda j: (0, j)),
                  pl.BlockSpec((1, bn), lambda j: (0, j))],
        out_specs=pl.BlockSpec((rows, bn), lambda j: (0, j)),
        out_shape=jax.ShapeDtypeStruct((rows, n), F32),
        compiler_params=_params(("arbitrary",)),
        name="ada",
    )(c_pad, w, b)


DT_COL0 = SSD_WIDTH + XBC_WIDTH
GATES_COL0 = DT_COL0 + 2 * SSD_HEADS


def _win_prep_kernel(w_ref, main_ref, dh_ref, dl_ref):
    w = w_ref[...]
    main_ref[:, 0:SSD_WIDTH] = w[:, 0:SSD_WIDTH].astype(BF16)
    main_ref[:, SSD_WIDTH:XBC_COL0] = w[:, GATES_COL0:GATES_COL0 + 3 * SC_WIDTH].astype(BF16)
    main_ref[:, XBC_COL0:MAIN_COLS] = w[:, SSD_WIDTH:DT_COL0].astype(BF16)
    dt = w[:, DT_COL0:DT_COL0 + LANES]
    lane = lax.broadcasted_iota(jnp.int32, dt.shape, 1)
    hi, lo = _split2(jnp.where(lane < 2 * SSD_HEADS, dt, 0.0))
    dh_ref[...] = hi
    dl_ref[...] = lo


def _win_prep(w):
    d, n = w.shape
    rows = 128
    return pl.pallas_call(
        _win_prep_kernel,
        grid=(d // rows,),
        in_specs=[pl.BlockSpec((rows, n), lambda i: (i, 0))],
        out_specs=[pl.BlockSpec((rows, MAIN_COLS), lambda i: (i, 0)),
                   pl.BlockSpec((rows, LANES), lambda i: (i, 0)),
                   pl.BlockSpec((rows, LANES), lambda i: (i, 0))],
        out_shape=[jax.ShapeDtypeStruct((d, MAIN_COLS), BF16), jax.ShapeDtypeStruct((d, LANES), BF16),
                   jax.ShapeDtypeStruct((d, LANES), BF16)],
        compiler_params=_params(("parallel",)),
        name="winprep",
    )(w)


def _inproj_kernel(x_ref, mod_ref, w_ref, wdh_ref, wdl_ref, o_ref, dt_ref, h_scr):
    @pl.when(pl.program_id(1) == 0)
    def _():
        m = mod_ref[0]
        h = x_ref[...] * (1.0 + m[1:2]) + m[0:1]
        hi, lo = _split2(h)
        h_scr[...] = hi
        dt_ref[...] = _dot(hi, wdh_ref[...]) + _dot(lo, wdh_ref[...]) + _dot(hi, wdl_ref[...])

    o_ref[...] = _dot(h_scr[...], w_ref[...]).astype(o_ref.dtype)


def _inproj(x2, mod3, w_main, wd_hi, wd_lo, seq):
    t, d = x2.shape
    n = w_main.shape[1]
    bm = min(1024, seq)
    bn = 1024
    per_batch = seq // bm
    return pl.pallas_call(
        _inproj_kernel,
        grid=(t // bm, n // bn),
        in_specs=[pl.BlockSpec((bm, d), lambda i, j: (i, 0)),
                  pl.BlockSpec((1, 6, d), lambda i, j: (i // per_batch, 0, 0)),
                  pl.BlockSpec((d, bn), lambda i, j: (0, j)),
                  pl.BlockSpec((d, LANES), lambda i, j: (0, 0)),
                  pl.BlockSpec((d, LANES), lambda i, j: (0, 0))],
        out_specs=[pl.BlockSpec((bm, bn), lambda i, j: (i, j)),
                   pl.BlockSpec((bm, LANES), lambda i, j: (i, 0))],
        out_shape=[jax.ShapeDtypeStruct((t, n), BF16),
                   jax.ShapeDtypeStruct((t, LANES), F32)],
        scratch_shapes=[pltpu.VMEM((bm, d), BF16)],
        compiler_params=_params(("parallel", "arbitrary")),
        name="inproj",
    )(x2, mod3, w_main, wd_hi, wd_lo)


def _shifted_rows(u, prev_row, next_row):
    rows = u.shape[0]
    ridx = lax.broadcasted_iota(jnp.int32, u.shape, 0)
    up = jnp.where(ridx == 0, prev_row, pltpu.roll(u, 1, axis=0))
    un = jnp.where(ridx == rows - 1, next_row, pltpu.roll(u, rows - 1, axis=0))
    return up, un


def _halo_specs(bm, bw, col_of, t):
    per = bm // HALO_ROWS
    last = t // HALO_ROWS - 1
    prev = pl.BlockSpec((HALO_ROWS, bw), lambda i, j: (jnp.maximum(i * per - 1, 0), col_of(j)))
    nxt = pl.BlockSpec((HALO_ROWS, bw), lambda i, j: (jnp.minimum((i + 1) * per, last), col_of(j)))
    return prev, nxt


def _conv_kernel(u_ref, p_ref, n_ref, w_ref, b_ref, o_ref, *, bm, seq):
    i = pl.program_id(0)
    u = u_ref[...].astype(F32)
    has_prev = ((i * bm) % seq != 0).astype(F32)
    has_next = (((i + 1) * bm) % seq != 0).astype(F32)
    up, un = _shifted_rows(u, p_ref[HALO_ROWS - 1:HALO_ROWS, :].astype(F32) * has_prev,
                           n_ref[0:1, :].astype(F32) * has_next)
    w = w_ref[...]
    o_ref[...] = _silu(up * w[0:1] + u * w[1:2] + un * w[2:3] + b_ref[...])


def _conv(proj, conv_w, conv_b, seq):
    t = proj.shape[0]
    bm = min(512, seq)
    bw = 1024
    col0 = XBC_COL0 // bw
    col_of = lambda j: col0 + j
    prev, nxt = _halo_specs(bm, bw, col_of, t)
    return pl.pallas_call(
        functools.partial(_conv_kernel, bm=bm, seq=seq),
        grid=(t // bm, XBC_WIDTH // bw),
        in_specs=[pl.BlockSpec((bm, bw), lambda i, j: (i, col_of(j))), prev, nxt,
                  pl.BlockSpec((3, bw), lambda i, j: (0, j)),
                  pl.BlockSpec((1, bw), lambda i, j: (0, j))],
        out_specs=pl.BlockSpec((bm, bw), lambda i, j: (i, j)),
        out_shape=jax.ShapeDtypeStruct((t, XBC_WIDTH), F32),
        compiler_params=_params(("parallel", "parallel")),
        name="ssdconv",
    )(proj, proj, proj, conv_w, conv_b)


def _ssd_direction(x_ref, b_ref, c_ref, dt_ref, e_ref, h_scr, y_ref, bias, a, dskip, backward):
    cs_len = SSD_CHUNK
    row = lax.broadcasted_iota(jnp.int32, (cs_len, cs_len), 0)
    col = lax.broadcasted_iota(jnp.int32, (cs_len, cs_len), 1)
    tri = (col <= row).astype(BF16)
    dt = _softplus(dt_ref[...] + bias)
    adt = dt * a
    h3 = _split3(adt)
    cs = _dot(tri, h3[0]) + _dot(tri, h3[1]) + _dot(tri, h3[2])
    tot = cs[cs_len - 1:cs_len, :]
    if backward:
        ecs = cs - adt
        p = -ecs
        wst = dt * jnp.exp(ecs)
        indec = jnp.exp(tot - ecs)
        mask = col >= row
    else:
        p = cs
        wst = dt * jnp.exp(tot - cs)
        indec = jnp.exp(cs)
        mask = col <= row
    pt = p.T
    e = e_ref[...]
    dtx = _dot(dt.astype(BF16), e)
    wstx = _dot(wst.astype(BF16), e)
    indx = _dot(indec.astype(BF16), e)
    xs = x_ref[...]
    xd = (xs * dtx).astype(BF16)
    xw = (xs * wstx).astype(BF16)
    col_off = SSD_HEADS if backward else 0
    pair = 2 * SSD_HEADDIM
    first_of_pair = lax.broadcasted_iota(jnp.int32, (cs_len, pair), 1) < SSD_HEADDIM
    zero = jnp.zeros((), BF16)
    for g in range(SSD_GROUPS):
        gs = slice(g * GROUP_WIDTH, (g + 1) * GROUP_WIDTH)
        bg = b_ref[:, g * SSD_STATE:(g + 1) * SSD_STATE]
        cg = c_ref[:, g * SSD_STATE:(g + 1) * SSD_STATE].astype(BF16)
        cb = _dot_nt(cg, bg.astype(BF16))
        h_in = h_scr[g]
        y_off = _dot(cg, h_in.astype(BF16)) * indx[:, gs]
        st = _dot(bg.T.astype(BF16), xw[:, gs])
        for r in range(0, HEADS_PER_GROUP, 2):
            hd = g * HEADS_PER_GROUP + r
            ms = []
            for ci in (hd + col_off, hd + col_off + 1):
                seg = p[:, ci:ci + 1] - pt[ci:ci + 1, :]
                lmat = jnp.exp(jnp.where(mask, seg, -jnp.inf))
                ms.append((cb * lmat).astype(BF16))
            hs = slice(hd * SSD_HEADDIM, (hd + 2) * SSD_HEADDIM)
            x2h = xd[:, hs]
            rhs = jnp.concatenate([jnp.where(first_of_pair, x2h, zero), jnp.where(first_of_pair, zero, x2h)], axis=0)
            y = _dot(jnp.concatenate(ms, axis=1), rhs) + y_off[:, r * SSD_HEADDIM:(r + 2) * SSD_HEADDIM]
            if dskip is not None:
                y = y + xs[:, hs] * dskip[:, hs]
            y_ref[:, hs] = y
        edge = 0 if backward else cs_len - 1
        h_scr[g] = indx[edge:edge + 1, gs] * h_in + st


def _ssd_kernel(xf_ref, bf_ref, cf_ref, dtf_ref, xb_ref, bb_ref, cb_ref, dtb_ref,
                bias_ref, alog_ref, dskip_ref, ef_ref, eb_ref, yf_ref, yb_ref, hf_scr, hb_scr):
    @pl.when(pl.program_id(1) == 0)
    def _():
        hf_scr[...] = jnp.zeros_like(hf_scr)
        hb_scr[...] = jnp.zeros_like(hb_scr)

    bias = bias_ref[...]
    a = -jnp.exp(alog_ref[...])
    _ssd_direction(xf_ref, bf_ref, cf_ref, dtf_ref, ef_ref, hf_scr, yf_ref, bias, a, dskip_ref[...], False)
    _ssd_direction(xb_ref, bb_ref, cb_ref, dtb_ref, eb_ref, hb_scr, yb_ref, bias, a, None, True)


def _ssd(xbc, dt_raw, bias, alog, dskip_x, e_f, e_b, batch, seq):
    t = xbc.shape[0]
    nc = seq // SSD_CHUNK
    cl = SSD_CHUNK
    gn = SSD_GROUPS * SSD_STATE
    bcol = SSD_WIDTH // gn
    fwd = lambda b, k: b * nc + k
    bwd = lambda b, k: b * nc + (nc - 1 - k)

    def specs(ch):
        return [pl.BlockSpec((cl, SSD_WIDTH), lambda b, k: (ch(b, k), 0)),
                pl.BlockSpec((cl, gn), lambda b, k: (ch(b, k), bcol)),
                pl.BlockSpec((cl, gn), lambda b, k: (ch(b, k), bcol + 1)),
                pl.BlockSpec((cl, LANES), lambda b, k: (ch(b, k), 0))]

    const = lambda shape: pl.BlockSpec(shape, lambda b, k: (0, 0))
    state = pltpu.VMEM((SSD_GROUPS, SSD_STATE, GROUP_WIDTH), F32)
    return pl.pallas_call(
        _ssd_kernel,
        grid=(batch, nc),
        in_specs=specs(fwd) + specs(bwd) + [const((1, LANES)), const((1, LANES)), const((1, SSD_WIDTH)),
                                            const((LANES, SSD_WIDTH)), const((LANES, SSD_WIDTH))],
        out_specs=[pl.BlockSpec((cl, SSD_WIDTH), lambda b, k: (fwd(b, k), 0)),
                   pl.BlockSpec((cl, SSD_WIDTH), lambda b, k: (bwd(b, k), 0))],
        out_shape=[jax.ShapeDtypeStruct((t, SSD_WIDTH), F32)] * 2,
        scratch_shapes=[state, state],
        compiler_params=_params(("parallel", "arbitrary")),
        name="ssd",
    )(xbc, xbc, xbc, dt_raw, xbc, xbc, xbc, dt_raw, bias, alog, dskip_x, e_f, e_b)


def _group_rms(y, r_ref, e_ref, group_size):
    gsum = _dot((y * y).astype(BF16), r_ref[...])
    inv = lax.rsqrt(gsum * (1.0 / group_size) + NORM_EPS)
    return y * _dot(inv.astype(BF16), e_ref[...])


def _layer_norm(v, g, b):
    mu = jnp.mean(v, axis=-1, keepdims=True)
    vc = v - mu
    var = jnp.mean(vc * vc, axis=-1, keepdims=True)
    return vc * lax.rsqrt(var + NORM_EPS) * g + b


def _post_kernel(yf_ref, yb_ref, z_ref, gb_ref, gc_ref, gcp_ref, gcn_ref, v_ref, vp_ref, vn_ref,
                 x_ref, mod_ref, scw_ref, ssdg_ref, scg_ref, r4_ref, e4_ref, r32_ref, e32_ref,
                 wout_ref, lng_ref, lnb_ref, x1_ref, h2_ref, amax_ref, sumsq_ref, *, bm, seq, alpha):
    i = pl.program_id(0)
    m = mod_ref[0]
    y = (yf_ref[...] + yb_ref[...]) * _silu(z_ref[...].astype(F32))
    y_ssd = (_group_rms(y, r4_ref, e4_ref, SSD_WIDTH // SSD_GROUPS) * ssdg_ref[...]).astype(BF16)
    n_cols = [slice(c, c + POST_COL_PIECE) for c in range(0, wout_ref.shape[1], POST_COL_PIECE)]
    mix_ssd = [_dot(y_ssd, wout_ref[0:SSD_WIDTH, cs]) for cs in n_cols]

    f32 = lambda ref, rows=slice(None): ref[rows, :].astype(F32)
    u = f32(gc_ref) * f32(v_ref)
    has_prev = ((i * bm) % seq != 0).astype(F32)
    has_next = (((i + 1) * bm) % seq != 0).astype(F32)
    last = slice(HALO_ROWS - 1, HALO_ROWS)
    first = slice(0, 1)
    up, un = _shifted_rows(u, f32(gcp_ref, last) * f32(vp_ref, last) * has_prev,
                           f32(gcn_ref, first) * f32(vn_ref, first) * has_next)
    w = scw_ref[...]
    y_sc = f32(gb_ref) * (up * w[0:1] + u * w[1:2] + un * w[2:3])
    y_sc = (_group_rms(y_sc, r32_ref, e32_ref, SC_WIDTH // SC_GROUPS) * scg_ref[...]).astype(BF16)
    mix = jnp.concatenate([part + _dot(y_sc, wout_ref[SSD_WIDTH:SSD_WIDTH + SC_WIDTH, cs])
                           for part, cs in zip(mix_ssd, n_cols)], axis=1)
    x1 = _layer_norm(alpha * x_ref[...] + m[2:3] * mix, lng_ref[...], lnb_ref[...])
    x1_ref[...] = x1
    h2 = x1 * (1.0 + m[4:5]) + m[3:4]
    h2_ref[...] = h2.astype(BF16)
    amax_ref[...] = jnp.max(jnp.abs(h2), axis=-1, keepdims=True)
    sumsq_ref[...] = jnp.sum(h2 * h2, axis=-1, keepdims=True)


def _single(shape, index_map):
    return pl.BlockSpec(shape, index_map, pipeline_mode=pl.Buffered(1))


def _post(yf, yb, proj, x2, mod3, scw, ssdg, scg, r4, e4, r32, e32, wout, lng, lnb, seq, alpha):
    t, d = x2.shape
    bm = min(256, seq)
    per_batch = seq // bm
    per = bm // HALO_ROWS
    last = t // HALO_ROWS - 1
    main = lambda c: pl.BlockSpec((bm, d), lambda i: (i, c))
    prev = lambda c: pl.BlockSpec((HALO_ROWS, d), lambda i: (jnp.maximum(i * per - 1, 0), c))
    nxt = lambda c: pl.BlockSpec((HALO_ROWS, d), lambda i: (jnp.minimum((i + 1) * per, last), c))
    const = lambda shape: _single(shape, lambda i: (0, 0))
    return pl.pallas_call(
        functools.partial(_post_kernel, bm=bm, seq=seq, alpha=alpha),
        grid=(t // bm,),
        in_specs=[main(0), main(0), main(0), main(1), main(2), prev(2), nxt(2), main(3), prev(3), nxt(3),
                  main(0), pl.BlockSpec((1, 6, d), lambda i: (i // per_batch, 0, 0)),
                  const((3, d)), const((1, d)), const((1, d)),
                  const((d, LANES)), const((LANES, d)), const((d, LANES)), const((LANES, d)),
                  const((2 * d, d)), const((1, d)), const((1, d))],
        out_specs=[pl.BlockSpec((bm, d), lambda i: (i, 0)), pl.BlockSpec((bm, d), lambda i: (i, 0)),
                   pl.BlockSpec((bm, 1), lambda i: (i, 0)), pl.BlockSpec((bm, 1), lambda i: (i, 0))],
        out_shape=[jax.ShapeDtypeStruct((t, d), F32), jax.ShapeDtypeStruct((t, d), BF16),
                   jax.ShapeDtypeStruct((t, 1), F32), jax.ShapeDtypeStruct((t, 1), F32)],
        compiler_params=_params(("parallel",)),
        name="post",
    )(yf, yb, proj, proj, proj, proj, proj, proj, proj, proj, x2, mod3,
      scw, ssdg, scg, r4, e4, r32, e32, wout, lng, lnb)


def _top_values(s, n, want_rank=False):
    vals = []
    rank = jnp.full(s.shape, float(n), F32) if want_rank else None
    for r in range(n):
        m = jnp.max(s, axis=0, keepdims=True)
        vals.append(m)
        hit = s == m
        if want_rank:
            rank = jnp.where(hit, float(r), rank)
        s = jnp.where(hit, -jnp.inf, s)
    return (vals, rank) if want_rank else vals


def _query_kernel(h2_ref, wq_ref, k_ref, r2_ref, e2_ref, n1_ref, c1_ref):
    tb = h2_ref.shape[0]
    for h in range(PEER_HEADS):
        q = _dot(h2_ref[...], wq_ref[:, h * D_QUERY:(h + 1) * D_QUERY]).astype(BF16)
        s1 = _dot_nt(k_ref[h, 0], q[:, :HALF_QUERY])
        s2 = _dot_nt(k_ref[h, 1], q[:, HALF_QUERY:])
        v1 = _top_values(s1, PEER_TOPK)
        v2, r2 = _top_values(s2, PEER_TOPK, want_rank=True)
        pairs = [(a, b) for a in range(PEER_TOPK) for b in range(PEER_TOPK) if (a + 1) * (b + 1) <= PEER_TOPK]
        sums = {ab: v1[ab[0]] + v2[ab[1]] for ab in pairs}
        rows = [sums[ab] for ab in pairs]
        rows += [jnp.full((1, tb), -jnp.inf, F32)] * ((-len(rows)) % SUBLANES)
        top = _top_values(jnp.concatenate(rows, axis=0), PEER_TOPK)
        z = jnp.ones((1, tb), F32)
        for kk in range(1, PEER_TOPK):
            z = z + jnp.exp(top[kk] - top[0])
        tau = top[PEER_TOPK - 1]
        n1 = jnp.zeros(s1.shape, F32)
        for a in range(PEER_TOPK):
            cnt = sum(jnp.where(sums[(a, b)] >= tau, 1.0, 0.0) for b in range(PEER_TOPK) if (a, b) in sums)
            n1 = jnp.where(s1 == v1[a], cnt, n1)
        r2_ref[h] = r2.astype(BF16)
        e2_ref[h] = jnp.exp(s2 - v2[0]).astype(BF16)
        n1_ref[h] = n1
        c1_ref[h] = jnp.exp(s1 - v1[0]) * (1.0 / z)


def _query(h2, wq, keys):
    t, d = h2.shape
    tb = min(256, t)
    ospec = pl.BlockSpec((PEER_HEADS, N_KEYS, tb), lambda i: (0, 0, i))
    return pl.pallas_call(
        _query_kernel,
        grid=(t // tb,),
        in_specs=[pl.BlockSpec((tb, d), lambda i: (i, 0)),
                  _single((d, PEER_HEADS * D_QUERY), lambda i: (0, 0)),
                  _single((PEER_HEADS, 2, N_KEYS, HALF_QUERY), lambda i: (0, 0, 0, 0))],
        out_specs=[ospec] * 4,
        out_shape=[jax.ShapeDtypeStruct((PEER_HEADS, N_KEYS, t), dt) for dt in (BF16, BF16, F32, F32)],
        compiler_params=_params(("parallel",)),
        name="peerquery",
    )(h2, wq, keys)


def _pow2_scale(bound):
    ratio = FP8_TARGET / jnp.where(bound > 0, bound, FP8_TARGET)
    exponent_only = lax.bitcast_convert_type(ratio, jnp.int32) & jnp.int32(0x7F800000)
    return jnp.minimum(lax.bitcast_convert_type(exponent_only, F32), FP8_SCALE_CAP)


def _expert_prep_kernel(w_ref, q_ref, inv_ref, sumsq_ref, *, transpose):
    w = w_ref[...]
    s = _pow2_scale(jnp.max(jnp.abs(w), axis=1, keepdims=True))
    inv_ref[...] = 1.0 / s
    sumsq_ref[...] = jnp.sum(w * w, axis=1, keepdims=True)
    ws = w * s
    q_ref[...] = (ws.T if transpose else ws).astype(FP8)


def _expert_prep(table, transpose):
    n_exp, d = table.shape
    rows = 512
    col = pl.BlockSpec((rows, 1), lambda i: (i, 0))
    q_spec = pl.BlockSpec((d, rows), lambda i: (0, i)) if transpose else pl.BlockSpec((rows, d), lambda i: (i, 0))
    q_shape = (d, n_exp) if transpose else (n_exp, d)
    return pl.pallas_call(
        functools.partial(_expert_prep_kernel, transpose=transpose),
        grid=(n_exp // rows,),
        in_specs=[pl.BlockSpec((rows, d), lambda i: (i, 0))],
        out_specs=[q_spec, col, col],
        out_shape=[jax.ShapeDtypeStruct(q_shape, FP8), jax.ShapeDtypeStruct((n_exp, 1), F32),
                   jax.ShapeDtypeStruct((n_exp, 1), F32)],
        compiler_params=_params(("parallel",)),
        name="expertprep_t" if transpose else "expertprep",
    )(table)


def _peer_kernel(h2_ref, u_ref, vt_ref, r2_ref, e2_ref, n1_ref, c1_ref, x1_ref, mod_ref, lng_ref, lnb_ref,
                 iu_ref, iv_ref, ish_ref, sp_ref, sh_ref, isp_ref, o_ref, acc_ref, h8_ref, *, alpha):
    e = pl.program_id(1)

    @pl.when(e == 0)
    def _():
        acc_ref[...] = jnp.zeros_like(acc_ref)
        h8_ref[...] = (h2_ref[...].astype(F32) * sh_ref[...]).T.astype(FP8)

    n1 = n1_ref[...].astype(BF16)
    c1 = c1_ref[...].astype(BF16)
    parts = []
    for ii in range(u_ref.shape[0] // N_KEYS):
        rows = slice(ii * N_KEYS, (ii + 1) * N_KEYS)
        at = _dot(u_ref[rows, :], h8_ref[...])
        g = None
        for h in range(PEER_HEADS):
            w = jnp.where(r2_ref[h] < n1[h, ii:ii + 1, :], e2_ref[h], jnp.zeros((), BF16)) * c1[h, ii:ii + 1, :]
            g = w if g is None else g + w
        b = at * (iu_ref[rows, :] * ish_ref[...])
        p = b * (1.0 + lax.erf(b)) * (iv_ref[rows, :] * sp_ref[...])
        parts.append((p.astype(BF16) * g).astype(FP8))
    pt = jnp.concatenate(parts, axis=0)
    for c in range(0, vt_ref.shape[0], PEER_OUT_PIECE):
        dr = slice(c, c + PEER_OUT_PIECE)
        acc_ref[dr, :] += _dot(vt_ref[dr, :], pt)

    @pl.when(e == pl.num_programs(1) - 1)
    def _():
        m = mod_ref[0]
        ffn = acc_ref[...].T * isp_ref[...]
        o_ref[...] = _layer_norm(alpha * x1_ref[...] + m[5:6] * ffn, lng_ref[...], lnb_ref[...])


def _peer(h2, u8, vt8, r2, e2, n1, c1, x1, mod3, lng, lnb, iu, iv, ish_blk, sp_blk, sh, isp, seq, alpha):
    t, d = h2.shape
    n_exp = u8.shape[0]
    tb = min(PEER_TOKEN_BLOCK, seq)
    eb = 1024
    per_batch = seq // tb
    kb = eb // N_KEYS
    full = _single((PEER_HEADS, N_KEYS, tb), lambda i, j: (0, 0, i))
    part = pl.BlockSpec((PEER_HEADS, kb, tb), lambda i, j: (0, j, i))
    ecol = pl.BlockSpec((eb, 1), lambda i, j: (j, 0))
    blk = _single((None, 1, 1), lambda i, j: (i, 0, 0))
    tcol = _single((tb, 1), lambda i, j: (i, 0))
    return pl.pallas_call(
        functools.partial(_peer_kernel, alpha=alpha),
        grid=(t // tb, n_exp // eb),
        in_specs=[_single((tb, d), lambda i, j: (i, 0)),
                  pl.BlockSpec((eb, d), lambda i, j: (j, 0)),
                  pl.BlockSpec((d, eb), lambda i, j: (0, j)),
                  full, full, part, part,
                  _single((tb, d), lambda i, j: (i, 0)),
                  pl.BlockSpec((1, 6, d), lambda i, j: (i // per_batch, 0, 0)),
                  pl.BlockSpec((1, d), lambda i, j: (0, 0)),
                  pl.BlockSpec((1, d), lambda i, j: (0, 0)),
                  ecol, ecol, blk, blk, tcol, tcol],
        out_specs=pl.BlockSpec((tb, d), lambda i, j: (i, 0)),
        out_shape=jax.ShapeDtypeStruct((t, d), F32),
        scratch_shapes=[pltpu.VMEM((d, tb), F32), pltpu.VMEM((d, tb), FP8)],
        compiler_params=_params(("parallel", "arbitrary")),
        name="peer",
    )(h2, u8, vt8, r2, e2, n1, c1, x1, mod3, lng, lnb, iu, iv, ish_blk, sp_blk, sh, isp)


def _ffn(h2, h_amax, h_sumsq, x1, mod3, w_query, sub_keys, expert_u, expert_v, ln2_g, ln2_b, seq, alpha):
    t = h2.shape[0]
    r2, e2, n1, c1 = _query(h2, w_query.astype(BF16), sub_keys.astype(BF16))
    u8, inv_su, u_sumsq = _expert_prep(expert_u, transpose=False)
    vt8, inv_sv, _ = _expert_prep(expert_v, transpose=True)
    tb = min(PEER_TOKEN_BLOCK, seq)
    per_block = lambda v: jnp.max(v.reshape(t // tb, tb), axis=1).reshape(-1, 1, 1)
    sh = _pow2_scale(per_block(h_amax))
    sp = _pow2_scale(jnp.sqrt(jnp.max(u_sumsq) * per_block(h_sumsq)) * (PEER_HEADS * jnp.max(inv_sv)))
    sqrt_half = np.float32(math.sqrt(0.5))
    per_token = lambda s: jnp.broadcast_to(s, (t // tb, tb, 1)).reshape(t, 1)
    return _peer(h2, u8, vt8, r2, e2, n1, c1, x1, mod3, _row(ln2_g), _row(ln2_b), inv_su * sqrt_half,
                 inv_sv * sqrt_half, 1.0 / sh, sp, per_token(sh), per_token(1.0 / sp), seq, alpha)


def _one_hot_cols(n_rows, n_cols, group_size, row_offset=0):
    r = np.arange(n_rows)[:, None]
    c = np.arange(n_cols)[None, :]
    return jnp.asarray(r == row_offset + c // group_size, dtype=BF16)


def _row(v):
    return v.reshape(1, -1).astype(F32)


def _pad_lanes(v):
    return jnp.pad(v, (0, LANES - v.shape[0])).reshape(1, LANES).astype(F32)


def kernel(x, c, w_ada, b_ada, w_in, conv_ssd_w, conv_ssd_b, dt_bias_f, dt_bias_b, a_log_f, a_log_b, d_skip, ssd_norm_g, short_conv_w, sc_norm_g, w_out, ln1_g, ln1_b, w_query, sub_keys, expert_u, expert_v, ln2_g, ln2_b):
    batch, seq, d = x.shape
    depth = w_ada.shape[0]
    alpha = (2.0 * depth) ** 0.25
    t = batch * seq
    x2 = x.reshape(t, d)
    c_pad = jnp.pad(c, ((0, SUBLANES - batch % SUBLANES), (0, 0))) if batch % SUBLANES else c

    e_f = _one_hot_cols(LANES, SSD_WIDTH, SSD_HEADDIM, 0)
    e_b = _one_hot_cols(LANES, SSD_WIDTH, SSD_HEADDIM, SSD_HEADS)
    e4 = _one_hot_cols(LANES, SSD_WIDTH, SSD_WIDTH // SSD_GROUPS)
    e32 = _one_hot_cols(LANES, SC_WIDTH, SC_WIDTH // SC_GROUPS)
    r4 = e4.T
    r32 = e32.T

    for i in range(depth):
        mod = _ada(c_pad, w_ada[i], b_ada[i].reshape(1, -1))
        mod3 = mod[:batch].reshape(batch, 6, d)

        w_main, wd_hi, wd_lo = _win_prep(w_in[i])
        proj, dt_raw = _inproj(x2, mod3, w_main, wd_hi, wd_lo, seq)

        xbc = _conv(proj, conv_ssd_w[i], conv_ssd_b[i].reshape(1, -1), seq)
        bias = _pad_lanes(jnp.concatenate([dt_bias_f[i], dt_bias_b[i]]))
        alog = _pad_lanes(jnp.concatenate([a_log_f[i], a_log_b[i]]))
        dskip_x = _row(jnp.repeat(d_skip[i], SSD_HEADDIM))
        y_f, y_b = _ssd(xbc, dt_raw, bias, alog, dskip_x, e_f, e_b, batch, seq)

        x1, h2, h_amax, h_sumsq = _post(y_f, y_b, proj, x2, mod3, short_conv_w[i], _row(ssd_norm_g[i]),
                                        _row(sc_norm_g[i]), r4, e4, r32, e32, w_out[i].astype(BF16),
                                        _row(ln1_g[i]), _row(ln1_b[i]), seq, alpha)

        x2 = _ffn(h2, h_amax, h_sumsq, x1, mod3, w_query[i], sub_keys[i], expert_u[i], expert_v[i],
                  ln2_g[i], ln2_b[i], seq, alpha)
    return x2.reshape(batch, seq, d)
```

```python
import functools
import math

import jax
import jax.numpy as jnp
import numpy as np
from jax import lax
from jax.experimental import pallas as pl
from jax.experimental.pallas import tpu as pltpu

F32 = jnp.float32
BF16 = jnp.bfloat16
FP8 = jnp.float8_e4m3fn
FP8_TARGET = 256.0
FP8_SCALE_CAP = 2.0 ** 60

D_MODEL = 2048
SSD_WIDTH = D_MODEL
SSD_HEADDIM = 64
SSD_HEADS = SSD_WIDTH // SSD_HEADDIM
SSD_GROUPS = 4
SSD_STATE = 128
SSD_CHUNK = 128
HEADS_PER_GROUP = SSD_HEADS // SSD_GROUPS
GROUP_WIDTH = HEADS_PER_GROUP * SSD_HEADDIM
SC_WIDTH = D_MODEL
SC_GROUPS = 32
XBC_WIDTH = SSD_WIDTH + 2 * SSD_GROUPS * SSD_STATE
MAIN_COLS = 4 * D_MODEL + XBC_WIDTH
XBC_COL0 = 4 * D_MODEL
PEER_HEADS = 8
N_KEYS = 128
PEER_TOPK = 16
D_QUERY = 512
HALF_QUERY = D_QUERY // 2
NORM_EPS = 1e-5
LANES = 128
SUBLANES = 8
HALO_ROWS = 16
POST_COL_PIECE = 256
PEER_TOKEN_BLOCK = 512
PEER_OUT_PIECE = 256
VMEM_LIMIT = 56 * 1024 * 1024


def _dot(a, b):
    return jnp.dot(a, b, preferred_element_type=F32)


def _dot_nt(a, b):
    return lax.dot_general(a, b, (((1,), (1,)), ((), ())), preferred_element_type=F32)


def _split2(x):
    hi = x.astype(BF16)
    lo = (x - hi.astype(F32)).astype(BF16)
    return hi, lo


def _split3(x):
    hi = x.astype(BF16)
    r = x - hi.astype(F32)
    mid = r.astype(BF16)
    lo = (r - mid.astype(F32)).astype(BF16)
    return hi, mid, lo


def _softplus(x):
    return jnp.maximum(x, 0.0) + jnp.log1p(jnp.exp(-jnp.abs(x)))


def _silu(x):
    return x * (1.0 / (1.0 + jnp.exp(-x)))


def _params(sem):
    return pltpu.CompilerParams(dimension_semantics=sem, vmem_limit_bytes=VMEM_LIMIT)


def _ada_kernel(c_ref, w_ref, b_ref, o_ref):
    sc = _silu(c_ref[...])
    o_ref[...] = _dot(sc.astype(BF16), w_ref[...].astype(BF16)) + b_ref[...]


def _ada(c_pad, w, b):
    rows, d = c_pad.shape
    n = w.shape[1]
    bn = 1024
    return pl.pallas_call(
        _ada_kernel,
        grid=(n // bn,),
        in_specs=[pl.BlockSpec((rows, d), lambda j: (0, 0)),
                  pl.BlockSpec((d, bn), lambda j: (0, j)),
                  pl.BlockSpec((1, bn), lambda j: (0, j))],
        out_specs=pl.BlockSpec((rows, bn), lambda j: (0, j)),
        out_shape=jax.ShapeDtypeStruct((rows, n), F32),
        compiler_params=_params(("arbitrary",)),
        name="ada",
    )(c_pad, w, b)


DT_COL0 = SSD_WIDTH + XBC_WIDTH
GATES_COL0 = DT_COL0 + 2 * SSD_HEADS


def _win_prep_kernel(w_ref, main_ref, dh_ref, dl_ref):
    w = w_ref[...]
    main_ref[:, 0:SSD_WIDTH] = w[:, 0:SSD_WIDTH].astype(BF16)
    main_ref[:, SSD_WIDTH:XBC_COL0] = w[:, GATES_COL0:GATES_COL0 + 3 * SC_WIDTH].astype(BF16)
    main_ref[:, XBC_COL0:MAIN_COLS] = w[:, SSD_WIDTH:DT_COL0].astype(BF16)
    dt = w[:, DT_COL0:DT_COL0 + LANES]
    lane = lax.broadcasted_iota(jnp.int32, dt.shape, 1)
    hi, lo = _split2(jnp.where(lane < 2 * SSD_HEADS, dt, 0.0))
    dh_ref[...] = hi
    dl_ref[...] = lo


def _win_prep(w):
    d, n = w.shape
    rows = 128
    return pl.pallas_call(
        _win_prep_kernel,
        grid=(d // rows,),
        in_specs=[pl.BlockSpec((rows, n), lambda i: (i, 0))],
        out_specs=[pl.BlockSpec((rows, MAIN_COLS), lambda i: (i, 0)),
                   pl.BlockSpec((rows, LANES), lambda i: (i, 0)),
                   pl.BlockSpec((rows, LANES), lambda i: (i, 0))],
        out_shape=[jax.ShapeDtypeStruct((d, MAIN_COLS), BF16), jax.ShapeDtypeStruct((d, LANES), BF16),
                   jax.ShapeDtypeStruct((d, LANES), BF16)],
        compiler_params=_params(("parallel",)),
        name="winprep",
    )(w)


def _inproj_kernel(x_ref, mod_ref, w_ref, wdh_ref, wdl_ref, o_ref, dt_ref, h_scr):
    @pl.when(pl.program_id(1) == 0)
    def _():
        m = mod_ref[0]
        h = x_ref[...] * (1.0 + m[1:2]) + m[0:1]
        hi, lo = _split2(h)
        h_scr[...] = hi
        dt_ref[...] = _dot(hi, wdh_ref[...]) + _dot(lo, wdh_ref[...]) + _dot(hi, wdl_ref[...])

    o_ref[...] = _dot(h_scr[...], w_ref[...]).astype(o_ref.dtype)


def _inproj(x2, mod3, w_main, wd_hi, wd_lo, seq):
    t, d = x2.shape
    n = w_main.shape[1]
    bm = min(1024, seq)
    bn = 1024
    per_batch = seq // bm
    return pl.pallas_call(
        _inproj_kernel,
        grid=(t // bm, n // bn),
        in_specs=[pl.BlockSpec((bm, d), lambda i, j: (i, 0)),
                  pl.BlockSpec((1, 6, d), lambda i, j: (i // per_batch, 0, 0)),
                  pl.BlockSpec((d, bn), lambda i, j: (0, j)),
                  pl.BlockSpec((d, LANES), lambda i, j: (0, 0)),
                  pl.BlockSpec((d, LANES), lambda i, j: (0, 0))],
        out_specs=[pl.BlockSpec((bm, bn), lambda i, j: (i, j)),
                   pl.BlockSpec((bm, LANES), lambda i, j: (i, 0))],
        out_shape=[jax.ShapeDtypeStruct((t, n), BF16),
                   jax.ShapeDtypeStruct((t, LANES), F32)],
        scratch_shapes=[pltpu.VMEM((bm, d), BF16)],
        compiler_params=_params(("parallel", "arbitrary")),
        name="inproj",
    )(x2, mod3, w_main, wd_hi, wd_lo)


def _shifted_rows(u, prev_row, next_row):
    rows = u.shape[0]
    ridx = lax.broadcasted_iota(jnp.int32, u.shape, 0)
    up = jnp.where(ridx == 0, prev_row, pltpu.roll(u, 1, axis=0))
    un = jnp.where(ridx == rows - 1, next_row, pltpu.roll(u, rows - 1, axis=0))
    return up, un


def _halo_specs(bm, bw, col_of, t):
    per = bm // HALO_ROWS
    last = t // HALO_ROWS - 1
    prev = pl.BlockSpec((HALO_ROWS, bw), lambda i, j: (jnp.maximum(i * per - 1, 0), col_of(j)))
    nxt = pl.BlockSpec((HALO_ROWS, bw), lambda i, j: (jnp.minimum((i + 1) * per, last), col_of(j)))
    return prev, nxt


def _conv_kernel(u_ref, p_ref, n_ref, w_ref, b_ref, o_ref, *, bm, seq):
    i = pl.program_id(0)
    u = u_ref[...].astype(F32)
    has_prev = ((i * bm) % seq != 0).astype(F32)
    has_next = (((i + 1) * bm) % seq != 0).astype(F32)
    up, un = _shifted_rows(u, p_ref[HALO_ROWS - 1:HALO_ROWS, :].astype(F32) * has_prev,
                           n_ref[0:1, :].astype(F32) * has_next)
    w = w_ref[...]
    o_ref[...] = _silu(up * w[0:1] + u * w[1:2] + un * w[2:3] + b_ref[...])


def _conv(proj, conv_w, conv_b, seq):
    t = proj.shape[0]
    bm = min(512, seq)
    bw = 1024
    col0 = XBC_COL0 // bw
    col_of = lambda j: col0 + j
    prev, nxt = _halo_specs(bm, bw, col_of, t)
    return pl.pallas_call(
        functools.partial(_conv_kernel, bm=bm, seq=seq),
        grid=(t // bm, XBC_WIDTH // bw),
        in_specs=[pl.BlockSpec((bm, bw), lambda i, j: (i, col_of(j))), prev, nxt,
                  pl.BlockSpec((3, bw), lambda i, j: (0, j)),
                  pl.BlockSpec((1, bw), lambda i, j: (0, j))],
        out_specs=pl.BlockSpec((bm, bw), lambda i, j: (i, j)),
        out_shape=jax.ShapeDtypeStruct((t, XBC_WIDTH), F32),
        compiler_params=_params(("parallel", "parallel")),
        name="ssdconv",
    )(proj, proj, proj, conv_w, conv_b)


def _ssd_direction(x_ref, b_ref, c_ref, dt_ref, e_ref, h_scr, y_ref, bias, a, dskip, backward):
    cs_len = SSD_CHUNK
    row = lax.broadcasted_iota(jnp.int32, (cs_len, cs_len), 0)
    col = lax.broadcasted_iota(jnp.int32, (cs_len, cs_len), 1)
    tri = (col <= row).astype(BF16)
    dt = _softplus(dt_ref[...] + bias)
    adt = dt * a
    h3 = _split3(adt)
    cs = _dot(tri, h3[0]) + _dot(tri, h3[1]) + _dot(tri, h3[2])
    tot = cs[cs_len - 1:cs_len, :]
    if backward:
        ecs = cs - adt
        p = -ecs
        wst = dt * jnp.exp(ecs)
        indec = jnp.exp(tot - ecs)
        mask = col >= row
    else:
        p = cs
        wst = dt * jnp.exp(tot - cs)
        indec = jnp.exp(cs)
        mask = col <= row
    pt = p.T
    e = e_ref[...]
    dtx = _dot(dt.astype(BF16), e)
    wstx = _dot(wst.astype(BF16), e)
    indx = _dot(indec.astype(BF16), e)
    xs = x_ref[...]
    xd = (xs * dtx).astype(BF16)
    xw = (xs * wstx).astype(BF16)
    col_off = SSD_HEADS if backward else 0
    pair = 2 * SSD_HEADDIM
    first_of_pair = lax.broadcasted_iota(jnp.int32, (cs_len, pair), 1) < SSD_HEADDIM
    zero = jnp.zeros((), BF16)
    for g in range(SSD_GROUPS):
        gs = slice(g * GROUP_WIDTH, (g + 1) * GROUP_WIDTH)
        bg = b_ref[:, g * SSD_STATE:(g + 1) * SSD_STATE]
        cg = c_ref[:, g * SSD_STATE:(g + 1) * SSD_STATE].astype(BF16)
        cb = _dot_nt(cg, bg.astype(BF16))
        h_in = h_scr[g]
        y_off = _dot(cg, h_in.astype(BF16)) * indx[:, gs]
        st = _dot(bg.T.astype(BF16), xw[:, gs])
        for r in range(0, HEADS_PER_GROUP, 2):
            hd = g * HEADS_PER_GROUP + r
            ms = []
            for ci in (hd + col_off, hd + col_off + 1):
                seg = p[:, ci:ci + 1] - pt[ci:ci + 1, :]
                lmat = jnp.exp(jnp.where(mask, seg, -jnp.inf))
                ms.append((cb * lmat).astype(BF16))
            hs = slice(hd * SSD_HEADDIM, (hd + 2) * SSD_HEADDIM)
            x2h = xd[:, hs]
            rhs = jnp.concatenate([jnp.where(first_of_pair, x2h, zero), jnp.where(first_of_pair, zero, x2h)], axis=0)
            y = _dot(jnp.concatenate(ms, axis=1), rhs) + y_off[:, r * SSD_HEADDIM:(r + 2) * SSD_HEADDIM]
            if dskip is not None:
                y = y + xs[:, hs] * dskip[:, hs]
            y_ref[:, hs] = y
        edge = 0 if backward else cs_len - 1
        h_scr[g] = indx[edge:edge + 1, gs] * h_in + st


def _ssd_kernel(xf_ref, bf_ref, cf_ref, dtf_ref, xb_ref, bb_ref, cb_ref, dtb_ref,
                bias_ref, alog_ref, dskip_ref, ef_ref, eb_ref, yf_ref, yb_ref, hf_scr, hb_scr):
    @pl.when(pl.program_id(1) == 0)
    def _():
        hf_scr[...] = jnp.zeros_like(hf_scr)
        hb_scr[...] = jnp.zeros_like(hb_scr)

    bias = bias_ref[...]
    a = -jnp.exp(alog_ref[...])
    _ssd_direction(xf_ref, bf_ref, cf_ref, dtf_ref, ef_ref, hf_scr, yf_ref, bias, a, dskip_ref[...], False)
    _ssd_direction(xb_ref, bb_ref, cb_ref, dtb_ref, eb_ref, hb_scr, yb_ref, bias, a, None, True)


def _ssd(xbc, dt_raw, bias, alog, dskip_x, e_f, e_b, batch, seq):
    t = xbc.shape[0]
    nc = seq // SSD_CHUNK
    cl = SSD_CHUNK
    gn = SSD_GROUPS * SSD_STATE
    bcol = SSD_WIDTH // gn
    fwd = lambda b, k: b * nc + k
    bwd = lambda b, k: b * nc + (nc - 1 - k)

    def specs(ch):
        return [pl.BlockSpec((cl, SSD_WIDTH), lambda b, k: (ch(b, k), 0)),
                pl.BlockSpec((cl, gn), lambda b, k: (ch(b, k), bcol)),
                pl.BlockSpec((cl, gn), lambda b, k: (ch(b, k), bcol + 1)),
                pl.BlockSpec((cl, LANES), lambda b, k: (ch(b, k), 0))]

    const = lambda shape: pl.BlockSpec(shape, lambda b, k: (0, 0))
    state = pltpu.VMEM((SSD_GROUPS, SSD_STATE, GROUP_WIDTH), F32)
    return pl.pallas_call(
        _ssd_kernel,
        grid=(batch, nc),
        in_specs=specs(fwd) + specs(bwd) + [const((1, LANES)), const((1, LANES)), const((1, SSD_WIDTH)),
                                            const((LANES, SSD_WIDTH)), const((LANES, SSD_WIDTH))],
        out_specs=[pl.BlockSpec((cl, SSD_WIDTH), lambda b, k: (fwd(b, k), 0)),
                   pl.BlockSpec((cl, SSD_WIDTH), lambda b, k: (bwd(b, k), 0))],
        out_shape=[jax.ShapeDtypeStruct((t, SSD_WIDTH), F32)] * 2,
        scratch_shapes=[state, state],
        compiler_params=_params(("parallel", "arbitrary")),
        name="ssd",
    )(xbc, xbc, xbc, dt_raw, xbc, xbc, xbc, dt_raw, bias, alog, dskip_x, e_f, e_b)


def _group_rms(y, r_ref, e_ref, group_size):
    gsum = _dot((y * y).astype(BF16), r_ref[...])
    inv = lax.rsqrt(gsum * (1.0 / group_size) + NORM_EPS)
    return y * _dot(inv.astype(BF16), e_ref[...])


def _layer_norm(v, g, b):
    mu = jnp.mean(v, axis=-1, keepdims=True)
    vc = v - mu
    var = jnp.mean(vc * vc, axis=-1, keepdims=True)
    return vc * lax.rsqrt(var + NORM_EPS) * g + b


def _post_kernel(yf_ref, yb_ref, z_ref, gb_ref, gc_ref, gcp_ref, gcn_ref, v_ref, vp_ref, vn_ref,
                 x_ref, mod_ref, scw_ref, ssdg_ref, scg_ref, r4_ref, e4_ref, r32_ref, e32_ref,
                 wout_ref, lng_ref, lnb_ref, x1_ref, h2_ref, amax_ref, sumsq_ref, *, bm, seq, alpha):
    i = pl.program_id(0)
    m = mod_ref[0]
    y = (yf_ref[...] + yb_ref[...]) * _silu(z_ref[...].astype(F32))
    y_ssd = (_group_rms(y, r4_ref, e4_ref, SSD_WIDTH // SSD_GROUPS) * ssdg_ref[...]).astype(BF16)
    n_cols = [slice(c, c + POST_COL_PIECE) for c in range(0, wout_ref.shape[1], POST_COL_PIECE)]
    mix_ssd = [_dot(y_ssd, wout_ref[0:SSD_WIDTH, cs]) for cs in n_cols]

    f32 = lambda ref, rows=slice(None): ref[rows, :].astype(F32)
    u = f32(gc_ref) * f32(v_ref)
    has_prev = ((i * bm) % seq != 0).astype(F32)
    has_next = (((i + 1) * bm) % seq != 0).astype(F32)
    last = slice(HALO_ROWS - 1, HALO_ROWS)
    first = slice(0, 1)
    up, un = _shifted_rows(u, f32(gcp_ref, last) * f32(vp_ref, last) * has_prev,
                           f32(gcn_ref, first) * f32(vn_ref, first) * has_next)
    w = scw_ref[...]
    y_sc = f32(gb_ref) * (up * w[0:1] + u * w[1:2] + un * w[2:3])
    y_sc = (_group_rms(y_sc, r32_ref, e32_ref, SC_WIDTH // SC_GROUPS) * scg_ref[...]).astype(BF16)
    mix = jnp.concatenate([part + _dot(y_sc, wout_ref[SSD_WIDTH:SSD_WIDTH + SC_WIDTH, cs])
                           for part, cs in zip(mix_ssd, n_cols)], axis=1)
    x1 = _layer_norm(alpha * x_ref[...] + m[2:3] * mix, lng_ref[...], lnb_ref[...])
    x1_ref[...] = x1
    h2 = x1 * (1.0 + m[4:5]) + m[3:4]
    h2_ref[...] = h2.astype(BF16)
    amax_ref[...] = jnp.max(jnp.abs(h2), axis=-1, keepdims=True)
    sumsq_ref[...] = jnp.sum(h2 * h2, axis=-1, keepdims=True)


def _single(shape, index_map):
    return pl.BlockSpec(shape, index_map, pipeline_mode=pl.Buffered(1))


def _post(yf, yb, proj, x2, mod3, scw, ssdg, scg, r4, e4, r32, e32, wout, lng, lnb, seq, alpha):
    t, d = x2.shape
    bm = min(256, seq)
    per_batch = seq // bm
    per = bm // HALO_ROWS
    last = t // HALO_ROWS - 1
    main = lambda c: pl.BlockSpec((bm, d), lambda i: (i, c))
    prev = lambda c: pl.BlockSpec((HALO_ROWS, d), lambda i: (jnp.maximum(i * per - 1, 0), c))
    nxt = lambda c: pl.BlockSpec((HALO_ROWS, d), lambda i: (jnp.minimum((i + 1) * per, last), c))
    const = lambda shape: _single(shape, lambda i: (0, 0))
    return pl.pallas_call(
        functools.partial(_post_kernel, bm=bm, seq=seq, alpha=alpha),
        grid=(t // bm,),
        in_specs=[main(0), main(0), main(0), main(1), main(2), prev(2), nxt(2), main(3), prev(3), nxt(3),
                  main(0), pl.BlockSpec((1, 6, d), lambda i: (i // per_batch, 0, 0)),
                  const((3, d)), const((1, d)), const((1, d)),
                  const((d, LANES)), const((LANES, d)), const((d, LANES)), const((LANES, d)),
                  const((2 * d, d)), const((1, d)), const((1, d))],
        out_specs=[pl.BlockSpec((bm, d), lambda i: (i, 0)), pl.BlockSpec((bm, d), lambda i: (i, 0)),
                   pl.BlockSpec((bm, 1), lambda i: (i, 0)), pl.BlockSpec((bm, 1), lambda i: (i, 0))],
        out_shape=[jax.ShapeDtypeStruct((t, d), F32), jax.ShapeDtypeStruct((t, d), BF16),
                   jax.ShapeDtypeStruct((t, 1), F32), jax.ShapeDtypeStruct((t, 1), F32)],
        compiler_params=_params(("parallel",)),
        name="post",
    )(yf, yb, proj, proj, proj, proj, proj, proj, proj, proj, x2, mod3,
      scw, ssdg, scg, r4, e4, r32, e32, wout, lng, lnb)


def _top_values(s, n, want_rank=False):
    vals = []
    rank = jnp.full(s.shape, float(n), F32) if want_rank else None
    for r in range(n):
        m = jnp.max(s, axis=0, keepdims=True)
        vals.append(m)
        hit = s == m
        if want_rank:
            rank = jnp.where(hit, float(r), rank)
        s = jnp.where(hit, -jnp.inf, s)
    return (vals, rank) if want_rank else vals


def _query_kernel(h2_ref, wq_ref, k_ref, r2_ref, e2_ref, n1_ref, c1_ref):
    tb = h2_ref.shape[0]
    for h in range(PEER_HEADS):
        q = _dot(h2_ref[...], wq_ref[:, h * D_QUERY:(h + 1) * D_QUERY]).astype(BF16)
        s1 = _dot_nt(k_ref[h, 0], q[:, :HALF_QUERY])
        s2 = _dot_nt(k_ref[h, 1], q[:, HALF_QUERY:])
        v1 = _top_values(s1, PEER_TOPK)
        v2, r2 = _top_values(s2, PEER_TOPK, want_rank=True)
        pairs = [(a, b) for a in range(PEER_TOPK) for b in range(PEER_TOPK) if (a + 1) * (b + 1) <= PEER_TOPK]
        sums = {ab: v1[ab[0]] + v2[ab[1]] for ab in pairs}
        rows = [sums[ab] for ab in pairs]
        rows += [jnp.full((1, tb), -jnp.inf, F32)] * ((-len(rows)) % SUBLANES)
        top = _top_values(jnp.concatenate(rows, axis=0), PEER_TOPK)
        z = jnp.ones((1, tb), F32)
        for kk in range(1, PEER_TOPK):
            z = z + jnp.exp(top[kk] - top[0])
        tau = top[PEER_TOPK - 1]
        n1 = jnp.zeros(s1.shape, F32)
        for a in range(PEER_TOPK):
            cnt = sum(jnp.where(sums[(a, b)] >= tau, 1.0, 0.0) for b in range(PEER_TOPK) if (a, b) in sums)
            n1 = jnp.where(s1 == v1[a], cnt, n1)
        r2_ref[h] = r2.astype(BF16)
        e2_ref[h] = jnp.exp(s2 - v2[0]).astype(BF16)
        n1_ref[h] = n1
        c1_ref[h] = jnp.exp(s1 - v1[0]) * (1.0 / z)


def _query(h2, wq, keys):
    t, d = h2.shape
    tb = min(256, t)
    ospec = pl.BlockSpec((PEER_HEADS, N_KEYS, tb), lambda i: (0, 0, i))
    return pl.pallas_call(
        _query_kernel,
        grid=(t // tb,),
        in_specs=[pl.BlockSpec((tb, d), lambda i: (i, 0)),
                  _single((d, PEER_HEADS * D_QUERY), lambda i: (0, 0)),
                  _single((PEER_HEADS, 2, N_KEYS, HALF_QUERY), lambda i: (0, 0, 0, 0))],
        out_specs=[ospec] * 4,
        out_shape=[jax.ShapeDtypeStruct((PEER_HEADS, N_KEYS, t), dt) for dt in (BF16, BF16, F32, F32)],
        compiler_params=_params(("parallel",)),
        name="peerquery",
    )(h2, wq, keys)


def _pow2_scale(bound):
    ratio = FP8_TARGET / jnp.where(bound > 0, bound, FP8_TARGET)
    exponent_only = lax.bitcast_convert_type(ratio, jnp.int32) & jnp.int32(0x7F800000)
    return jnp.minimum(lax.bitcast_convert_type(exponent_only, F32), FP8_SCALE_CAP)


def _expert_prep_kernel(w_ref, q_ref, inv_ref, sumsq_ref, *, transpose):
    w = w_ref[...]
    s = _pow2_scale(jnp.max(jnp.abs(w), axis=1, keepdims=True))
    inv_ref[...] = 1.0 / s
    sumsq_ref[...] = jnp.sum(w * w, axis=1, keepdims=True)
    ws = w * s
    q_ref[...] = (ws.T if transpose else ws).astype(FP8)


def _expert_prep(table, transpose):
    n_exp, d = table.shape
    rows = 512
    col = pl.BlockSpec((rows, 1), lambda i: (i, 0))
    q_spec = pl.BlockSpec((d, rows), lambda i: (0, i)) if transpose else pl.BlockSpec((rows, d), lambda i: (i, 0))
    q_shape = (d, n_exp) if transpose else (n_exp, d)
    return pl.pallas_call(
        functools.partial(_expert_prep_kernel, transpose=transpose),
        grid=(n_exp // rows,),
        in_specs=[pl.BlockSpec((rows, d), lambda i: (i, 0))],
        out_specs=[q_spec, col, col],
        out_shape=[jax.ShapeDtypeStruct(q_shape, FP8), jax.ShapeDtypeStruct((n_exp, 1), F32),
                   jax.ShapeDtypeStruct((n_exp, 1), F32)],
        compiler_params=_params(("parallel",)),
        name="expertprep_t" if transpose else "expertprep",
    )(table)


def _peer_kernel(h2_ref, u_ref, vt_ref, r2_ref, e2_ref, n1_ref, c1_ref, x1_ref, mod_ref, lng_ref, lnb_ref,
                 iu_ref, iv_ref, ish_ref, sp_ref, sh_ref, isp_ref, o_ref, acc_ref, h8_ref, *, alpha):
    e = pl.program_id(1)

    @pl.when(e == 0)
    def _():
        acc_ref[...] = jnp.zeros_like(acc_ref)
        h8_ref[...] = (h2_ref[...].astype(F32) * sh_ref[...]).T.astype(FP8)

    n1 = n1_ref[...].astype(BF16)
    c1 = c1_ref[...].astype(BF16)
    parts = []
    for ii in range(u_ref.shape[0] // N_KEYS):
        rows = slice(ii * N_KEYS, (ii + 1) * N_KEYS)
        at = _dot(u_ref[rows, :], h8_ref[...])
        g = None
        for h in range(PEER_HEADS):
            w = jnp.where(r2_ref[h] < n1[h, ii:ii + 1, :], e2_ref[h], jnp.zeros((), BF16)) * c1[h, ii:ii + 1, :]
            g = w if g is None else g + w
        b = at * (iu_ref[rows, :] * ish_ref[...])
        p = b * (1.0 + lax.erf(b)) * (iv_ref[rows, :] * sp_ref[...])
        parts.append((p.astype(BF16) * g).astype(FP8))
    pt = jnp.concatenate(parts, axis=0)
    for c in range(0, vt_ref.shape[0], PEER_OUT_PIECE):
        dr = slice(c, c + PEER_OUT_PIECE)
        acc_ref[dr, :] += _dot(vt_ref[dr, :], pt)

    @pl.when(e == pl.num_programs(1) - 1)
    def _():
        m = mod_ref[0]
        ffn = acc_ref[...].T * isp_ref[...]
        o_ref[...] = _layer_norm(alpha * x1_ref[...] + m[5:6] * ffn, lng_ref[...], lnb_ref[...])


def _peer(h2, u8, vt8, r2, e2, n1, c1, x1, mod3, lng, lnb, iu, iv, ish_blk, sp_blk, sh, isp, seq, alpha):
    t, d = h2.shape
    n_exp = u8.shape[0]
    tb = min(PEER_TOKEN_BLOCK, seq)
    eb = 1024
    per_batch = seq // tb
    kb = eb // N_KEYS
    full = pl.BlockSpec((PEER_HEADS, N_KEYS, tb), lambda i, j: (0, 0, i))
    part = pl.BlockSpec((PEER_HEADS, kb, tb), lambda i, j: (0, j, i))
    ecol = pl.BlockSpec((eb, 1), lambda i, j: (j, 0))
    blk = pl.BlockSpec((None, 1, 1), lambda i, j: (i, 0, 0))
    tcol = pl.BlockSpec((tb, 1), lambda i, j: (i, 0))
    return pl.pallas_call(
        functools.partial(_peer_kernel, alpha=alpha),
        grid=(t // tb, n_exp // eb),
        in_specs=[pl.BlockSpec((tb, d), lambda i, j: (i, 0)),
                  pl.BlockSpec((eb, d), lambda i, j: (j, 0)),
                  pl.BlockSpec((d, eb), lambda i, j: (0, j)),
                  full, full, part, part,
                  pl.BlockSpec((tb, d), lambda i, j: (i, 0)),
                  pl.BlockSpec((1, 6, d), lambda i, j: (i // per_batch, 0, 0)),
                  pl.BlockSpec((1, d), lambda i, j: (0, 0)),
                  pl.BlockSpec((1, d), lambda i, j: (0, 0)),
                  ecol, ecol, blk, blk, tcol, tcol],
        out_specs=pl.BlockSpec((tb, d), lambda i, j: (i, 0)),
        out_shape=jax.ShapeDtypeStruct((t, d), F32),
        scratch_shapes=[pltpu.VMEM((d, tb), F32), pltpu.VMEM((d, tb), FP8)],
        compiler_params=_params(("parallel", "arbitrary")),
        name="peer",
    )(h2, u8, vt8, r2, e2, n1, c1, x1, mod3, lng, lnb, iu, iv, ish_blk, sp_blk, sh, isp)


def _ffn(h2, h_amax, h_sumsq, x1, mod3, w_query, sub_keys, expert_u, expert_v, ln2_g, ln2_b, seq, alpha):
    t = h2.shape[0]
    r2, e2, n1, c1 = _query(h2, w_query.astype(BF16), sub_keys.astype(BF16))
    u8, inv_su, u_sumsq = _expert_prep(expert_u, transpose=False)
    vt8, inv_sv, _ = _expert_prep(expert_v, transpose=True)
    tb = min(PEER_TOKEN_BLOCK, seq)
    per_block = lambda v: jnp.max(v.reshape(t // tb, tb), axis=1).reshape(-1, 1, 1)
    sh = _pow2_scale(per_block(h_amax))
    sp = _pow2_scale(jnp.sqrt(jnp.max(u_sumsq) * per_block(h_sumsq)) * (PEER_HEADS * jnp.max(inv_sv)))
    sqrt_half = np.float32(math.sqrt(0.5))
    per_token = lambda s: jnp.broadcast_to(s, (t // tb, tb, 1)).reshape(t, 1)
    return _peer(h2, u8, vt8, r2, e2, n1, c1, x1, mod3, _row(ln2_g), _row(ln2_b), inv_su * sqrt_half,
                 inv_sv * sqrt_half, 1.0 / sh, sp, per_token(sh), per_token(1.0 / sp), seq, alpha)


def _one_hot_cols(n_rows, n_cols, group_size, row_offset=0):
    r = np.arange(n_rows)[:, None]
    c = np.arange(n_cols)[None, :]
    return jnp.asarray(r == row_offset + c // group_size, dtype=BF16)


def _row(v):
    return v.reshape(1, -1).astype(F32)


def _pad_lanes(v):
    return jnp.pad(v, (0, LANES - v.shape[0])).reshape(1, LANES).astype(F32)


def kernel(x, c, w_ada, b_ada, w_in, conv_ssd_w, conv_ssd_b, dt_bias_f, dt_bias_b, a_log_f, a_log_b, d_skip, ssd_norm_g, short_conv_w, sc_norm_g, w_out, ln1_g, ln1_b, w_query, sub_keys, expert_u, expert_v, ln2_g, ln2_b):
    batch, seq, d = x.shape
    depth = w_ada.shape[0]
    alpha = (2.0 * depth) ** 0.25
    t = batch * seq
    x2 = x.reshape(t, d)
    c_pad = jnp.pad(c, ((0, SUBLANES - batch % SUBLANES), (0, 0))) if batch % SUBLANES else c

    e_f = _one_hot_cols(LANES, SSD_WIDTH, SSD_HEADDIM, 0)
    e_b = _one_hot_cols(LANES, SSD_WIDTH, SSD_HEADDIM, SSD_HEADS)
    e4 = _one_hot_cols(LANES, SSD_WIDTH, SSD_WIDTH // SSD_GROUPS)
    e32 = _one_hot_cols(LANES, SC_WIDTH, SC_WIDTH // SC_GROUPS)
    r4 = e4.T
    r32 = e32.T

    for i in range(depth):
        mod = _ada(c_pad, w_ada[i], b_ada[i].reshape(1, -1))
        mod3 = mod[:batch].reshape(batch, 6, d)

        w_main, wd_hi, wd_lo = _win_prep(w_in[i])
        proj, dt_raw = _inproj(x2, mod3, w_main, wd_hi, wd_lo, seq)

        xbc = _conv(proj, conv_ssd_w[i], conv_ssd_b[i].reshape(1, -1), seq)
        bias = _pad_lanes(jnp.concatenate([dt_bias_f[i], dt_bias_b[i]]))
        alog = _pad_lanes(jnp.concatenate([a_log_f[i], a_log_b[i]]))
        dskip_x = _row(jnp.repeat(d_skip[i], SSD_HEADDIM))
        y_f, y_b = _ssd(xbc, dt_raw, bias, alog, dskip_x, e_f, e_b, batch, seq)

        x1, h2, h_amax, h_sumsq = _post(y_f, y_b, proj, x2, mod3, short_conv_w[i], _row(ssd_norm_g[i]),
                                        _row(sc_norm_g[i]), r4, e4, r32, e32, w_out[i].astype(BF16),
                                        _row(ln1_g[i]), _row(ln1_b[i]), seq, alpha)

        x2 = _ffn(h2, h_amax, h_sumsq, x1, mod3, w_query[i], sub_keys[i], expert_u[i], expert_v[i],
                  ln2_g[i], ln2_b[i], seq, alpha)
    return x2.reshape(batch, seq, d)
```

```python
import functools
import math

import jax
import jax.numpy as jnp
import numpy as np
from jax import lax
from jax.experimental import pallas as pl
from jax.experimental.pallas import tpu as pltpu

F32 = jnp.float32
BF16 = jnp.bfloat16
FP8 = jnp.float8_e4m3fn
FP8_TARGET = 256.0
FP8_SCALE_CAP = 2.0 ** 60

D_MODEL = 2048
SSD_WIDTH = D_MODEL
SSD_HEADDIM = 64
SSD_HEADS = SSD_WIDTH // SSD_HEADDIM
SSD_GROUPS = 4
SSD_STATE = 128
SSD_CHUNK = 128
HEADS_PER_GROUP = SSD_HEADS // SSD_GROUPS
GROUP_WIDTH = HEADS_PER_GROUP * SSD_HEADDIM
SC_WIDTH = D_MODEL
SC_GROUPS = 32
XBC_WIDTH = SSD_WIDTH + 2 * SSD_GROUPS * SSD_STATE
MAIN_COLS = 4 * D_MODEL + XBC_WIDTH
XBC_COL0 = 4 * D_MODEL
PEER_HEADS = 8
N_KEYS = 128
PEER_TOPK = 16
D_QUERY = 512
HALF_QUERY = D_QUERY // 2
NORM_EPS = 1e-5
LANES = 128
SUBLANES = 8
HALO_ROWS = 16
POST_COL_PIECE = 256
PEER_TOKEN_BLOCK = 512
PEER_OUT_PIECE = 256
VMEM_LIMIT = 56 * 1024 * 1024


def _dot(a, b):
    return jnp.dot(a, b, preferred_element_type=F32)


def _dot_nt(a, b):
    return lax.dot_general(a, b, (((1,), (1,)), ((), ())), preferred_element_type=F32)


def _split2(x):
    hi = x.astype(BF16)
    lo = (x - hi.astype(F32)).astype(BF16)
    return hi, lo


def _split3(x):
    hi = x.astype(BF16)
    r = x - hi.astype(F32)
    mid = r.astype(BF16)
    lo = (r - mid.astype(F32)).astype(BF16)
    return hi, mid, lo


def _softplus(x):
    return jnp.maximum(x, 0.0) + jnp.log1p(jnp.exp(-jnp.abs(x)))


def _silu(x):
    return x * (1.0 / (1.0 + jnp.exp(-x)))


def _params(sem):
    return pltpu.CompilerParams(dimension_semantics=sem, vmem_limit_bytes=VMEM_LIMIT)


def _ada_kernel(c_ref, w_ref, b_ref, o_ref):
    sc = _silu(c_ref[...])
    o_ref[...] = _dot(sc.astype(BF16), w_ref[...].astype(BF16)) + b_ref[...]


def _ada(c_pad, w, b):
    rows, d = c_pad.shape
    n = w.shape[1]
    bn = 1024
    return pl.pallas_call(
        _ada_kernel,
        grid=(n // bn,),
        in_specs=[pl.BlockSpec((rows, d), lambda j: (0, 0)),
                  pl.BlockSpec((d, bn), lambda j: (0, j)),
                  pl.BlockSpec((1, bn), lambda j: (0, j))],
        out_specs=pl.BlockSpec((rows, bn), lambda j: (0, j)),
        out_shape=jax.ShapeDtypeStruct((rows, n), F32),
        compiler_params=_params(("arbitrary",)),
        name="ada",
    )(c_pad, w, b)


DT_COL0 = SSD_WIDTH + XBC_WIDTH
GATES_COL0 = DT_COL0 + 2 * SSD_HEADS


def _win_prep_kernel(w_ref, main_ref, dh_ref, dl_ref):
    w = w_ref[...]
    main_ref[:, 0:SSD_WIDTH] = w[:, 0:SSD_WIDTH].astype(BF16)
    main_ref[:, SSD_WIDTH:XBC_COL0] = w[:, GATES_COL0:GATES_COL0 + 3 * SC_WIDTH].astype(BF16)
    main_ref[:, XBC_COL0:MAIN_COLS] = w[:, SSD_WIDTH:DT_COL0].astype(BF16)
    dt = w[:, DT_COL0:DT_COL0 + LANES]
    lane = lax.broadcasted_iota(jnp.int32, dt.shape, 1)
    hi, lo = _split2(jnp.where(lane < 2 * SSD_HEADS, dt, 0.0))
    dh_ref[...] = hi
    dl_ref[...] = lo


def _win_prep(w):
    d, n = w.shape
    rows = 128
    return pl.pallas_call(
        _win_prep_kernel,
        grid=(d // rows,),
        in_specs=[pl.BlockSpec((rows, n), lambda i: (i, 0))],
        out_specs=[pl.BlockSpec((rows, MAIN_COLS), lambda i: (i, 0)),
                   pl.BlockSpec((rows, LANES), lambda i: (i, 0)),
                   pl.BlockSpec((rows, LANES), lambda i: (i, 0))],
        out_shape=[jax.ShapeDtypeStruct((d, MAIN_COLS), BF16), jax.ShapeDtypeStruct((d, LANES), BF16),
                   jax.ShapeDtypeStruct((d, LANES), BF16)],
        compiler_params=_params(("parallel",)),
        name="winprep",
    )(w)


def _inproj_kernel(x_ref, mod_ref, w_ref, wdh_ref, wdl_ref, o_ref, dt_ref, h_scr):
    @pl.when(pl.program_id(1) == 0)
    def _():
        m = mod_ref[0]
        h = x_ref[...] * (1.0 + m[1:2]) + m[0:1]
        hi, lo = _split2(h)
        h_scr[...] = hi
        dt_ref[...] = _dot(hi, wdh_ref[...]) + _dot(lo, wdh_ref[...]) + _dot(hi, wdl_ref[...])

    o_ref[...] = _dot(h_scr[...], w_ref[...]).astype(o_ref.dtype)


def _inproj(x2, mod3, w_main, wd_hi, wd_lo, seq):
    t, d = x2.shape
    n = w_main.shape[1]
    bm = min(1024, seq)
    bn = 1024
    per_batch = seq // bm
    return pl.pallas_call(
        _inproj_kernel,
        grid=(t // bm, n // bn),
        in_specs=[pl.BlockSpec((bm, d), lambda i, j: (i, 0)),
                  pl.BlockSpec((1, 6, d), lambda i, j: (i // per_batch, 0, 0)),
                  pl.BlockSpec((d, bn), lambda i, j: (0, j)),
                  pl.BlockSpec((d, LANES), lambda i, j: (0, 0)),
                  pl.BlockSpec((d, LANES), lambda i, j: (0, 0))],
        out_specs=[pl.BlockSpec((bm, bn), lambda i, j: (i, j)),
                   pl.BlockSpec((bm, LANES), lambda i, j: (i, 0))],
        out_shape=[jax.ShapeDtypeStruct((t, n), BF16),
                   jax.ShapeDtypeStruct((t, LANES), F32)],
        scratch_shapes=[pltpu.VMEM((bm, d), BF16)],
        compiler_params=_params(("parallel", "arbitrary")),
        name="inproj",
    )(x2, mod3, w_main, wd_hi, wd_lo)


def _shifted_rows(u, prev_row, next_row):
    rows = u.shape[0]
    ridx = lax.broadcasted_iota(jnp.int32, u.shape, 0)
    up = jnp.where(ridx == 0, prev_row, pltpu.roll(u, 1, axis=0))
    un = jnp.where(ridx == rows - 1, next_row, pltpu.roll(u, rows - 1, axis=0))
    return up, un


def _halo_specs(bm, bw, col_of, t):
    per = bm // HALO_ROWS
    last = t // HALO_ROWS - 1
    prev = pl.BlockSpec((HALO_ROWS, bw), lambda i, j: (jnp.maximum(i * per - 1, 0), col_of(j)))
    nxt = pl.BlockSpec((HALO_ROWS, bw), lambda i, j: (jnp.minimum((i + 1) * per, last), col_of(j)))
    return prev, nxt


def _conv_kernel(u_ref, p_ref, n_ref, w_ref, b_ref, o_ref, *, bm, seq):
    i = pl.program_id(0)
    u = u_ref[...].astype(F32)
    has_prev = ((i * bm) % seq != 0).astype(F32)
    has_next = (((i + 1) * bm) % seq != 0).astype(F32)
    up, un = _shifted_rows(u, p_ref[HALO_ROWS - 1:HALO_ROWS, :].astype(F32) * has_prev,
                           n_ref[0:1, :].astype(F32) * has_next)
    w = w_ref[...]
    o_ref[...] = _silu(up * w[0:1] + u * w[1:2] + un * w[2:3] + b_ref[...])


def _conv(proj, conv_w, conv_b, seq):
    t = proj.shape[0]
    bm = min(1024, seq)
    bw = 1024
    col0 = XBC_COL0 // bw
    col_of = lambda j: col0 + j
    prev, nxt = _halo_specs(bm, bw, col_of, t)
    return pl.pallas_call(
        functools.partial(_conv_kernel, bm=bm, seq=seq),
        grid=(t // bm, XBC_WIDTH // bw),
        in_specs=[pl.BlockSpec((bm, bw), lambda i, j: (i, col_of(j))), prev, nxt,
                  pl.BlockSpec((3, bw), lambda i, j: (0, j)),
                  pl.BlockSpec((1, bw), lambda i, j: (0, j))],
        out_specs=pl.BlockSpec((bm, bw), lambda i, j: (i, j)),
        out_shape=jax.ShapeDtypeStruct((t, XBC_WIDTH), F32),
        compiler_params=_params(("parallel", "parallel")),
        name="ssdconv",
    )(proj, proj, proj, conv_w, conv_b)


def _ssd_direction(x_ref, b_ref, c_ref, dt_ref, e_ref, h_scr, y_ref, bias, a, dskip, backward):
    cs_len = SSD_CHUNK
    row = lax.broadcasted_iota(jnp.int32, (cs_len, cs_len), 0)
    col = lax.broadcasted_iota(jnp.int32, (cs_len, cs_len), 1)
    tri = (col <= row).astype(BF16)
    dt = _softplus(dt_ref[...] + bias)
    adt = dt * a
    h3 = _split3(adt)
    cs = _dot(tri, h3[0]) + _dot(tri, h3[1]) + _dot(tri, h3[2])
    tot = cs[cs_len - 1:cs_len, :]
    if backward:
        ecs = cs - adt
        p = -ecs
        wst = dt * jnp.exp(ecs)
        indec = jnp.exp(tot - ecs)
        mask = col >= row
    else:
        p = cs
        wst = dt * jnp.exp(tot - cs)
        indec = jnp.exp(cs)
        mask = col <= row
    pt = p.T
    e = e_ref[...]
    dtx = _dot(dt.astype(BF16), e)
    wstx = _dot(wst.astype(BF16), e)
    indx = _dot(indec.astype(BF16), e)
    xs = x_ref[...]
    xd = (xs * dtx).astype(BF16)
    xw = (xs * wstx).astype(BF16)
    col_off = SSD_HEADS if backward else 0
    pair = 2 * SSD_HEADDIM
    first_of_pair = lax.broadcasted_iota(jnp.int32, (cs_len, pair), 1) < SSD_HEADDIM
    zero = jnp.zeros((), BF16)
    for g in range(SSD_GROUPS):
        gs = slice(g * GROUP_WIDTH, (g + 1) * GROUP_WIDTH)
        bg = b_ref[:, g * SSD_STATE:(g + 1) * SSD_STATE]
        cg = c_ref[:, g * SSD_STATE:(g + 1) * SSD_STATE].astype(BF16)
        cb = _dot_nt(cg, bg.astype(BF16))
        h_in = h_scr[g]
        y_off = _dot(cg, h_in.astype(BF16)) * indx[:, gs]
        st = _dot(bg.T.astype(BF16), xw[:, gs])
        for r in range(0, HEADS_PER_GROUP, 2):
            hd = g * HEADS_PER_GROUP + r
            ms = []
            for ci in (hd + col_off, hd + col_off + 1):
                seg = p[:, ci:ci + 1] - pt[ci:ci + 1, :]
                lmat = jnp.exp(jnp.where(mask, seg, -jnp.inf))
                ms.append((cb * lmat).astype(BF16))
            hs = slice(hd * SSD_HEADDIM, (hd + 2) * SSD_HEADDIM)
            x2h = xd[:, hs]
            rhs = jnp.concatenate([jnp.where(first_of_pair, x2h, zero), jnp.where(first_of_pair, zero, x2h)], axis=0)
            y = _dot(jnp.concatenate(ms, axis=1), rhs) + y_off[:, r * SSD_HEADDIM:(r + 2) * SSD_HEADDIM]
            if dskip is not None:
                y = y + xs[:, hs] * dskip[:, hs]
            y_ref[:, hs] = y
        edge = 0 if backward else cs_len - 1
        h_scr[g] = indx[edge:edge + 1, gs] * h_in + st


def _ssd_kernel(xf_ref, bf_ref, cf_ref, dtf_ref, xb_ref, bb_ref, cb_ref, dtb_ref,
                bias_ref, alog_ref, dskip_ref, ef_ref, eb_ref, yf_ref, yb_ref, hf_scr, hb_scr):
    @pl.when(pl.program_id(1) == 0)
    def _():
        hf_scr[...] = jnp.zeros_like(hf_scr)
        hb_scr[...] = jnp.zeros_like(hb_scr)

    bias = bias_ref[...]
    a = -jnp.exp(alog_ref[...])
    _ssd_direction(xf_ref, bf_ref, cf_ref, dtf_ref, ef_ref, hf_scr, yf_ref, bias, a, dskip_ref[...], False)
    _ssd_direction(xb_ref, bb_ref, cb_ref, dtb_ref, eb_ref, hb_scr, yb_ref, bias, a, None, True)


def _ssd(xbc, dt_raw, bias, alog, dskip_x, e_f, e_b, batch, seq):
    t = xbc.shape[0]
    nc = seq // SSD_CHUNK
    cl = SSD_CHUNK
    gn = SSD_GROUPS * SSD_STATE
    bcol = SSD_WIDTH // gn
    fwd = lambda b, k: b * nc + k
    bwd = lambda b, k: b * nc + (nc - 1 - k)

    def specs(ch):
        return [pl.BlockSpec((cl, SSD_WIDTH), lambda b, k: (ch(b, k), 0)),
                pl.BlockSpec((cl, gn), lambda b, k: (ch(b, k), bcol)),
                pl.BlockSpec((cl, gn), lambda b, k: (ch(b, k), bcol + 1)),
                pl.BlockSpec((cl, LANES), lambda b, k: (ch(b, k), 0))]

    const = lambda shape: pl.BlockSpec(shape, lambda b, k: (0, 0))
    state = pltpu.VMEM((SSD_GROUPS, SSD_STATE, GROUP_WIDTH), F32)
    return pl.pallas_call(
        _ssd_kernel,
        grid=(batch, nc),
        in_specs=specs(fwd) + specs(bwd) + [const((1, LANES)), const((1, LANES)), const((1, SSD_WIDTH)),
                                            const((LANES, SSD_WIDTH)), const((LANES, SSD_WIDTH))],
        out_specs=[pl.BlockSpec((cl, SSD_WIDTH), lambda b, k: (fwd(b, k), 0)),
                   pl.BlockSpec((cl, SSD_WIDTH), lambda b, k: (bwd(b, k), 0))],
        out_shape=[jax.ShapeDtypeStruct((t, SSD_WIDTH), F32)] * 2,
        scratch_shapes=[state, state],
        compiler_params=_params(("parallel", "arbitrary")),
        name="ssd",
    )(xbc, xbc, xbc, dt_raw, xbc, xbc, xbc, dt_raw, bias, alog, dskip_x, e_f, e_b)


def _group_rms(y, r_ref, e_ref, group_size):
    gsum = _dot((y * y).astype(BF16), r_ref[...])
    inv = lax.rsqrt(gsum * (1.0 / group_size) + NORM_EPS)
    return y * _dot(inv.astype(BF16), e_ref[...])


def _layer_norm(v, g, b):
    mu = jnp.mean(v, axis=-1, keepdims=True)
    vc = v - mu
    var = jnp.mean(vc * vc, axis=-1, keepdims=True)
    return vc * lax.rsqrt(var + NORM_EPS) * g + b


def _post_kernel(yf_ref, yb_ref, z_ref, gb_ref, gc_ref, gcp_ref, gcn_ref, v_ref, vp_ref, vn_ref,
                 x_ref, mod_ref, scw_ref, ssdg_ref, scg_ref, r4_ref, e4_ref, r32_ref, e32_ref,
                 wout_ref, lng_ref, lnb_ref, x1_ref, h2_ref, amax_ref, sumsq_ref, *, bm, seq, alpha):
    i = pl.program_id(0)
    m = mod_ref[0]
    y = (yf_ref[...] + yb_ref[...]) * _silu(z_ref[...].astype(F32))
    y_ssd = (_group_rms(y, r4_ref, e4_ref, SSD_WIDTH // SSD_GROUPS) * ssdg_ref[...]).astype(BF16)
    n_cols = [slice(c, c + POST_COL_PIECE) for c in range(0, wout_ref.shape[1], POST_COL_PIECE)]
    mix_ssd = [_dot(y_ssd, wout_ref[0:SSD_WIDTH, cs]) for cs in n_cols]

    f32 = lambda ref, rows=slice(None): ref[rows, :].astype(F32)
    u = f32(gc_ref) * f32(v_ref)
    has_prev = ((i * bm) % seq != 0).astype(F32)
    has_next = (((i + 1) * bm) % seq != 0).astype(F32)
    last = slice(HALO_ROWS - 1, HALO_ROWS)
    first = slice(0, 1)
    up, un = _shifted_rows(u, f32(gcp_ref, last) * f32(vp_ref, last) * has_prev,
                           f32(gcn_ref, first) * f32(vn_ref, first) * has_next)
    w = scw_ref[...]
    y_sc = f32(gb_ref) * (up * w[0:1] + u * w[1:2] + un * w[2:3])
    y_sc = (_group_rms(y_sc, r32_ref, e32_ref, SC_WIDTH // SC_GROUPS) * scg_ref[...]).astype(BF16)
    mix = jnp.concatenate([part + _dot(y_sc, wout_ref[SSD_WIDTH:SSD_WIDTH + SC_WIDTH, cs])
                           for part, cs in zip(mix_ssd, n_cols)], axis=1)
    x1 = _layer_norm(alpha * x_ref[...] + m[2:3] * mix, lng_ref[...], lnb_ref[...])
    x1_ref[...] = x1
    h2 = x1 * (1.0 + m[4:5]) + m[3:4]
    h2_ref[...] = h2.astype(BF16)
    amax_ref[...] = jnp.max(jnp.abs(h2), axis=-1, keepdims=True)
    sumsq_ref[...] = jnp.sum(h2 * h2, axis=-1, keepdims=True)


def _single(shape, index_map):
    return pl.BlockSpec(shape, index_map, pipeline_mode=pl.Buffered(1))


def _post(yf, yb, proj, x2, mod3, scw, ssdg, scg, r4, e4, r32, e32, wout, lng, lnb, seq, alpha):
    t, d = x2.shape
    bm = min(256, seq)
    per_batch = seq // bm
    per = bm // HALO_ROWS
    last = t // HALO_ROWS - 1
    main = lambda c: pl.BlockSpec((bm, d), lambda i: (i, c))
    prev = lambda c: pl.BlockSpec((HALO_ROWS, d), lambda i: (jnp.maximum(i * per - 1, 0), c))
    nxt = lambda c: pl.BlockSpec((HALO_ROWS, d), lambda i: (jnp.minimum((i + 1) * per, last), c))
    const = lambda shape: _single(shape, lambda i: (0, 0))
    return pl.pallas_call(
        functools.partial(_post_kernel, bm=bm, seq=seq, alpha=alpha),
        grid=(t // bm,),
        in_specs=[main(0), main(0), main(0), main(1), main(2), prev(2), nxt(2), main(3), prev(3), nxt(3),
                  main(0), pl.BlockSpec((1, 6, d), lambda i: (i // per_batch, 0, 0)),
                  const((3, d)), const((1, d)), const((1, d)),
                  const((d, LANES)), const((LANES, d)), const((d, LANES)), const((LANES, d)),
                  const((2 * d, d)), const((1, d)), const((1, d))],
        out_specs=[pl.BlockSpec((bm, d), lambda i: (i, 0)), pl.BlockSpec((bm, d), lambda i: (i, 0)),
                   pl.BlockSpec((bm, 1), lambda i: (i, 0)), pl.BlockSpec((bm, 1), lambda i: (i, 0))],
        out_shape=[jax.ShapeDtypeStruct((t, d), F32), jax.ShapeDtypeStruct((t, d), BF16),
                   jax.ShapeDtypeStruct((t, 1), F32), jax.ShapeDtypeStruct((t, 1), F32)],
        compiler_params=_params(("parallel",)),
        name="post",
    )(yf, yb, proj, proj, proj, proj, proj, proj, proj, proj, x2, mod3,
      scw, ssdg, scg, r4, e4, r32, e32, wout, lng, lnb)


def _top_values(s, n, want_rank=False):
    vals = []
    rank = jnp.full(s.shape, float(n), F32) if want_rank else None
    for r in range(n):
        m = jnp.max(s, axis=0, keepdims=True)
        vals.append(m)
        hit = s == m
        if want_rank:
            rank = jnp.where(hit, float(r), rank)
        s = jnp.where(hit, -jnp.inf, s)
    return (vals, rank) if want_rank else vals


def _query_kernel(h2_ref, wq_ref, k_ref, r2_ref, e2_ref, n1_ref, c1_ref):
    tb = h2_ref.shape[0]
    for h in range(PEER_HEADS):
        q = _dot(h2_ref[...], wq_ref[:, h * D_QUERY:(h + 1) * D_QUERY]).astype(BF16)
        s1 = _dot_nt(k_ref[h, 0], q[:, :HALF_QUERY])
        s2 = _dot_nt(k_ref[h, 1], q[:, HALF_QUERY:])
        v1 = _top_values(s1, PEER_TOPK)
        v2, r2 = _top_values(s2, PEER_TOPK, want_rank=True)
        pairs = [(a, b) for a in range(PEER_TOPK) for b in range(PEER_TOPK) if (a + 1) * (b + 1) <= PEER_TOPK]
        sums = {ab: v1[ab[0]] + v2[ab[1]] for ab in pairs}
        rows = [sums[ab] for ab in pairs]
        rows += [jnp.full((1, tb), -jnp.inf, F32)] * ((-len(rows)) % SUBLANES)
        top = _top_values(jnp.concatenate(rows, axis=0), PEER_TOPK)
        z = jnp.ones((1, tb), F32)
        for kk in range(1, PEER_TOPK):
            z = z + jnp.exp(top[kk] - top[0])
        tau = top[PEER_TOPK - 1]
        n1 = jnp.zeros(s1.shape, F32)
        for a in range(PEER_TOPK):
            cnt = sum(jnp.where(sums[(a, b)] >= tau, 1.0, 0.0) for b in range(PEER_TOPK) if (a, b) in sums)
            n1 = jnp.where(s1 == v1[a], cnt, n1)
        r2_ref[h] = r2.astype(BF16)
        e2_ref[h] = jnp.exp(s2 - v2[0]).astype(BF16)
        n1_ref[h] = n1
        c1_ref[h] = jnp.exp(s1 - v1[0]) * (1.0 / z)


def _query(h2, wq, keys):
    t, d = h2.shape
    tb = min(256, t)
    ospec = pl.BlockSpec((PEER_HEADS, N_KEYS, tb), lambda i: (0, 0, i))
    return pl.pallas_call(
        _query_kernel,
        grid=(t // tb,),
        in_specs=[pl.BlockSpec((tb, d), lambda i: (i, 0)),
                  _single((d, PEER_HEADS * D_QUERY), lambda i: (0, 0)),
                  _single((PEER_HEADS, 2, N_KEYS, HALF_QUERY), lambda i: (0, 0, 0, 0))],
        out_specs=[ospec] * 4,
        out_shape=[jax.ShapeDtypeStruct((PEER_HEADS, N_KEYS, t), dt) for dt in (BF16, BF16, F32, F32)],
        compiler_params=_params(("parallel",)),
        name="peerquery",
    )(h2, wq, keys)


def _pow2_scale(bound):
    ratio = FP8_TARGET / jnp.where(bound > 0, bound, FP8_TARGET)
    exponent_only = lax.bitcast_convert_type(ratio, jnp.int32) & jnp.int32(0x7F800000)
    return jnp.minimum(lax.bitcast_convert_type(exponent_only, F32), FP8_SCALE_CAP)


def _expert_prep_kernel(w_ref, q_ref, inv_ref, sumsq_ref, *, transpose):
    w = w_ref[...]
    s = _pow2_scale(jnp.max(jnp.abs(w), axis=1, keepdims=True))
    inv_ref[...] = 1.0 / s
    sumsq_ref[...] = jnp.sum(w * w, axis=1, keepdims=True)
    ws = w * s
    q_ref[...] = (ws.T if transpose else ws).astype(FP8)


def _expert_prep(table, transpose):
    n_exp, d = table.shape
    rows = 512
    col = pl.BlockSpec((rows, 1), lambda i: (i, 0))
    q_spec = pl.BlockSpec((d, rows), lambda i: (0, i)) if transpose else pl.BlockSpec((rows, d), lambda i: (i, 0))
    q_shape = (d, n_exp) if transpose else (n_exp, d)
    return pl.pallas_call(
        functools.partial(_expert_prep_kernel, transpose=transpose),
        grid=(n_exp // rows,),
        in_specs=[pl.BlockSpec((rows, d), lambda i: (i, 0))],
        out_specs=[q_spec, col, col],
        out_shape=[jax.ShapeDtypeStruct(q_shape, FP8), jax.ShapeDtypeStruct((n_exp, 1), F32),
                   jax.ShapeDtypeStruct((n_exp, 1), F32)],
        compiler_params=_params(("parallel",)),
        name="expertprep_t" if transpose else "expertprep",
    )(table)


def _peer_kernel(h2_ref, u_ref, vt_ref, r2_ref, e2_ref, n1_ref, c1_ref, x1_ref, mod_ref, lng_ref, lnb_ref,
                 iu_ref, iv_ref, ish_ref, sp_ref, sh_ref, isp_ref, o_ref, acc_ref, h8_ref, *, alpha):
    e = pl.program_id(1)

    @pl.when(e == 0)
    def _():
        acc_ref[...] = jnp.zeros_like(acc_ref)
        h8_ref[...] = (h2_ref[...].astype(F32) * sh_ref[...]).T.astype(FP8)

    n1 = n1_ref[...].astype(BF16)
    c1 = c1_ref[...].astype(BF16)
    parts = []
    for ii in range(u_ref.shape[0] // N_KEYS):
        rows = slice(ii * N_KEYS, (ii + 1) * N_KEYS)
        at = _dot(u_ref[rows, :], h8_ref[...])
        g = None
        for h in range(PEER_HEADS):
            w = jnp.where(r2_ref[h] < n1[h, ii:ii + 1, :], e2_ref[h], jnp.zeros((), BF16)) * c1[h, ii:ii + 1, :]
            g = w if g is None else g + w
        b = at * (iu_ref[rows, :] * ish_ref[...])
        p = b * (1.0 + lax.erf(b)) * (iv_ref[rows, :] * sp_ref[...])
        parts.append((p.astype(BF16) * g).astype(FP8))
    pt = jnp.concatenate(parts, axis=0)
    for c in range(0, vt_ref.shape[0], PEER_OUT_PIECE):
        dr = slice(c, c + PEER_OUT_PIECE)
        acc_ref[dr, :] += _dot(vt_ref[dr, :], pt)

    @pl.when(e == pl.num_programs(1) - 1)
    def _():
        m = mod_ref[0]
        ffn = acc_ref[...].T * isp_ref[...]
        o_ref[...] = _layer_norm(alpha * x1_ref[...] + m[5:6] * ffn, lng_ref[...], lnb_ref[...])


def _peer(h2, u8, vt8, r2, e2, n1, c1, x1, mod3, lng, lnb, iu, iv, ish_blk, sp_blk, sh, isp, seq, alpha):
    t, d = h2.shape
    n_exp = u8.shape[0]
    tb = min(PEER_TOKEN_BLOCK, seq)
    eb = 1024
    per_batch = seq // tb
    kb = eb // N_KEYS
    full = pl.BlockSpec((PEER_HEADS, N_KEYS, tb), lambda i, j: (0, 0, i))
    part = pl.BlockSpec((PEER_HEADS, kb, tb), lambda i, j: (0, j, i))
    ecol = pl.BlockSpec((eb, 1), lambda i, j: (j, 0))
    blk = pl.BlockSpec((None, 1, 1), lambda i, j: (i, 0, 0))
    tcol = pl.BlockSpec((tb, 1), lambda i, j: (i, 0))
    return pl.pallas_call(
        functools.partial(_peer_kernel, alpha=alpha),
        grid=(t // tb, n_exp // eb),
        in_specs=[pl.BlockSpec((tb, d), lambda i, j: (i, 0)),
                  pl.BlockSpec((eb, d), lambda i, j: (j, 0)),
                  pl.BlockSpec((d, eb), lambda i, j: (0, j)),
                  full, full, part, part,
                  pl.BlockSpec((tb, d), lambda i, j: (i, 0)),
                  pl.BlockSpec((1, 6, d), lambda i, j: (i // per_batch, 0, 0)),
                  pl.BlockSpec((1, d), lambda i, j: (0, 0)),
                  pl.BlockSpec((1, d), lambda i, j: (0, 0)),
                  ecol, ecol, blk, blk, tcol, tcol],
        out_specs=pl.BlockSpec((tb, d), lambda i, j: (i, 0)),
        out_shape=jax.ShapeDtypeStruct((t, d), F32),
        scratch_shapes=[pltpu.VMEM((d, tb), F32), pltpu.VMEM((d, tb), FP8)],
        compiler_params=_params(("parallel", "arbitrary")),
        name="peer",
    )(h2, u8, vt8, r2, e2, n1, c1, x1, mod3, lng, lnb, iu, iv, ish_blk, sp_blk, sh, isp)


def _ffn(h2, h_amax, h_sumsq, x1, mod3, w_query, sub_keys, expert_u, expert_v, ln2_g, ln2_b, seq, alpha):
    t = h2.shape[0]
    r2, e2, n1, c1 = _query(h2, w_query.astype(BF16), sub_keys.astype(BF16))
    u8, inv_su, u_sumsq = _expert_prep(expert_u, transpose=False)
    vt8, inv_sv, _ = _expert_prep(expert_v, transpose=True)
    tb = min(PEER_TOKEN_BLOCK, seq)
    per_block = lambda v: jnp.max(v.reshape(t // tb, tb), axis=1).reshape(-1, 1, 1)
    sh = _pow2_scale(per_block(h_amax))
    sp = _pow2_scale(jnp.sqrt(jnp.max(u_sumsq) * per_block(h_sumsq)) * (PEER_HEADS * jnp.max(inv_sv)))
    sqrt_half = np.float32(math.sqrt(0.5))
    per_token = lambda s: jnp.broadcast_to(s, (t // tb, tb, 1)).reshape(t, 1)
    return _peer(h2, u8, vt8, r2, e2, n1, c1, x1, mod3, _row(ln2_g), _row(ln2_b), inv_su * sqrt_half,
                 inv_sv * sqrt_half, 1.0 / sh, sp, per_token(sh), per_token(1.0 / sp), seq, alpha)


def _one_hot_cols(n_rows, n_cols, group_size, row_offset=0):
    r = np.arange(n_rows)[:, None]
    c = np.arange(n_cols)[None, :]
    return jnp.asarray(r == row_offset + c // group_size, dtype=BF16)


def _row(v):
    return v.reshape(1, -1).astype(F32)


def _pad_lanes(v):
    return jnp.pad(v, (0, LANES - v.shape[0])).reshape(1, LANES).astype(F32)


def kernel(x, c, w_ada, b_ada, w_in, conv_ssd_w, conv_ssd_b, dt_bias_f, dt_bias_b, a_log_f, a_log_b, d_skip, ssd_norm_g, short_conv_w, sc_norm_g, w_out, ln1_g, ln1_b, w_query, sub_keys, expert_u, expert_v, ln2_g, ln2_b):
    batch, seq, d = x.shape
    depth = w_ada.shape[0]
    alpha = (2.0 * depth) ** 0.25
    t = batch * seq
    x2 = x.reshape(t, d)
    c_pad = jnp.pad(c, ((0, SUBLANES - batch % SUBLANES), (0, 0))) if batch % SUBLANES else c

    e_f = _one_hot_cols(LANES, SSD_WIDTH, SSD_HEADDIM, 0)
    e_b = _one_hot_cols(LANES, SSD_WIDTH, SSD_HEADDIM, SSD_HEADS)
    e4 = _one_hot_cols(LANES, SSD_WIDTH, SSD_WIDTH // SSD_GROUPS)
    e32 = _one_hot_cols(LANES, SC_WIDTH, SC_WIDTH // SC_GROUPS)
    r4 = e4.T
    r32 = e32.T

    for i in range(depth):
        mod = _ada(c_pad, w_ada[i], b_ada[i].reshape(1, -1))
        mod3 = mod[:batch].reshape(batch, 6, d)

        w_main, wd_hi, wd_lo = _win_prep(w_in[i])
        proj, dt_raw = _inproj(x2, mod3, w_main, wd_hi, wd_lo, seq)

        xbc = _conv(proj, conv_ssd_w[i], conv_ssd_b[i].reshape(1, -1), seq)
        bias = _pad_lanes(jnp.concatenate([dt_bias_f[i], dt_bias_b[i]]))
        alog = _pad_lanes(jnp.concatenate([a_log_f[i], a_log_b[i]]))
        dskip_x = _row(jnp.repeat(d_skip[i], SSD_HEADDIM))
        y_f, y_b = _ssd(xbc, dt_raw, bias, alog, dskip_x, e_f, e_b, batch, seq)

        x1, h2, h_amax, h_sumsq = _post(y_f, y_b, proj, x2, mod3, short_conv_w[i], _row(ssd_norm_g[i]),
                                        _row(sc_norm_g[i]), r4, e4, r32, e32, w_out[i].astype(BF16),
                                        _row(ln1_g[i]), _row(ln1_b[i]), seq, alpha)

        x2 = _ffn(h2, h_amax, h_sumsq, x1, mod3, w_query[i], sub_keys[i], expert_u[i], expert_v[i],
                  ln2_g[i], ln2_b[i], seq, alpha)
    return x2.reshape(batch, seq, d)
```

```python
import functools
import math

import jax
import jax.numpy as jnp
import numpy as np
from jax import lax
from jax.experimental import pallas as pl
from jax.experimental.pallas import tpu as pltpu

F32 = jnp.float32
BF16 = jnp.bfloat16
FP8 = jnp.float8_e4m3fn
FP8_TARGET = 256.0
FP8_SCALE_CAP = 2.0 ** 60

D_MODEL = 2048
SSD_WIDTH = D_MODEL
SSD_HEADDIM = 64
SSD_HEADS = SSD_WIDTH // SSD_HEADDIM
SSD_GROUPS = 4
SSD_STATE = 128
SSD_CHUNK = 128
HEADS_PER_GROUP = SSD_HEADS // SSD_GROUPS
GROUP_WIDTH = HEADS_PER_GROUP * SSD_HEADDIM
SC_WIDTH = D_MODEL
SC_GROUPS = 32
XBC_WIDTH = SSD_WIDTH + 2 * SSD_GROUPS * SSD_STATE
MAIN_COLS = 4 * D_MODEL + XBC_WIDTH
XBC_COL0 = 4 * D_MODEL
PEER_HEADS = 8
N_KEYS = 128
PEER_TOPK = 16
D_QUERY = 512
HALF_QUERY = D_QUERY // 2
NORM_EPS = 1e-5
LANES = 128
SUBLANES = 8
HALO_ROWS = 16
POST_COL_PIECE = 256
PEER_TOKEN_BLOCK = 512
PEER_OUT_PIECE = 256
VMEM_LIMIT = 56 * 1024 * 1024


def _dot(a, b):
    return jnp.dot(a, b, preferred_element_type=F32)


def _dot_nt(a, b):
    return lax.dot_general(a, b, (((1,), (1,)), ((), ())), preferred_element_type=F32)


def _split2(x):
    hi = x.astype(BF16)
    lo = (x - hi.astype(F32)).astype(BF16)
    return hi, lo


def _split3(x):
    hi = x.astype(BF16)
    r = x - hi.astype(F32)
    mid = r.astype(BF16)
    lo = (r - mid.astype(F32)).astype(BF16)
    return hi, mid, lo


def _softplus(x):
    return jnp.maximum(x, 0.0) + jnp.log1p(jnp.exp(-jnp.abs(x)))


def _silu(x):
    return x * (1.0 / (1.0 + jnp.exp(-x)))


def _params(sem):
    return pltpu.CompilerParams(dimension_semantics=sem, vmem_limit_bytes=VMEM_LIMIT)


def _ada_kernel(c_ref, w_ref, b_ref, o_ref):
    sc = _silu(c_ref[...])
    o_ref[...] = _dot(sc.astype(BF16), w_ref[...].astype(BF16)) + b_ref[...]


def _ada(c_pad, w, b):
    rows, d = c_pad.shape
    n = w.shape[1]
    bn = 1024
    return pl.pallas_call(
        _ada_kernel,
        grid=(n // bn,),
        in_specs=[pl.BlockSpec((rows, d), lambda j: (0, 0)),
                  pl.BlockSpec((d, bn), lambda j: (0, j)),
                  pl.BlockSpec((1, bn), lambda j: (0, j))],
        out_specs=pl.BlockSpec((rows, bn), lambda j: (0, j)),
        out_shape=jax.ShapeDtypeStruct((rows, n), F32),
        compiler_params=_params(("arbitrary",)),
        name="ada",
    )(c_pad, w, b)


DT_COL0 = SSD_WIDTH + XBC_WIDTH
GATES_COL0 = DT_COL0 + 2 * SSD_HEADS


def _win_prep_kernel(w_ref, main_ref, dh_ref, dl_ref):
    w = w_ref[...]
    main_ref[:, 0:SSD_WIDTH] = w[:, 0:SSD_WIDTH].astype(BF16)
    main_ref[:, SSD_WIDTH:XBC_COL0] = w[:, GATES_COL0:GATES_COL0 + 3 * SC_WIDTH].astype(BF16)
    main_ref[:, XBC_COL0:MAIN_COLS] = w[:, SSD_WIDTH:DT_COL0].astype(BF16)
    dt = w[:, DT_COL0:DT_COL0 + LANES]
    lane = lax.broadcasted_iota(jnp.int32, dt.shape, 1)
    hi, lo = _split2(jnp.where(lane < 2 * SSD_HEADS, dt, 0.0))
    dh_ref[...] = hi
    dl_ref[...] = lo


def _win_prep(w):
    d, n = w.shape
    rows = 128
    return pl.pallas_call(
        _win_prep_kernel,
        grid=(d // rows,),
        in_specs=[pl.BlockSpec((rows, n), lambda i: (i, 0))],
        out_specs=[pl.BlockSpec((rows, MAIN_COLS), lambda i: (i, 0)),
                   pl.BlockSpec((rows, LANES), lambda i: (i, 0)),
                   pl.BlockSpec((rows, LANES), lambda i: (i, 0))],
        out_shape=[jax.ShapeDtypeStruct((d, MAIN_COLS), BF16), jax.ShapeDtypeStruct((d, LANES), BF16),
                   jax.ShapeDtypeStruct((d, LANES), BF16)],
        compiler_params=_params(("parallel",)),
        name="winprep",
    )(w)


def _inproj_kernel(x_ref, mod_ref, w_ref, wdh_ref, wdl_ref, o_ref, dt_ref, h_scr):
    @pl.when(pl.program_id(1) == 0)
    def _():
        m = mod_ref[0]
        h = x_ref[...] * (1.0 + m[1:2]) + m[0:1]
        hi, lo = _split2(h)
        h_scr[...] = hi
        dt_ref[...] = _dot(hi, wdh_ref[...]) + _dot(lo, wdh_ref[...]) + _dot(hi, wdl_ref[...])

    o_ref[...] = _dot(h_scr[...], w_ref[...]).astype(o_ref.dtype)


def _inproj(x2, mod3, w_main, wd_hi, wd_lo, seq):
    t, d = x2.shape
    n = w_main.shape[1]
    bm = min(1024, seq)
    bn = 1024
    per_batch = seq // bm
    return pl.pallas_call(
        _inproj_kernel,
        grid=(t // bm, n // bn),
        in_specs=[pl.BlockSpec((bm, d), lambda i, j: (i, 0)),
                  pl.BlockSpec((1, 6, d), lambda i, j: (i // per_batch, 0, 0)),
                  pl.BlockSpec((d, bn), lambda i, j: (0, j)),
                  pl.BlockSpec((d, LANES), lambda i, j: (0, 0)),
                  pl.BlockSpec((d, LANES), lambda i, j: (0, 0))],
        out_specs=[pl.BlockSpec((bm, bn), lambda i, j: (i, j)),
                   pl.BlockSpec((bm, LANES), lambda i, j: (i, 0))],
        out_shape=[jax.ShapeDtypeStruct((t, n), BF16),
                   jax.ShapeDtypeStruct((t, LANES), F32)],
        scratch_shapes=[pltpu.VMEM((bm, d), BF16)],
        compiler_params=_params(("parallel", "arbitrary")),
        name="inproj",
    )(x2, mod3, w_main, wd_hi, wd_lo)


def _shifted_rows(u, prev_row, next_row):
    rows = u.shape[0]
    ridx = lax.broadcasted_iota(jnp.int32, u.shape, 0)
    up = jnp.where(ridx == 0, prev_row, pltpu.roll(u, 1, axis=0))
    un = jnp.where(ridx == rows - 1, next_row, pltpu.roll(u, rows - 1, axis=0))
    return up, un


def _halo_specs(bm, bw, col_of, t):
    per = bm // HALO_ROWS
    last = t // HALO_ROWS - 1
    prev = pl.BlockSpec((HALO_ROWS, bw), lambda i, j: (jnp.maximum(i * per - 1, 0), col_of(j)))
    nxt = pl.BlockSpec((HALO_ROWS, bw), lambda i, j: (jnp.minimum((i + 1) * per, last), col_of(j)))
    return prev, nxt


def _conv_kernel(u_ref, p_ref, n_ref, w_ref, b_ref, o_ref, *, bm, seq):
    i = pl.program_id(0)
    u = u_ref[...].astype(F32)
    has_prev = ((i * bm) % seq != 0).astype(F32)
    has_next = (((i + 1) * bm) % seq != 0).astype(F32)
    up, un = _shifted_rows(u, p_ref[HALO_ROWS - 1:HALO_ROWS, :].astype(F32) * has_prev,
                           n_ref[0:1, :].astype(F32) * has_next)
    w = w_ref[...]
    o_ref[...] = _silu(up * w[0:1] + u * w[1:2] + un * w[2:3] + b_ref[...])


def _conv(proj, conv_w, conv_b, seq):
    t = proj.shape[0]
    bm = min(1024, seq)
    bw = 1024
    col0 = XBC_COL0 // bw
    col_of = lambda j: col0 + j
    prev, nxt = _halo_specs(bm, bw, col_of, t)
    return pl.pallas_call(
        functools.partial(_conv_kernel, bm=bm, seq=seq),
        grid=(t // bm, XBC_WIDTH // bw),
        in_specs=[pl.BlockSpec((bm, bw), lambda i, j: (i, col_of(j))), prev, nxt,
                  pl.BlockSpec((3, bw), lambda i, j: (0, j)),
                  pl.BlockSpec((1, bw), lambda i, j: (0, j))],
        out_specs=pl.BlockSpec((bm, bw), lambda i, j: (i, j)),
        out_shape=jax.ShapeDtypeStruct((t, XBC_WIDTH), F32),
        compiler_params=_params(("parallel", "parallel")),
        name="ssdconv",
    )(proj, proj, proj, conv_w, conv_b)


def _ssd_direction(x_ref, b_ref, c_ref, dt_ref, e_ref, h_scr, y_ref, bias, a, dskip, backward):
    cs_len = SSD_CHUNK
    row = lax.broadcasted_iota(jnp.int32, (cs_len, cs_len), 0)
    col = lax.broadcasted_iota(jnp.int32, (cs_len, cs_len), 1)
    tri = (col <= row).astype(BF16)
    dt = _softplus(dt_ref[...] + bias)
    adt = dt * a
    h3 = _split3(adt)
    cs = _dot(tri, h3[0]) + _dot(tri, h3[1]) + _dot(tri, h3[2])
    tot = cs[cs_len - 1:cs_len, :]
    if backward:
        ecs = cs - adt
        p = -ecs
        wst = dt * jnp.exp(ecs)
        indec = jnp.exp(tot - ecs)
        mask = col >= row
    else:
        p = cs
        wst = dt * jnp.exp(tot - cs)
        indec = jnp.exp(cs)
        mask = col <= row
    pt = p.T
    e = e_ref[...]
    dtx = _dot(dt.astype(BF16), e)
    wstx = _dot(wst.astype(BF16), e)
    indx = _dot(indec.astype(BF16), e)
    xs = x_ref[...]
    xd = (xs * dtx).astype(BF16)
    xw = (xs * wstx).astype(BF16)
    col_off = SSD_HEADS if backward else 0
    pair = 2 * SSD_HEADDIM
    first_of_pair = lax.broadcasted_iota(jnp.int32, (cs_len, pair), 1) < SSD_HEADDIM
    zero = jnp.zeros((), BF16)
    for g in range(SSD_GROUPS):
        gs = slice(g * GROUP_WIDTH, (g + 1) * GROUP_WIDTH)
        bg = b_ref[:, g * SSD_STATE:(g + 1) * SSD_STATE]
        cg = c_ref[:, g * SSD_STATE:(g + 1) * SSD_STATE].astype(BF16)
        cb = _dot_nt(cg, bg.astype(BF16))
        h_in = h_scr[g]
        y_off = _dot(cg, h_in.astype(BF16)) * indx[:, gs]
        st = _dot(bg.T.astype(BF16), xw[:, gs])
        for r in range(0, HEADS_PER_GROUP, 2):
            hd = g * HEADS_PER_GROUP + r
            ms = []
            for ci in (hd + col_off, hd + col_off + 1):
                seg = p[:, ci:ci + 1] - pt[ci:ci + 1, :]
                lmat = jnp.exp(jnp.where(mask, seg, -jnp.inf))
                ms.append((cb * lmat).astype(BF16))
            hs = slice(hd * SSD_HEADDIM, (hd + 2) * SSD_HEADDIM)
            x2h = xd[:, hs]
            rhs = jnp.concatenate([jnp.where(first_of_pair, x2h, zero), jnp.where(first_of_pair, zero, x2h)], axis=0)
            y = _dot(jnp.concatenate(ms, axis=1), rhs) + y_off[:, r * SSD_HEADDIM:(r + 2) * SSD_HEADDIM]
            if dskip is not None:
                y = y + xs[:, hs] * dskip[:, hs]
            y_ref[:, hs] = y
        edge = 0 if backward else cs_len - 1
        h_scr[g] = indx[edge:edge + 1, gs] * h_in + st


def _ssd_kernel(xf_ref, bf_ref, cf_ref, dtf_ref, xb_ref, bb_ref, cb_ref, dtb_ref,
                bias_ref, alog_ref, dskip_ref, ef_ref, eb_ref, yf_ref, yb_ref, hf_scr, hb_scr):
    @pl.when(pl.program_id(1) == 0)
    def _():
        hf_scr[...] = jnp.zeros_like(hf_scr)
        hb_scr[...] = jnp.zeros_like(hb_scr)

    bias = bias_ref[...]
    a = -jnp.exp(alog_ref[...])
    _ssd_direction(xf_ref, bf_ref, cf_ref, dtf_ref, ef_ref, hf_scr, yf_ref, bias, a, dskip_ref[...], False)
    _ssd_direction(xb_ref, bb_ref, cb_ref, dtb_ref, eb_ref, hb_scr, yb_ref, bias, a, None, True)


def _ssd(xbc, dt_raw, bias, alog, dskip_x, e_f, e_b, batch, seq):
    t = xbc.shape[0]
    nc = seq // SSD_CHUNK
    cl = SSD_CHUNK
    gn = SSD_GROUPS * SSD_STATE
    bcol = SSD_WIDTH // gn
    fwd = lambda b, k: b * nc + k
    bwd = lambda b, k: b * nc + (nc - 1 - k)

    def specs(ch):
        return [pl.BlockSpec((cl, SSD_WIDTH), lambda b, k: (ch(b, k), 0)),
                pl.BlockSpec((cl, gn), lambda b, k: (ch(b, k), bcol)),
                pl.BlockSpec((cl, gn), lambda b, k: (ch(b, k), bcol + 1)),
                pl.BlockSpec((cl, LANES), lambda b, k: (ch(b, k), 0))]

    const = lambda shape: pl.BlockSpec(shape, lambda b, k: (0, 0))
    state = pltpu.VMEM((SSD_GROUPS, SSD_STATE, GROUP_WIDTH), F32)
    return pl.pallas_call(
        _ssd_kernel,
        grid=(batch, nc),
        in_specs=specs(fwd) + specs(bwd) + [const((1, LANES)), const((1, LANES)), const((1, SSD_WIDTH)),
                                            const((LANES, SSD_WIDTH)), const((LANES, SSD_WIDTH))],
        out_specs=[pl.BlockSpec((cl, SSD_WIDTH), lambda b, k: (fwd(b, k), 0)),
                   pl.BlockSpec((cl, SSD_WIDTH), lambda b, k: (bwd(b, k), 0))],
        out_shape=[jax.ShapeDtypeStruct((t, SSD_WIDTH), F32)] * 2,
        scratch_shapes=[state, state],
        compiler_params=_params(("parallel", "arbitrary")),
        name="ssd",
    )(xbc, xbc, xbc, dt_raw, xbc, xbc, xbc, dt_raw, bias, alog, dskip_x, e_f, e_b)


def _group_rms(y, r_ref, e_ref, group_size):
    gsum = _dot((y * y).astype(BF16), r_ref[...])
    inv = lax.rsqrt(gsum * (1.0 / group_size) + NORM_EPS)
    return y * _dot(inv.astype(BF16), e_ref[...])


def _layer_norm(v, g, b):
    mu = jnp.mean(v, axis=-1, keepdims=True)
    vc = v - mu
    var = jnp.mean(vc * vc, axis=-1, keepdims=True)
    return vc * lax.rsqrt(var + NORM_EPS) * g + b


def _post_kernel(yf_ref, yb_ref, z_ref, gb_ref, gc_ref, gcp_ref, gcn_ref, v_ref, vp_ref, vn_ref,
                 x_ref, mod_ref, scw_ref, ssdg_ref, scg_ref, r4_ref, e4_ref, r32_ref, e32_ref,
                 wout_ref, lng_ref, lnb_ref, x1_ref, h2_ref, amax_ref, sumsq_ref, *, bm, seq, alpha):
    i = pl.program_id(0)
    m = mod_ref[0]
    y = (yf_ref[...] + yb_ref[...]) * _silu(z_ref[...].astype(F32))
    y_ssd = (_group_rms(y, r4_ref, e4_ref, SSD_WIDTH // SSD_GROUPS) * ssdg_ref[...]).astype(BF16)
    n_cols = [slice(c, c + POST_COL_PIECE) for c in range(0, wout_ref.shape[1], POST_COL_PIECE)]
    mix_ssd = [_dot(y_ssd, wout_ref[0:SSD_WIDTH, cs]) for cs in n_cols]

    f32 = lambda ref, rows=slice(None): ref[rows, :].astype(F32)
    u = f32(gc_ref) * f32(v_ref)
    has_prev = ((i * bm) % seq != 0).astype(F32)
    has_next = (((i + 1) * bm) % seq != 0).astype(F32)
    last = slice(HALO_ROWS - 1, HALO_ROWS)
    first = slice(0, 1)
    up, un = _shifted_rows(u, f32(gcp_ref, last) * f32(vp_ref, last) * has_prev,
                           f32(gcn_ref, first) * f32(vn_ref, first) * has_next)
    w = scw_ref[...]
    y_sc = f32(gb_ref) * (up * w[0:1] + u * w[1:2] + un * w[2:3])
    y_sc = (_group_rms(y_sc, r32_ref, e32_ref, SC_WIDTH // SC_GROUPS) * scg_ref[...]).astype(BF16)
    mix = jnp.concatenate([part + _dot(y_sc, wout_ref[SSD_WIDTH:SSD_WIDTH + SC_WIDTH, cs])
                           for part, cs in zip(mix_ssd, n_cols)], axis=1)
    x1 = _layer_norm(alpha * x_ref[...] + m[2:3] * mix, lng_ref[...], lnb_ref[...])
    x1_ref[...] = x1
    h2 = x1 * (1.0 + m[4:5]) + m[3:4]
    h2_ref[...] = h2.astype(BF16)
    amax_ref[...] = jnp.max(jnp.abs(h2), axis=-1, keepdims=True)
    sumsq_ref[...] = jnp.sum(h2 * h2, axis=-1, keepdims=True)


def _single(shape, index_map):
    return pl.BlockSpec(shape, index_map, pipeline_mode=pl.Buffered(1))


def _post(yf, yb, proj, x2, mod3, scw, ssdg, scg, r4, e4, r32, e32, wout, lng, lnb, seq, alpha):
    t, d = x2.shape
    bm = min(256, seq)
    per_batch = seq // bm
    per = bm // HALO_ROWS
    last = t // HALO_ROWS - 1
    main = lambda c: pl.BlockSpec((bm, d), lambda i: (i, c))
    prev = lambda c: pl.BlockSpec((HALO_ROWS, d), lambda i: (jnp.maximum(i * per - 1, 0), c))
    nxt = lambda c: pl.BlockSpec((HALO_ROWS, d), lambda i: (jnp.minimum((i + 1) * per, last), c))
    const = lambda shape: _single(shape, lambda i: (0, 0))
    return pl.pallas_call(
        functools.partial(_post_kernel, bm=bm, seq=seq, alpha=alpha),
        grid=(t // bm,),
        in_specs=[main(0), main(0), main(0), main(1), main(2), prev(2), nxt(2), main(3), prev(3), nxt(3),
                  main(0), pl.BlockSpec((1, 6, d), lambda i: (i // per_batch, 0, 0)),
                  const((3, d)), const((1, d)), const((1, d)),
                  const((d, LANES)), const((LANES, d)), const((d, LANES)), const((LANES, d)),
                  const((2 * d, d)), const((1, d)), const((1, d))],
        out_specs=[pl.BlockSpec((bm, d), lambda i: (i, 0)), pl.BlockSpec((bm, d), lambda i: (i, 0)),
                   pl.BlockSpec((bm, 1), lambda i: (i, 0)), pl.BlockSpec((bm, 1), lambda i: (i, 0))],
        out_shape=[jax.ShapeDtypeStruct((t, d), F32), jax.ShapeDtypeStruct((t, d), BF16),
                   jax.ShapeDtypeStruct((t, 1), F32), jax.ShapeDtypeStruct((t, 1), F32)],
        compiler_params=_params(("parallel",)),
        name="post",
    )(yf, yb, proj, proj, proj, proj, proj, proj, proj, proj, x2, mod3,
      scw, ssdg, scg, r4, e4, r32, e32, wout, lng, lnb)


def _top_values(s, n, want_rank=False):
    vals = []
    rank = jnp.full(s.shape, float(n), F32) if want_rank else None
    for r in range(n):
        m = jnp.max(s, axis=0, keepdims=True)
        vals.append(m)
        hit = s == m
        if want_rank:
            rank = jnp.where(hit, float(r), rank)
        s = jnp.where(hit, -jnp.inf, s)
    return (vals, rank) if want_rank else vals


def _query_kernel(h2_ref, wq_ref, k_ref, r2_ref, e2_ref, n1_ref, c1_ref):
    tb = h2_ref.shape[0]
    for h in range(PEER_HEADS):
        q = _dot(h2_ref[...], wq_ref[:, h * D_QUERY:(h + 1) * D_QUERY]).astype(BF16)
        s1 = _dot_nt(k_ref[h, 0], q[:, :HALF_QUERY])
        s2 = _dot_nt(k_ref[h, 1], q[:, HALF_QUERY:])
        v1 = _top_values(s1, PEER_TOPK)
        v2, r2 = _top_values(s2, PEER_TOPK, want_rank=True)
        pairs = [(a, b) for a in range(PEER_TOPK) for b in range(PEER_TOPK) if (a + 1) * (b + 1) <= PEER_TOPK]
        sums = {ab: v1[ab[0]] + v2[ab[1]] for ab in pairs}
        rows = [sums[ab] for ab in pairs]
        rows += [jnp.full((1, tb), -jnp.inf, F32)] * ((-len(rows)) % SUBLANES)
        top = _top_values(jnp.concatenate(rows, axis=0), PEER_TOPK)
        z = jnp.ones((1, tb), F32)
        for kk in range(1, PEER_TOPK):
            z = z + jnp.exp(top[kk] - top[0])
        tau = top[PEER_TOPK - 1]
        n1 = jnp.zeros(s1.shape, F32)
        for a in range(PEER_TOPK):
            cnt = sum(jnp.where(sums[(a, b)] >= tau, 1.0, 0.0) for b in range(PEER_TOPK) if (a, b) in sums)
            n1 = jnp.where(s1 == v1[a], cnt, n1)
        r2_ref[h] = r2.astype(BF16)
        e2_ref[h] = jnp.exp(s2 - v2[0]).astype(BF16)
        n1_ref[h] = n1
        c1_ref[h] = jnp.exp(s1 - v1[0]) * (1.0 / z)


def _query(h2, wq, keys):
    t, d = h2.shape
    tb = min(256, t)
    ospec = pl.BlockSpec((PEER_HEADS, N_KEYS, tb), lambda i: (0, 0, i))
    return pl.pallas_call(
        _query_kernel,
        grid=(t // tb,),
        in_specs=[pl.BlockSpec((tb, d), lambda i: (i, 0)),
                  _single((d, PEER_HEADS * D_QUERY), lambda i: (0, 0)),
                  _single((PEER_HEADS, 2, N_KEYS, HALF_QUERY), lambda i: (0, 0, 0, 0))],
        out_specs=[ospec] * 4,
        out_shape=[jax.ShapeDtypeStruct((PEER_HEADS, N_KEYS, t), dt) for dt in (BF16, BF16, F32, F32)],
        compiler_params=_params(("parallel",)),
        name="peerquery",
    )(h2, wq, keys)


def _pow2_scale(bound):
    ratio = FP8_TARGET / jnp.where(bound > 0, bound, FP8_TARGET)
    exponent_only = lax.bitcast_convert_type(ratio, jnp.int32) & jnp.int32(0x7F800000)
    return jnp.minimum(lax.bitcast_convert_type(exponent_only, F32), FP8_SCALE_CAP)


def _expert_prep_kernel(w_ref, q_ref, inv_ref, sumsq_ref, *, transpose):
    w = w_ref[...]
    s = _pow2_scale(jnp.max(jnp.abs(w), axis=1, keepdims=True))
    inv_ref[...] = 1.0 / s
    sumsq_ref[...] = jnp.sum(w * w, axis=1, keepdims=True)
    ws = w * s
    q_ref[...] = (ws.T if transpose else ws).astype(FP8)


def _expert_prep(table, transpose):
    n_exp, d = table.shape
    rows = 1024
    col = pl.BlockSpec((rows, 1), lambda i: (i, 0))
    q_spec = pl.BlockSpec((d, rows), lambda i: (0, i)) if transpose else pl.BlockSpec((rows, d), lambda i: (i, 0))
    q_shape = (d, n_exp) if transpose else (n_exp, d)
    return pl.pallas_call(
        functools.partial(_expert_prep_kernel, transpose=transpose),
        grid=(n_exp // rows,),
        in_specs=[pl.BlockSpec((rows, d), lambda i: (i, 0))],
        out_specs=[q_spec, col, col],
        out_shape=[jax.ShapeDtypeStruct(q_shape, FP8), jax.ShapeDtypeStruct((n_exp, 1), F32),
                   jax.ShapeDtypeStruct((n_exp, 1), F32)],
        compiler_params=_params(("parallel",)),
        name="expertprep_t" if transpose else "expertprep",
    )(table)


def _peer_kernel(h2_ref, u_ref, vt_ref, r2_ref, e2_ref, n1_ref, c1_ref, x1_ref, mod_ref, lng_ref, lnb_ref,
                 iu_ref, iv_ref, ish_ref, sp_ref, sh_ref, isp_ref, o_ref, acc_ref, h8_ref, *, alpha):
    e = pl.program_id(1)

    @pl.when(e == 0)
    def _():
        acc_ref[...] = jnp.zeros_like(acc_ref)
        h8_ref[...] = (h2_ref[...].astype(F32) * sh_ref[...]).T.astype(FP8)

    n1 = n1_ref[...].astype(BF16)
    c1 = c1_ref[...].astype(BF16)
    parts = []
    for ii in range(u_ref.shape[0] // N_KEYS):
        rows = slice(ii * N_KEYS, (ii + 1) * N_KEYS)
        at = _dot(u_ref[rows, :], h8_ref[...])
        g = None
        for h in range(PEER_HEADS):
            w = jnp.where(r2_ref[h] < n1[h, ii:ii + 1, :], e2_ref[h], jnp.zeros((), BF16)) * c1[h, ii:ii + 1, :]
            g = w if g is None else g + w
        b = at * (iu_ref[rows, :] * ish_ref[...])
        p = b * (1.0 + lax.erf(b)) * (iv_ref[rows, :] * sp_ref[...])
        parts.append((p.astype(BF16) * g).astype(FP8))
    pt = jnp.concatenate(parts, axis=0)
    for c in range(0, vt_ref.shape[0], PEER_OUT_PIECE):
        dr = slice(c, c + PEER_OUT_PIECE)
        acc_ref[dr, :] += _dot(vt_ref[dr, :], pt)

    @pl.when(e == pl.num_programs(1) - 1)
    def _():
        m = mod_ref[0]
        ffn = acc_ref[...].T * isp_ref[...]
        o_ref[...] = _layer_norm(alpha * x1_ref[...] + m[5:6] * ffn, lng_ref[...], lnb_ref[...])


def _peer(h2, u8, vt8, r2, e2, n1, c1, x1, mod3, lng, lnb, iu, iv, ish_blk, sp_blk, sh, isp, seq, alpha):
    t, d = h2.shape
    n_exp = u8.shape[0]
    tb = min(PEER_TOKEN_BLOCK, seq)
    eb = 1024
    per_batch = seq // tb
    kb = eb // N_KEYS
    full = pl.BlockSpec((PEER_HEADS, N_KEYS, tb), lambda i, j: (0, 0, i))
    part = pl.BlockSpec((PEER_HEADS, kb, tb), lambda i, j: (0, j, i))
    ecol = pl.BlockSpec((eb, 1), lambda i, j: (j, 0))
    blk = pl.BlockSpec((None, 1, 1), lambda i, j: (i, 0, 0))
    tcol = pl.BlockSpec((tb, 1), lambda i, j: (i, 0))
    return pl.pallas_call(
        functools.partial(_peer_kernel, alpha=alpha),
        grid=(t // tb, n_exp // eb),
        in_specs=[pl.BlockSpec((tb, d), lambda i, j: (i, 0)),
                  pl.BlockSpec((eb, d), lambda i, j: (j, 0)),
                  pl.BlockSpec((d, eb), lambda i, j: (0, j)),
                  full, full, part, part,
                  pl.BlockSpec((tb, d), lambda i, j: (i, 0)),
                  pl.BlockSpec((1, 6, d), lambda i, j: (i // per_batch, 0, 0)),
                  pl.BlockSpec((1, d), lambda i, j: (0, 0)),
                  pl.BlockSpec((1, d), lambda i, j: (0, 0)),
                  ecol, ecol, blk, blk, tcol, tcol],
        out_specs=pl.BlockSpec((tb, d), lambda i, j: (i, 0)),
        out_shape=jax.ShapeDtypeStruct((t, d), F32),
        scratch_shapes=[pltpu.VMEM((d, tb), F32), pltpu.VMEM((d, tb), FP8)],
        compiler_params=_params(("parallel", "arbitrary")),
        name="peer",
    )(h2, u8, vt8, r2, e2, n1, c1, x1, mod3, lng, lnb, iu, iv, ish_blk, sp_blk, sh, isp)


def _ffn(h2, h_amax, h_sumsq, x1, mod3, w_query, sub_keys, expert_u, expert_v, ln2_g, ln2_b, seq, alpha):
    t = h2.shape[0]
    r2, e2, n1, c1 = _query(h2, w_query.astype(BF16), sub_keys.astype(BF16))
    u8, inv_su, u_sumsq = _expert_prep(expert_u, transpose=False)
    vt8, inv_sv, _ = _expert_prep(expert_v, transpose=True)
    tb = min(PEER_TOKEN_BLOCK, seq)
    per_block = lambda v: jnp.max(v.reshape(t // tb, tb), axis=1).reshape(-1, 1, 1)
    sh = _pow2_scale(per_block(h_amax))
    sp = _pow2_scale(jnp.sqrt(jnp.max(u_sumsq) * per_block(h_sumsq)) * (PEER_HEADS * jnp.max(inv_sv)))
    sqrt_half = np.float32(math.sqrt(0.5))
    per_token = lambda s: jnp.broadcast_to(s, (t // tb, tb, 1)).reshape(t, 1)
    return _peer(h2, u8, vt8, r2, e2, n1, c1, x1, mod3, _row(ln2_g), _row(ln2_b), inv_su * sqrt_half,
                 inv_sv * sqrt_half, 1.0 / sh, sp, per_token(sh), per_token(1.0 / sp), seq, alpha)


def _one_hot_cols(n_rows, n_cols, group_size, row_offset=0):
    r = np.arange(n_rows)[:, None]
    c = np.arange(n_cols)[None, :]
    return jnp.asarray(r == row_offset + c // group_size, dtype=BF16)


def _row(v):
    return v.reshape(1, -1).astype(F32)


def _pad_lanes(v):
    return jnp.pad(v, (0, LANES - v.shape[0])).reshape(1, LANES).astype(F32)


def kernel(x, c, w_ada, b_ada, w_in, conv_ssd_w, conv_ssd_b, dt_bias_f, dt_bias_b, a_log_f, a_log_b, d_skip, ssd_norm_g, short_conv_w, sc_norm_g, w_out, ln1_g, ln1_b, w_query, sub_keys, expert_u, expert_v, ln2_g, ln2_b):
    batch, seq, d = x.shape
    depth = w_ada.shape[0]
    alpha = (2.0 * depth) ** 0.25
    t = batch * seq
    x2 = x.reshape(t, d)
    c_pad = jnp.pad(c, ((0, SUBLANES - batch % SUBLANES), (0, 0))) if batch % SUBLANES else c

    e_f = _one_hot_cols(LANES, SSD_WIDTH, SSD_HEADDIM, 0)
    e_b = _one_hot_cols(LANES, SSD_WIDTH, SSD_HEADDIM, SSD_HEADS)
    e4 = _one_hot_cols(LANES, SSD_WIDTH, SSD_WIDTH // SSD_GROUPS)
    e32 = _one_hot_cols(LANES, SC_WIDTH, SC_WIDTH // SC_GROUPS)
    r4 = e4.T
    r32 = e32.T

    for i in range(depth):
        mod = _ada(c_pad, w_ada[i], b_ada[i].reshape(1, -1))
        mod3 = mod[:batch].reshape(batch, 6, d)

        w_main, wd_hi, wd_lo = _win_prep(w_in[i])
        proj, dt_raw = _inproj(x2, mod3, w_main, wd_hi, wd_lo, seq)

        xbc = _conv(proj, conv_ssd_w[i], conv_ssd_b[i].reshape(1, -1), seq)
        bias = _pad_lanes(jnp.concatenate([dt_bias_f[i], dt_bias_b[i]]))
        alog = _pad_lanes(jnp.concatenate([a_log_f[i], a_log_b[i]]))
        dskip_x = _row(jnp.repeat(d_skip[i], SSD_HEADDIM))
        y_f, y_b = _ssd(xbc, dt_raw, bias, alog, dskip_x, e_f, e_b, batch, seq)

        x1, h2, h_amax, h_sumsq = _post(y_f, y_b, proj, x2, mod3, short_conv_w[i], _row(ssd_norm_g[i]),
                                        _row(sc_norm_g[i]), r4, e4, r32, e32, w_out[i].astype(BF16),
                                        _row(ln1_g[i]), _row(ln1_b[i]), seq, alpha)

        x2 = _ffn(h2, h_amax, h_sumsq, x1, mod3, w_query[i], sub_keys[i], expert_u[i], expert_v[i],
                  ln2_g[i], ln2_b[i], seq, alpha)
    return x2.reshape(batch, seq, d)
```

```python
import functools
import math

import jax
import jax.numpy as jnp
import numpy as np
from jax import lax
from jax.experimental import pallas as pl
from jax.experimental.pallas import tpu as pltpu

F32 = jnp.float32
BF16 = jnp.bfloat16
FP8 = jnp.float8_e4m3fn
FP8_TARGET = 256.0
FP8_SCALE_CAP = 2.0 ** 60

D_MODEL = 2048
SSD_WIDTH = D_MODEL
SSD_HEADDIM = 64
SSD_HEADS = SSD_WIDTH // SSD_HEADDIM
SSD_GROUPS = 4
SSD_STATE = 128
SSD_CHUNK = 128
HEADS_PER_GROUP = SSD_HEADS // SSD_GROUPS
GROUP_WIDTH = HEADS_PER_GROUP * SSD_HEADDIM
SC_WIDTH = D_MODEL
SC_GROUPS = 32
XBC_WIDTH = SSD_WIDTH + 2 * SSD_GROUPS * SSD_STATE
MAIN_COLS = 4 * D_MODEL + XBC_WIDTH
XBC_COL0 = 4 * D_MODEL
PEER_HEADS = 8
N_KEYS = 128
PEER_TOPK = 16
D_QUERY = 512
HALF_QUERY = D_QUERY // 2
NORM_EPS = 1e-5
LANES = 128
SUBLANES = 8
HALO_ROWS = 16
POST_COL_PIECE = 256
PEER_TOKEN_BLOCK = 512
PEER_OUT_PIECE = 256
VMEM_LIMIT = 56 * 1024 * 1024


def _dot(a, b):
    return jnp.dot(a, b, preferred_element_type=F32)


def _dot_nt(a, b):
    return lax.dot_general(a, b, (((1,), (1,)), ((), ())), preferred_element_type=F32)


def _split2(x):
    hi = x.astype(BF16)
    lo = (x - hi.astype(F32)).astype(BF16)
    return hi, lo


def _split3(x):
    hi = x.astype(BF16)
    r = x - hi.astype(F32)
    mid = r.astype(BF16)
    lo = (r - mid.astype(F32)).astype(BF16)
    return hi, mid, lo


def _softplus(x):
    return jnp.maximum(x, 0.0) + jnp.log1p(jnp.exp(-jnp.abs(x)))


def _silu(x):
    return x * (1.0 / (1.0 + jnp.exp(-x)))


def _params(sem):
    return pltpu.CompilerParams(dimension_semantics=sem, vmem_limit_bytes=VMEM_LIMIT)


def _ada_kernel(c_ref, w_ref, b_ref, o_ref):
    sc = _silu(c_ref[...])
    o_ref[...] = _dot(sc.astype(BF16), w_ref[...].astype(BF16)) + b_ref[...]


def _ada(c_pad, w, b):
    rows, d = c_pad.shape
    n = w.shape[1]
    bn = 1024
    return pl.pallas_call(
        _ada_kernel,
        grid=(n // bn,),
        in_specs=[pl.BlockSpec((rows, d), lambda j: (0, 0)),
                  pl.BlockSpec((d, bn), lambda j: (0, j)),
                  pl.BlockSpec((1, bn), lambda j: (0, j))],
        out_specs=pl.BlockSpec((rows, bn), lambda j: (0, j)),
        out_shape=jax.ShapeDtypeStruct((rows, n), F32),
        compiler_params=_params(("arbitrary",)),
        name="ada",
    )(c_pad, w, b)


DT_COL0 = SSD_WIDTH + XBC_WIDTH
GATES_COL0 = DT_COL0 + 2 * SSD_HEADS


def _win_prep_kernel(w_ref, main_ref, dh_ref, dl_ref):
    w = w_ref[...]
    main_ref[:, 0:SSD_WIDTH] = w[:, 0:SSD_WIDTH].astype(BF16)
    main_ref[:, SSD_WIDTH:XBC_COL0] = w[:, GATES_COL0:GATES_COL0 + 3 * SC_WIDTH].astype(BF16)
    main_ref[:, XBC_COL0:MAIN_COLS] = w[:, SSD_WIDTH:DT_COL0].astype(BF16)
    dt = w[:, DT_COL0:DT_COL0 + LANES]
    lane = lax.broadcasted_iota(jnp.int32, dt.shape, 1)
    hi, lo = _split2(jnp.where(lane < 2 * SSD_HEADS, dt, 0.0))
    dh_ref[...] = hi
    dl_ref[...] = lo


def _win_prep(w):
    d, n = w.shape
    rows = 128
    return pl.pallas_call(
        _win_prep_kernel,
        grid=(d // rows,),
        in_specs=[pl.BlockSpec((rows, n), lambda i: (i, 0))],
        out_specs=[pl.BlockSpec((rows, MAIN_COLS), lambda i: (i, 0)),
                   pl.BlockSpec((rows, LANES), lambda i: (i, 0)),
                   pl.BlockSpec((rows, LANES), lambda i: (i, 0))],
        out_shape=[jax.ShapeDtypeStruct((d, MAIN_COLS), BF16), jax.ShapeDtypeStruct((d, LANES), BF16),
                   jax.ShapeDtypeStruct((d, LANES), BF16)],
        compiler_params=_params(("parallel",)),
        name="winprep",
    )(w)


def _inproj_kernel(x_ref, mod_ref, w_ref, wdh_ref, wdl_ref, o_ref, dt_ref, h_scr):
    @pl.when(pl.program_id(1) == 0)
    def _():
        m = mod_ref[0]
        h = x_ref[...] * (1.0 + m[1:2]) + m[0:1]
        hi, lo = _split2(h)
        h_scr[...] = hi
        dt_ref[...] = _dot(hi, wdh_ref[...]) + _dot(lo, wdh_ref[...]) + _dot(hi, wdl_ref[...])

    o_ref[...] = _dot(h_scr[...], w_ref[...]).astype(o_ref.dtype)


def _inproj(x2, mod3, w_main, wd_hi, wd_lo, seq):
    t, d = x2.shape
    n = w_main.shape[1]
    bm = min(1024, seq)
    bn = 1024
    per_batch = seq // bm
    return pl.pallas_call(
        _inproj_kernel,
        grid=(t // bm, n // bn),
        in_specs=[pl.BlockSpec((bm, d), lambda i, j: (i, 0)),
                  pl.BlockSpec((1, 6, d), lambda i, j: (i // per_batch, 0, 0)),
                  pl.BlockSpec((d, bn), lambda i, j: (0, j)),
                  pl.BlockSpec((d, LANES), lambda i, j: (0, 0)),
                  pl.BlockSpec((d, LANES), lambda i, j: (0, 0))],
        out_specs=[pl.BlockSpec((bm, bn), lambda i, j: (i, j)),
                   pl.BlockSpec((bm, LANES), lambda i, j: (i, 0))],
        out_shape=[jax.ShapeDtypeStruct((t, n), BF16),
                   jax.ShapeDtypeStruct((t, LANES), F32)],
        scratch_shapes=[pltpu.VMEM((bm, d), BF16)],
        compiler_params=_params(("parallel", "arbitrary")),
        name="inproj",
    )(x2, mod3, w_main, wd_hi, wd_lo)


def _shifted_rows(u, prev_row, next_row):
    rows = u.shape[0]
    ridx = lax.broadcasted_iota(jnp.int32, u.shape, 0)
    up = jnp.where(ridx == 0, prev_row, pltpu.roll(u, 1, axis=0))
    un = jnp.where(ridx == rows - 1, next_row, pltpu.roll(u, rows - 1, axis=0))
    return up, un


def _halo_specs(bm, bw, col_of, t):
    per = bm // HALO_ROWS
    last = t // HALO_ROWS - 1
    prev = pl.BlockSpec((HALO_ROWS, bw), lambda i, j: (jnp.maximum(i * per - 1, 0), col_of(j)))
    nxt = pl.BlockSpec((HALO_ROWS, bw), lambda i, j: (jnp.minimum((i + 1) * per, last), col_of(j)))
    return prev, nxt


def _conv_kernel(u_ref, p_ref, n_ref, w_ref, b_ref, o_ref, *, bm, seq):
    i = pl.program_id(0)
    u = u_ref[...].astype(F32)
    has_prev = ((i * bm) % seq != 0).astype(F32)
    has_next = (((i + 1) * bm) % seq != 0).astype(F32)
    up, un = _shifted_rows(u, p_ref[HALO_ROWS - 1:HALO_ROWS, :].astype(F32) * has_prev,
                           n_ref[0:1, :].astype(F32) * has_next)
    w = w_ref[...]
    o_ref[...] = _silu(up * w[0:1] + u * w[1:2] + un * w[2:3] + b_ref[...])


def _conv(proj, conv_w, conv_b, seq):
    t = proj.shape[0]
    bm = min(1024, seq)
    bw = 1024
    col0 = XBC_COL0 // bw
    col_of = lambda j: col0 + j
    prev, nxt = _halo_specs(bm, bw, col_of, t)
    return pl.pallas_call(
        functools.partial(_conv_kernel, bm=bm, seq=seq),
        grid=(t // bm, XBC_WIDTH // bw),
        in_specs=[pl.BlockSpec((bm, bw), lambda i, j: (i, col_of(j))), prev, nxt,
                  pl.BlockSpec((3, bw), lambda i, j: (0, j)),
                  pl.BlockSpec((1, bw), lambda i, j: (0, j))],
        out_specs=pl.BlockSpec((bm, bw), lambda i, j: (i, j)),
        out_shape=jax.ShapeDtypeStruct((t, XBC_WIDTH), F32),
        compiler_params=_params(("parallel", "parallel")),
        name="ssdconv",
    )(proj, proj, proj, conv_w, conv_b)


def _ssd_direction(x_ref, b_ref, c_ref, dt_ref, e_ref, h_scr, y_ref, bias, a, dskip, backward):
    cs_len = SSD_CHUNK
    row = lax.broadcasted_iota(jnp.int32, (cs_len, cs_len), 0)
    col = lax.broadcasted_iota(jnp.int32, (cs_len, cs_len), 1)
    tri = (col <= row).astype(BF16)
    dt = _softplus(dt_ref[...] + bias)
    adt = dt * a
    h3 = _split3(adt)
    cs = _dot(tri, h3[0]) + _dot(tri, h3[1]) + _dot(tri, h3[2])
    tot = cs[cs_len - 1:cs_len, :]
    if backward:
        ecs = cs - adt
        p = -ecs
        wst = dt * jnp.exp(ecs)
        indec = jnp.exp(tot - ecs)
        mask = col >= row
    else:
        p = cs
        wst = dt * jnp.exp(tot - cs)
        indec = jnp.exp(cs)
        mask = col <= row
    pt = p.T
    e = e_ref[...]
    dtx = _dot(dt.astype(BF16), e)
    wstx = _dot(wst.astype(BF16), e)
    indx = _dot(indec.astype(BF16), e)
    xs = x_ref[...]
    xd = (xs * dtx).astype(BF16)
    xw = (xs * wstx).astype(BF16)
    col_off = SSD_HEADS if backward else 0
    pair = 2 * SSD_HEADDIM
    first_of_pair = lax.broadcasted_iota(jnp.int32, (cs_len, pair), 1) < SSD_HEADDIM
    zero = jnp.zeros((), BF16)
    for g in range(SSD_GROUPS):
        gs = slice(g * GROUP_WIDTH, (g + 1) * GROUP_WIDTH)
        bg = b_ref[:, g * SSD_STATE:(g + 1) * SSD_STATE]
        cg = c_ref[:, g * SSD_STATE:(g + 1) * SSD_STATE].astype(BF16)
        cb = _dot_nt(cg, bg.astype(BF16))
        h_in = h_scr[g]
        y_off = _dot(cg, h_in.astype(BF16)) * indx[:, gs]
        st = _dot(bg.T.astype(BF16), xw[:, gs])
        for r in range(0, HEADS_PER_GROUP, 2):
            hd = g * HEADS_PER_GROUP + r
            ms = []
            for ci in (hd + col_off, hd + col_off + 1):
                seg = p[:, ci:ci + 1] - pt[ci:ci + 1, :]
                lmat = jnp.exp(jnp.where(mask, seg, -jnp.inf))
                ms.append((cb * lmat).astype(BF16))
            hs = slice(hd * SSD_HEADDIM, (hd + 2) * SSD_HEADDIM)
            x2h = xd[:, hs]
            rhs = jnp.concatenate([jnp.where(first_of_pair, x2h, zero), jnp.where(first_of_pair, zero, x2h)], axis=0)
            y = _dot(jnp.concatenate(ms, axis=1), rhs) + y_off[:, r * SSD_HEADDIM:(r + 2) * SSD_HEADDIM]
            if dskip is not None:
                y = y + xs[:, hs] * dskip[:, hs]
            y_ref[:, hs] = y
        edge = 0 if backward else cs_len - 1
        h_scr[g] = indx[edge:edge + 1, gs] * h_in + st


def _ssd_kernel(xf_ref, bf_ref, cf_ref, dtf_ref, xb_ref, bb_ref, cb_ref, dtb_ref,
                bias_ref, alog_ref, dskip_ref, ef_ref, eb_ref, yf_ref, yb_ref, hf_scr, hb_scr):
    @pl.when(pl.program_id(1) == 0)
    def _():
        hf_scr[...] = jnp.zeros_like(hf_scr)
        hb_scr[...] = jnp.zeros_like(hb_scr)

    bias = bias_ref[...]
    a = -jnp.exp(alog_ref[...])
    _ssd_direction(xf_ref, bf_ref, cf_ref, dtf_ref, ef_ref, hf_scr, yf_ref, bias, a, dskip_ref[...], False)
    _ssd_direction(xb_ref, bb_ref, cb_ref, dtb_ref, eb_ref, hb_scr, yb_ref, bias, a, None, True)


def _ssd(xbc, dt_raw, bias, alog, dskip_x, e_f, e_b, batch, seq):
    t = xbc.shape[0]
    nc = seq // SSD_CHUNK
    cl = SSD_CHUNK
    gn = SSD_GROUPS * SSD_STATE
    bcol = SSD_WIDTH // gn
    fwd = lambda b, k: b * nc + k
    bwd = lambda b, k: b * nc + (nc - 1 - k)

    def specs(ch):
        return [pl.BlockSpec((cl, SSD_WIDTH), lambda b, k: (ch(b, k), 0)),
                pl.BlockSpec((cl, gn), lambda b, k: (ch(b, k), bcol)),
                pl.BlockSpec((cl, gn), lambda b, k: (ch(b, k), bcol + 1)),
                pl.BlockSpec((cl, LANES), lambda b, k: (ch(b, k), 0))]

    const = lambda shape: pl.BlockSpec(shape, lambda b, k: (0, 0))
    state = pltpu.VMEM((SSD_GROUPS, SSD_STATE, GROUP_WIDTH), F32)
    return pl.pallas_call(
        _ssd_kernel,
        grid=(batch, nc),
        in_specs=specs(fwd) + specs(bwd) + [const((1, LANES)), const((1, LANES)), const((1, SSD_WIDTH)),
                                            const((LANES, SSD_WIDTH)), const((LANES, SSD_WIDTH))],
        out_specs=[pl.BlockSpec((cl, SSD_WIDTH), lambda b, k: (fwd(b, k), 0)),
                   pl.BlockSpec((cl, SSD_WIDTH), lambda b, k: (bwd(b, k), 0))],
        out_shape=[jax.ShapeDtypeStruct((t, SSD_WIDTH), F32)] * 2,
        scratch_shapes=[state, state],
        compiler_params=_params(("parallel", "arbitrary")),
        name="ssd",
    )(xbc, xbc, xbc, dt_raw, xbc, xbc, xbc, dt_raw, bias, alog, dskip_x, e_f, e_b)


def _group_rms(y, r_ref, e_ref, group_size):
    gsum = _dot((y * y).astype(BF16), r_ref[...])
    inv = lax.rsqrt(gsum * (1.0 / group_size) + NORM_EPS)
    return y * _dot(inv.astype(BF16), e_ref[...])


def _layer_norm(v, g, b):
    mu = jnp.mean(v, axis=-1, keepdims=True)
    vc = v - mu
    var = jnp.mean(vc * vc, axis=-1, keepdims=True)
    return vc * lax.rsqrt(var + NORM_EPS) * g + b


def _post_kernel(yf_ref, yb_ref, z_ref, gb_ref, gc_ref, gcp_ref, gcn_ref, v_ref, vp_ref, vn_ref,
                 x_ref, mod_ref, scw_ref, ssdg_ref, scg_ref, r4_ref, e4_ref, r32_ref, e32_ref,
                 wout_ref, lng_ref, lnb_ref, x1_ref, h2_ref, amax_ref, sumsq_ref, *, bm, seq, alpha):
    i = pl.program_id(0)
    m = mod_ref[0]
    y = (yf_ref[...] + yb_ref[...]) * _silu(z_ref[...].astype(F32))
    y_ssd = (_group_rms(y, r4_ref, e4_ref, SSD_WIDTH // SSD_GROUPS) * ssdg_ref[...]).astype(BF16)
    n_cols = [slice(c, c + POST_COL_PIECE) for c in range(0, wout_ref.shape[1], POST_COL_PIECE)]
    mix_ssd = [_dot(y_ssd, wout_ref[0:SSD_WIDTH, cs]) for cs in n_cols]

    f32 = lambda ref, rows=slice(None): ref[rows, :].astype(F32)
    u = f32(gc_ref) * f32(v_ref)
    has_prev = ((i * bm) % seq != 0).astype(F32)
    has_next = (((i + 1) * bm) % seq != 0).astype(F32)
    last = slice(HALO_ROWS - 1, HALO_ROWS)
    first = slice(0, 1)
    up, un = _shifted_rows(u, f32(gcp_ref, last) * f32(vp_ref, last) * has_prev,
                           f32(gcn_ref, first) * f32(vn_ref, first) * has_next)
    w = scw_ref[...]
    y_sc = f32(gb_ref) * (up * w[0:1] + u * w[1:2] + un * w[2:3])
    y_sc = (_group_rms(y_sc, r32_ref, e32_ref, SC_WIDTH // SC_GROUPS) * scg_ref[...]).astype(BF16)
    mix = jnp.concatenate([part + _dot(y_sc, wout_ref[SSD_WIDTH:SSD_WIDTH + SC_WIDTH, cs])
                           for part, cs in zip(mix_ssd, n_cols)], axis=1)
    x1 = _layer_norm(alpha * x_ref[...] + m[2:3] * mix, lng_ref[...], lnb_ref[...])
    x1_ref[...] = x1
    h2 = x1 * (1.0 + m[4:5]) + m[3:4]
    h2_ref[...] = h2.astype(BF16)
    amax_ref[...] = jnp.max(jnp.abs(h2), axis=-1, keepdims=True)
    sumsq_ref[...] = jnp.sum(h2 * h2, axis=-1, keepdims=True)


def _single(shape, index_map):
    return pl.BlockSpec(shape, index_map, pipeline_mode=pl.Buffered(1))


def _post(yf, yb, proj, x2, mod3, scw, ssdg, scg, r4, e4, r32, e32, wout, lng, lnb, seq, alpha):
    t, d = x2.shape
    bm = min(256, seq)
    per_batch = seq // bm
    per = bm // HALO_ROWS
    last = t // HALO_ROWS - 1
    main = lambda c: pl.BlockSpec((bm, d), lambda i: (i, c))
    prev = lambda c: pl.BlockSpec((HALO_ROWS, d), lambda i: (jnp.maximum(i * per - 1, 0), c))
    nxt = lambda c: pl.BlockSpec((HALO_ROWS, d), lambda i: (jnp.minimum((i + 1) * per, last), c))
    const = lambda shape: _single(shape, lambda i: (0, 0))
    return pl.pallas_call(
        functools.partial(_post_kernel, bm=bm, seq=seq, alpha=alpha),
        grid=(t // bm,),
        in_specs=[main(0), main(0), main(0), main(1), main(2), prev(2), nxt(2), main(3), prev(3), nxt(3),
                  main(0), pl.BlockSpec((1, 6, d), lambda i: (i // per_batch, 0, 0)),
                  const((3, d)), const((1, d)), const((1, d)),
                  const((d, LANES)), const((LANES, d)), const((d, LANES)), const((LANES, d)),
                  const((2 * d, d)), const((1, d)), const((1, d))],
        out_specs=[pl.BlockSpec((bm, d), lambda i: (i, 0)), pl.BlockSpec((bm, d), lambda i: (i, 0)),
                   pl.BlockSpec((bm, 1), lambda i: (i, 0)), pl.BlockSpec((bm, 1), lambda i: (i, 0))],
        out_shape=[jax.ShapeDtypeStruct((t, d), F32), jax.ShapeDtypeStruct((t, d), BF16),
                   jax.ShapeDtypeStruct((t, 1), F32), jax.ShapeDtypeStruct((t, 1), F32)],
        compiler_params=_params(("parallel",)),
        name="post",
    )(yf, yb, proj, proj, proj, proj, proj, proj, proj, proj, x2, mod3,
      scw, ssdg, scg, r4, e4, r32, e32, wout, lng, lnb)


def _top_values(s, n, want_rank=False):
    vals = []
    rank = jnp.full(s.shape, float(n), F32) if want_rank else None
    for r in range(n):
        m = jnp.max(s, axis=0, keepdims=True)
        vals.append(m)
        hit = s == m
        if want_rank:
            rank = jnp.where(hit, float(r), rank)
        s = jnp.where(hit, -jnp.inf, s)
    return (vals, rank) if want_rank else vals


def _query_kernel(h2_ref, wq_ref, k_ref, r2_ref, e2_ref, n1_ref, c1_ref):
    tb = h2_ref.shape[0]
    for h in range(PEER_HEADS):
        q = _dot(h2_ref[...], wq_ref[:, h * D_QUERY:(h + 1) * D_QUERY]).astype(BF16)
        s1 = _dot_nt(k_ref[h, 0], q[:, :HALF_QUERY])
        s2 = _dot_nt(k_ref[h, 1], q[:, HALF_QUERY:])
        v1 = _top_values(s1, PEER_TOPK)
        v2, r2 = _top_values(s2, PEER_TOPK, want_rank=True)
        pairs = [(a, b) for a in range(PEER_TOPK) for b in range(PEER_TOPK) if (a + 1) * (b + 1) <= PEER_TOPK]
        sums = {ab: v1[ab[0]] + v2[ab[1]] for ab in pairs}
        rows = [sums[ab] for ab in pairs]
        rows += [jnp.full((1, tb), -jnp.inf, F32)] * ((-len(rows)) % SUBLANES)
        top = _top_values(jnp.concatenate(rows, axis=0), PEER_TOPK)
        z = jnp.ones((1, tb), F32)
        for kk in range(1, PEER_TOPK):
            z = z + jnp.exp(top[kk] - top[0])
        tau = top[PEER_TOPK - 1]
        n1 = jnp.zeros(s1.shape, F32)
        for a in range(PEER_TOPK):
            cnt = sum(jnp.where(sums[(a, b)] >= tau, 1.0, 0.0) for b in range(PEER_TOPK) if (a, b) in sums)
            n1 = jnp.where(s1 == v1[a], cnt, n1)
        r2_ref[h] = r2.astype(BF16)
        e2_ref[h] = jnp.exp(s2 - v2[0]).astype(BF16)
        n1_ref[h] = n1
        c1_ref[h] = jnp.exp(s1 - v1[0]) * (1.0 / z)


def _query(h2, wq, keys):
    t, d = h2.shape
    tb = min(256, t)
    ospec = pl.BlockSpec((PEER_HEADS, N_KEYS, tb), lambda i: (0, 0, i))
    return pl.pallas_call(
        _query_kernel,
        grid=(t // tb,),
        in_specs=[pl.BlockSpec((tb, d), lambda i: (i, 0)),
                  _single((d, PEER_HEADS * D_QUERY), lambda i: (0, 0)),
                  _single((PEER_HEADS, 2, N_KEYS, HALF_QUERY), lambda i: (0, 0, 0, 0))],
        out_specs=[ospec] * 4,
        out_shape=[jax.ShapeDtypeStruct((PEER_HEADS, N_KEYS, t), dt) for dt in (BF16, BF16, F32, F32)],
        compiler_params=_params(("parallel",)),
        name="peerquery",
    )(h2, wq, keys)


def _pow2_scale(bound):
    ratio = FP8_TARGET / jnp.where(bound > 0, bound, FP8_TARGET)
    exponent_only = lax.bitcast_convert_type(ratio, jnp.int32) & jnp.int32(0x7F800000)
    return jnp.minimum(lax.bitcast_convert_type(exponent_only, F32), FP8_SCALE_CAP)


def _expert_prep_kernel(w_ref, q_ref, inv_ref, sumsq_ref, *, transpose):
    w = w_ref[...]
    s = _pow2_scale(jnp.max(jnp.abs(w), axis=1, keepdims=True))
    inv_ref[...] = 1.0 / s
    sumsq_ref[...] = jnp.sum(w * w, axis=1, keepdims=True)
    ws = w * s
    q_ref[...] = (ws.T if transpose else ws).astype(FP8)


def _expert_prep(table, transpose):
    n_exp, d = table.shape
    rows = 1024
    col = pl.BlockSpec((rows, 1), lambda i: (i, 0))
    q_spec = pl.BlockSpec((d, rows), lambda i: (0, i)) if transpose else pl.BlockSpec((rows, d), lambda i: (i, 0))
    q_shape = (d, n_exp) if transpose else (n_exp, d)
    return pl.pallas_call(
        functools.partial(_expert_prep_kernel, transpose=transpose),
        grid=(n_exp // rows,),
        in_specs=[pl.BlockSpec((rows, d), lambda i: (i, 0))],
        out_specs=[q_spec, col, col],
        out_shape=[jax.ShapeDtypeStruct(q_shape, FP8), jax.ShapeDtypeStruct((n_exp, 1), F32),
                   jax.ShapeDtypeStruct((n_exp, 1), F32)],
        compiler_params=_params(("parallel",)),
        name="expertprep_t" if transpose else "expertprep",
    )(table)


def _peer_kernel(h2_ref, u_ref, vt_ref, r2_ref, e2_ref, n1_ref, c1_ref, x1_ref, mod_ref, lng_ref, lnb_ref,
                 iu_ref, iv_ref, ish_ref, sp_ref, sh_ref, isp_ref, o_ref, acc_ref, h8_ref, *, alpha):
    e = pl.program_id(1)

    @pl.when(e == 0)
    def _():
        acc_ref[...] = jnp.zeros_like(acc_ref)
        h8_ref[...] = (h2_ref[...].astype(F32) * sh_ref[...]).T.astype(FP8)

    n1 = n1_ref[...].astype(BF16)
    c1 = c1_ref[...].astype(BF16)
    parts = []
    for ii in range(u_ref.shape[0] // N_KEYS):
        rows = slice(ii * N_KEYS, (ii + 1) * N_KEYS)
        at = _dot(u_ref[rows, :], h8_ref[...])
        g = None
        for h in range(PEER_HEADS):
            w = jnp.where(r2_ref[h] < n1[h, ii:ii + 1, :], e2_ref[h], jnp.zeros((), BF16)) * c1[h, ii:ii + 1, :]
            g = w if g is None else g + w
        iu_col = jnp.transpose(iu_ref[...])[:, ii:ii + 1]
        iv_col = jnp.transpose(iv_ref[...])[:, ii:ii + 1]
        b = at * (iu_col * ish_ref[...])
        p = b * (1.0 + lax.erf(b)) * (iv_col * sp_ref[...])
        parts.append((p.astype(BF16) * g).astype(FP8))
    pt = jnp.concatenate(parts, axis=0)
    for c in range(0, vt_ref.shape[0], PEER_OUT_PIECE):
        dr = slice(c, c + PEER_OUT_PIECE)
        acc_ref[dr, :] += _dot(vt_ref[dr, :], pt)

    @pl.when(e == pl.num_programs(1) - 1)
    def _():
        m = mod_ref[0]
        ffn = acc_ref[...].T * isp_ref[...]
        o_ref[...] = _layer_norm(alpha * x1_ref[...] + m[5:6] * ffn, lng_ref[...], lnb_ref[...])


def _peer(h2, u8, vt8, r2, e2, n1, c1, x1, mod3, lng, lnb, iu, iv, ish_blk, sp_blk, sh, isp, seq, alpha):
    t, d = h2.shape
    n_exp = u8.shape[0]
    tb = min(PEER_TOKEN_BLOCK, seq)
    eb = 1024
    per_batch = seq // tb
    kb = eb // N_KEYS
    full = pl.BlockSpec((PEER_HEADS, N_KEYS, tb), lambda i, j: (0, 0, i))
    part = pl.BlockSpec((PEER_HEADS, kb, tb), lambda i, j: (0, j, i))
    ecol = pl.BlockSpec((kb, N_KEYS), lambda i, j: (j, 0))
    blk = pl.BlockSpec((None, 1, 1), lambda i, j: (i, 0, 0))
    tcol = pl.BlockSpec((tb, 1), lambda i, j: (i, 0))
    return pl.pallas_call(
        functools.partial(_peer_kernel, alpha=alpha),
        grid=(t // tb, n_exp // eb),
        in_specs=[pl.BlockSpec((tb, d), lambda i, j: (i, 0)),
                  pl.BlockSpec((eb, d), lambda i, j: (j, 0)),
                  pl.BlockSpec((d, eb), lambda i, j: (0, j)),
                  full, full, part, part,
                  pl.BlockSpec((tb, d), lambda i, j: (i, 0)),
                  pl.BlockSpec((1, 6, d), lambda i, j: (i // per_batch, 0, 0)),
                  pl.BlockSpec((1, d), lambda i, j: (0, 0)),
                  pl.BlockSpec((1, d), lambda i, j: (0, 0)),
                  ecol, ecol, blk, blk, tcol, tcol],
        out_specs=pl.BlockSpec((tb, d), lambda i, j: (i, 0)),
        out_shape=jax.ShapeDtypeStruct((t, d), F32),
        scratch_shapes=[pltpu.VMEM((d, tb), F32), pltpu.VMEM((d, tb), FP8)],
        compiler_params=_params(("parallel", "arbitrary")),
        name="peer",
    )(h2, u8, vt8, r2, e2, n1, c1, x1, mod3, lng, lnb, iu, iv, ish_blk, sp_blk, sh, isp)


def _ffn(h2, h_amax, h_sumsq, x1, mod3, w_query, sub_keys, expert_u, expert_v, ln2_g, ln2_b, seq, alpha):
    t = h2.shape[0]
    r2, e2, n1, c1 = _query(h2, w_query.astype(BF16), sub_keys.astype(BF16))
    u8, inv_su, u_sumsq = _expert_prep(expert_u, transpose=False)
    vt8, inv_sv, _ = _expert_prep(expert_v, transpose=True)
    tb = min(PEER_TOKEN_BLOCK, seq)
    per_block = lambda v: jnp.max(v.reshape(t // tb, tb), axis=1).reshape(-1, 1, 1)
    sh = _pow2_scale(per_block(h_amax))
    sp = _pow2_scale(jnp.sqrt(jnp.max(u_sumsq) * per_block(h_sumsq)) * (PEER_HEADS * jnp.max(inv_sv)))
    sqrt_half = np.float32(math.sqrt(0.5))
    per_token = lambda s: jnp.broadcast_to(s, (t // tb, tb, 1)).reshape(t, 1)
    by_key_block = lambda s: (s * sqrt_half).reshape(-1, N_KEYS)
    return _peer(h2, u8, vt8, r2, e2, n1, c1, x1, mod3, _row(ln2_g), _row(ln2_b), by_key_block(inv_su),
                 by_key_block(inv_sv), 1.0 / sh, sp, per_token(sh), per_token(1.0 / sp), seq, alpha)


def _one_hot_cols(n_rows, n_cols, group_size, row_offset=0):
    r = np.arange(n_rows)[:, None]
    c = np.arange(n_cols)[None, :]
    return jnp.asarray(r == row_offset + c // group_size, dtype=BF16)


def _row(v):
    return v.reshape(1, -1).astype(F32)


def _pad_lanes(v):
    return jnp.pad(v, (0, LANES - v.shape[0])).reshape(1, LANES).astype(F32)


def kernel(x, c, w_ada, b_ada, w_in, conv_ssd_w, conv_ssd_b, dt_bias_f, dt_bias_b, a_log_f, a_log_b, d_skip, ssd_norm_g, short_conv_w, sc_norm_g, w_out, ln1_g, ln1_b, w_query, sub_keys, expert_u, expert_v, ln2_g, ln2_b):
    batch, seq, d = x.shape
    depth = w_ada.shape[0]
    alpha = (2.0 * depth) ** 0.25
    t = batch * seq
    x2 = x.reshape(t, d)
    c_pad = jnp.pad(c, ((0, SUBLANES - batch % SUBLANES), (0, 0))) if batch % SUBLANES else c

    e_f = _one_hot_cols(LANES, SSD_WIDTH, SSD_HEADDIM, 0)
    e_b = _one_hot_cols(LANES, SSD_WIDTH, SSD_HEADDIM, SSD_HEADS)
    e4 = _one_hot_cols(LANES, SSD_WIDTH, SSD_WIDTH // SSD_GROUPS)
    e32 = _one_hot_cols(LANES, SC_WIDTH, SC_WIDTH // SC_GROUPS)
    r4 = e4.T
    r32 = e32.T

    for i in range(depth):
        mod = _ada(c_pad, w_ada[i], b_ada[i].reshape(1, -1))
        mod3 = mod[:batch].reshape(batch, 6, d)

        w_main, wd_hi, wd_lo = _win_prep(w_in[i])
        proj, dt_raw = _inproj(x2, mod3, w_main, wd_hi, wd_lo, seq)

        xbc = _conv(proj, conv_ssd_w[i], conv_ssd_b[i].reshape(1, -1), seq)
        bias = _pad_lanes(jnp.concatenate([dt_bias_f[i], dt_bias_b[i]]))
        alog = _pad_lanes(jnp.concatenate([a_log_f[i], a_log_b[i]]))
        dskip_x = _row(jnp.repeat(d_skip[i], SSD_HEADDIM))
        y_f, y_b = _ssd(xbc, dt_raw, bias, alog, dskip_x, e_f, e_b, batch, seq)

        x1, h2, h_amax, h_sumsq = _post(y_f, y_b, proj, x2, mod3, short_conv_w[i], _row(ssd_norm_g[i]),
                                        _row(sc_norm_g[i]), r4, e4, r32, e32, w_out[i].astype(BF16),
                                        _row(ln1_g[i]), _row(ln1_b[i]), seq, alpha)

        x2 = _ffn(h2, h_amax, h_sumsq, x1, mod3, w_query[i], sub_keys[i], expert_u[i], expert_v[i],
                  ln2_g[i], ln2_b[i], seq, alpha)
    return x2.reshape(batch, seq, d)
```
